```python
import math
import jax, jax.numpy as jnp
from jax import lax
import numpy as np

D_MODEL = 1024
BATCH = 16
SEQ = 2048
DEPTH = 4

CHUNK = 64
Q_BLOCK = 128
MLA_WIDTH = D_MODEL // 2
MLA_V_DIM = 64
MLA_HEADS = MLA_WIDTH // MLA_V_DIM
MLA_QK_NOPE = 64
MLA_QK_ROPE = 32
MLA_Q_RANK = D_MODEL // 4
MLA_KV_RANK = D_MODEL // 8
ROPE_THETA = 10000.0
GDN_WIDTH = D_MODEL // 4
GDN_HEAD_DIM = 64
GDN_HEADS = GDN_WIDTH // GDN_HEAD_DIM
GDN_CONV = 4
POOL_WIDTH = D_MODEL - MLA_WIDTH - GDN_WIDTH
POOL_WINDOWS = (2, 4, 8, 16)
POOL_GROUPS = 4
POOL_GROUP_DIM = POOL_WIDTH // POOL_GROUPS
MIX_WIDTH = MLA_WIDTH + GDN_WIDTH + POOL_WIDTH
IN_SIZES = (MLA_Q_RANK, MLA_KV_RANK, MLA_QK_ROPE, GDN_WIDTH, GDN_WIDTH, GDN_WIDTH, GDN_WIDTH, GDN_HEADS, GDN_HEADS, POOL_WIDTH)
IN_PROJ_DIM = sum(IN_SIZES)
N_GROUPS = 4
EXPERTS_PER_GROUP = 8
N_EXPERTS = N_GROUPS * EXPERTS_PER_GROUP
TOP_K = 2
EXPERT_FF = D_MODEL // 4
MOE_BLOCK = 128
DEEPNORM_ALPHA = (2 * DEPTH) ** 0.25
DEEPNORM_BETA = (8 * DEPTH) ** -0.25
LN_EPS = 1e-5
RMS_EPS = 1e-6

kernel_name = 'hybrid_mla_gdn_pool_hmoe_deepnorm'

F32 = jnp.float32


def layer_norm(x, g, b):
    xf = x.astype(F32)
    mu = jnp.mean(xf, axis=-1, keepdims=True)
    var = jnp.mean(jnp.square(xf - mu), axis=-1, keepdims=True)
    return ((xf - mu) * lax.rsqrt(var + LN_EPS)).astype(x.dtype) * g + b


def rms_norm(x, g):
    xf = x.astype(F32)
    return (xf * lax.rsqrt(jnp.mean(xf * xf, axis=-1, keepdims=True) + RMS_EPS)).astype(x.dtype) * g


def l2_norm(x):
    xf = x.astype(F32)
    return xf * lax.rsqrt(jnp.sum(xf * xf, axis=-1, keepdims=True) + RMS_EPS)


def split_sizes(x, sizes):
    offs, acc = [], 0
    for s in sizes[:-1]:
        acc += s
        offs.append(acc)
    return jnp.split(x, offs, axis=-1)


def rope_tables(seq):
    inv_freq = jnp.power(ROPE_THETA, -jnp.arange(0, MLA_QK_ROPE, 2, dtype=F32) / MLA_QK_ROPE)
    ang = jnp.arange(seq, dtype=F32)[:, None] * inv_freq[None, :]
    return jnp.cos(ang), jnp.sin(ang)


def apply_rope(x, cos, sin):
    xf = x.astype(F32)
    x1, x2 = jnp.split(xf, 2, axis=-1)
    return jnp.concatenate([x1 * cos - x2 * sin, x2 * cos + x1 * sin], axis=-1).astype(x.dtype)


def mla_attention(c_q, c_kv, k_rope, q_norm_g, kv_norm_g, w_uq, w_ukv, cos, sin):
    B, S, _ = c_q.shape
    q = (rms_norm(c_q, q_norm_g) @ w_uq).reshape(B, S, MLA_HEADS, MLA_QK_NOPE + MLA_QK_ROPE)
    q_nope, q_rope = q[..., :MLA_QK_NOPE], q[..., MLA_QK_NOPE:]
    q_rope = apply_rope(q_rope, cos[:, None, :], sin[:, None, :])
    kv = (rms_norm(c_kv, kv_norm_g) @ w_ukv).reshape(B, S, MLA_HEADS, MLA_QK_NOPE + MLA_V_DIM)
    k_nope, v = kv[..., :MLA_QK_NOPE], kv[..., MLA_QK_NOPE:]
    k_rope = apply_rope(k_rope, cos, sin)
    scale = (MLA_QK_NOPE + MLA_QK_ROPE) ** -0.5
    outs = []
    for qb in range(S // Q_BLOCK):
        q0, kend = qb * Q_BLOCK, (qb + 1) * Q_BLOCK
        s = (jnp.einsum('bqhd,bkhd->bhqk', q_nope[:, q0:kend], k_nope[:, :kend])
             + jnp.einsum('bqhd,bkd->bhqk', q_rope[:, q0:kend], k_rope[:, :kend]))
        s = s.astype(F32) * scale
        q_chunk = jnp.arange(q0, kend) // CHUNK
        k_chunk = jnp.arange(kend) // CHUNK
        allowed = k_chunk[None, :] <= q_chunk[:, None]
        p = jax.nn.softmax(jnp.where(allowed, s, -jnp.inf), axis=-1).astype(v.dtype)
        outs.append(jnp.einsum('bhqk,bkhd->bqhd', p, v[:, :kend]))
    return jnp.concatenate(outs, axis=1).reshape(B, S, MLA_WIDTH)


def causal_depthwise_conv(x, w):
    C = x.shape[-1]
    return lax.conv_general_dilated(x, w[:, None, :].astype(x.dtype), window_strides=(1,),
                                    padding=[(GDN_CONV - 1, 0)],
                                    dimension_numbers=('NWC', 'WIO', 'NWC'),
                                    feature_group_count=C)


def gated_delta_rule(q, k, v, g, beta):
    B, S, H, dk = q.shape
    dv = v.shape[-1]
    nc = S // CHUNK

    def to_chunks(t):
        t = t.astype(F32).reshape((B, nc, CHUNK, H) + t.shape[3:])
        return jnp.moveaxis(t, 3, 1)

    qc = to_chunks(q) * dk ** -0.5
    kc, vc = to_chunks(k), to_chunks(v)
    gc = jnp.cumsum(to_chunks(g), axis=-1)
    bc = to_chunks(beta)
    idx = jnp.arange(CHUNK)
    incl = idx[:, None] >= idx[None, :]
    strict = idx[:, None] > idx[None, :]
    diff = gc[..., :, None] - gc[..., None, :]
    decay = jnp.where(incl, jnp.exp(jnp.where(incl, diff, 0.0)), 0.0)
    kb = kc * bc[..., None]
    a_mat = jnp.where(strict, jnp.einsum('bhnid,bhnjd->bhnij', kb, kc) * decay, 0.0)
    lower = a_mat + jnp.eye(CHUNK, dtype=F32)
    rhs = jnp.concatenate([vc * bc[..., None], kb * jnp.exp(gc)[..., None]], axis=-1)
    sol = lax.linalg.triangular_solve(lower, rhs, left_side=True, lower=True, unit_diagonal=True)
    u, w = sol[..., :dv], sol[..., dv:]
    qk = jnp.where(incl, jnp.einsum('bhnid,bhnjd->bhnij', qc, kc) * decay, 0.0)
    q_dec = qc * jnp.exp(gc)[..., None]
    k_dec = kc * jnp.exp(gc[..., -1:] - gc)[..., None]
    g_last = jnp.exp(gc[..., -1])

    def step(state, inp):
        q_i, k_i, u_i, w_i, qk_i, gl_i = inp
        v_new = u_i - jnp.einsum('bhck,bhkv->bhcv', w_i, state)
        o_i = jnp.einsum('bhck,bhkv->bhcv', q_i, state) + jnp.einsum('bhij,bhjv->bhiv', qk_i, v_new)
        state = state * gl_i[..., None, None] + jnp.einsum('bhck,bhcv->bhkv', k_i, v_new)
        return state, o_i

    xs = tuple(jnp.moveaxis(t, 2, 0) for t in (q_dec, k_dec, u, w, qk, g_last))
    _, o = lax.scan(step, jnp.zeros((B, H, dk, dv), F32), xs)
    o = jnp.moveaxis(o, 0, 2).reshape(B, H, S, dv)
    return jnp.transpose(o, (0, 2, 1, 3))


def gdn_mixer(q, k, v, z, a, b, conv_w, a_log, dt_bias, out_g):
    B, S, _ = q.shape
    qkv = jax.nn.silu(causal_depthwise_conv(jnp.concatenate([q, k, v], axis=-1), conv_w))
    q, k, v = jnp.split(qkv, 3, axis=-1)
    shp = (B, S, GDN_HEADS, GDN_HEAD_DIM)
    q, k, v = l2_norm(q.reshape(shp)), l2_norm(k.reshape(shp)), v.reshape(shp)
    g = -jnp.exp(a_log.astype(F32)) * jax.nn.softplus(a.astype(F32) + dt_bias.astype(F32))
    beta = jax.nn.sigmoid(b.astype(F32))
    o = gated_delta_rule(q, k, v, g, beta).astype(z.dtype)
    o = rms_norm(o, out_g) * jax.nn.silu(z.reshape(shp))
    return o.reshape(B, S, GDN_WIDTH)


def pool_mixer(p, pool_w, pool_scale):
    B, S, _ = p.shape
    pg = p.reshape(B, S, POOL_GROUPS, POOL_GROUP_DIM).astype(F32)
    cs = jnp.cumsum(pg, axis=1)
    cs = jnp.concatenate([jnp.zeros_like(cs[:, :1]), cs], axis=1)
    t = jnp.arange(S)
    means = []
    for gi, win in enumerate(POOL_WINDOWS):
        start = jnp.maximum(t + 1 - win, 0)
        cnt = (t + 1 - start).astype(F32)
        cs_g = cs[:, :, gi, :]
        means.append((cs_g[:, 1:] - cs_g[:, start]) / cnt[None, :, None])
    delta = (jnp.stack(means, axis=2) - pg).astype(p.dtype)
    mixed = jnp.einsum('bsgc,gcd->bsgd', delta, pool_w)
    return mixed.reshape(B, S, POOL_WIDTH) * pool_scale


def hybrid_mixer(h, w_in, q_norm_g, kv_norm_g, w_uq, w_ukv, conv_w, a_log, dt_bias, gdn_g,
                 pool_w, pool_scale, w_out, cos, sin):
    proj = h @ w_in
    c_q, c_kv, k_rope, gq, gk, gv, gz, ga, gb, p = split_sizes(proj, IN_SIZES)
    y_mla = mla_attention(c_q, c_kv, k_rope, q_norm_g, kv_norm_g, w_uq, w_ukv, cos, sin)
    y_gdn = gdn_mixer(gq, gk, gv, gz, ga, gb, conv_w, a_log, dt_bias, gdn_g)
    y_pool = pool_mixer(p, pool_w, pool_scale)
    return jnp.concatenate([y_mla, y_gdn, y_pool], axis=-1) @ w_out


def grouped_expert_ffn(hf, expert, weights, w_gate, w_up, w_down):
    T, D = hf.shape
    A = T * TOP_K
    flat_e = expert.reshape(A)
    flat_tok = jnp.arange(A, dtype=jnp.int32) // TOP_K
    flat_w = weights.reshape(A).astype(hf.dtype)
    order = jnp.argsort(flat_e)
    se = flat_e[order]
    counts = jax.ops.segment_sum(jnp.ones((A,), jnp.int32), flat_e, num_segments=N_EXPERTS)
    padded = (counts + MOE_BLOCK - 1) // MOE_BLOCK * MOE_BLOCK
    pad_end = jnp.cumsum(padded)
    pad_start = pad_end - padded
    start = jnp.cumsum(counts) - counts
    dest = pad_start[se] + (jnp.arange(A, dtype=jnp.int32) - start[se])
    nb = (A + N_EXPERTS * (MOE_BLOCK - 1) + MOE_BLOCK - 1) // MOE_BLOCK
    P = nb * MOE_BLOCK
    buf_tok = jnp.full((P,), T, jnp.int32).at[dest].set(flat_tok[order])
    buf_w = jnp.zeros((P,), hf.dtype).at[dest].set(flat_w[order])
    block_expert = jnp.clip(jnp.searchsorted(pad_end, jnp.arange(nb) * MOE_BLOCK, side='right'), 0, N_EXPERTS - 1)
    xpad = jnp.concatenate([hf, jnp.zeros((1, D), hf.dtype)], axis=0)
    xb = xpad[buf_tok].reshape(nb, MOE_BLOCK, D)

    def block_ffn(args):
        xb_i, e = args
        return (jax.nn.silu(xb_i @ w_gate[e]) * (xb_i @ w_up[e])) @ w_down[e]

    yb = lax.map(block_ffn, (xb, block_expert)).reshape(P, D)
    out = jnp.zeros((T + 1, D), yb.dtype).at[buf_tok].add(yb * buf_w[:, None])
    return out[:T]


def hierarchical_moe(h, w_rg, b_rg, w_re, b_re, w_gate, w_up, w_down):
    B, S, D = h.shape
    hf = h.reshape(B * S, D)
    g_prob = jax.nn.softmax((hf @ w_rg + b_rg).astype(F32), axis=-1)
    g_p, g_sel = lax.top_k(g_prob, 1)
    e_logit = (hf @ w_re + b_re).astype(F32).reshape(-1, N_GROUPS, EXPERTS_PER_GROUP)
    e_logit = jnp.take_along_axis(e_logit, g_sel[:, :, None], axis=1)[:, 0]
    e_p, e_idx = lax.top_k(jax.nn.softmax(e_logit, axis=-1), TOP_K)
    weights = g_p * e_p / jnp.sum(e_p, axis=-1, keepdims=True)
    expert = g_sel * EXPERTS_PER_GROUP + e_idx
    return grouped_expert_ffn(hf, expert, weights, w_gate, w_up, w_down).reshape(B, S, D)


def setup_inputs(seed: int = 0) -> dict:
    key = jax.random.key(seed)
    ks = jax.random.split(key, 32)
    L, D = DEPTH, D_MODEL

    def nrm(k, shape, scale):
        return jax.random.normal(k, shape, F32) * scale

    dt = jnp.exp(jax.random.uniform(ks[9], (L, GDN_HEADS), F32, math.log(1e-3), math.log(1e-1)))
    return {
        'x': nrm(ks[0], (BATCH, SEQ, D), 1.0),
        'c': nrm(ks[1], (BATCH, D), 1.0),
        'w_in': nrm(ks[2], (L, D, IN_PROJ_DIM), D ** -0.5),
        'mla_q_norm': 1.0 + nrm(ks[3], (L, MLA_Q_RANK), 0.05),
        'mla_kv_norm': 1.0 + nrm(ks[4], (L, MLA_KV_RANK), 0.05),
        'mla_w_uq': nrm(ks[5], (L, MLA_Q_RANK, MLA_HEADS * (MLA_QK_NOPE + MLA_QK_ROPE)), MLA_Q_RANK ** -0.5),
        'mla_w_ukv': nrm(ks[6], (L, MLA_KV_RANK, MLA_HEADS * (MLA_QK_NOPE + MLA_V_DIM)), MLA_KV_RANK ** -0.5),
        'gdn_conv': nrm(ks[7], (L, GDN_CONV, 3 * GDN_WIDTH), GDN_CONV ** -0.5),
        'gdn_a_log': jnp.log(jax.random.uniform(ks[8], (L, GDN_HEADS), F32, 1.0, 16.0)),
        'gdn_dt_bias': dt + jnp.log(-jnp.expm1(-dt)),
        'gdn_out_norm': 1.0 + nrm(ks[10], (L, GDN_HEAD_DIM), 0.05),
        'pool_w': nrm(ks[11], (L, POOL_GROUPS, POOL_GROUP_DIM, POOL_GROUP_DIM), POOL_GROUP_DIM ** -0.5),
        'pool_scale': 1.0 + nrm(ks[12], (L, POOL_WIDTH), 0.05),
        'w_out': nrm(ks[13], (L, MIX_WIDTH, D), MIX_WIDTH ** -0.5 * DEEPNORM_BETA),
        'w_mod': nrm(ks[14], (L, D, 6 * D), 0.2 * D ** -0.5),
        'b_mod': nrm(ks[15], (L, 6 * D), 0.02),
        'ln1_g': 1.0 + nrm(ks[16], (L, D), 0.05),
        'ln1_b': nrm(ks[17], (L, D), 0.02),
        'ln2_g': 1.0 + nrm(ks[18], (L, D), 0.05),
        'ln2_b': nrm(ks[19], (L, D), 0.02),
        'router_w_group': nrm(ks[20], (L, D, N_GROUPS), D ** -0.5),
        'router_b_group': nrm(ks[21], (L, N_GROUPS), 0.01),
        'router_w_expert': nrm(ks[22], (L, D, N_EXPERTS), D ** -0.5),
        'router_b_expert': nrm(ks[23], (L, N_EXPERTS), 0.01),
        'moe_w_gate': nrm(ks[24], (L, N_EXPERTS, D, EXPERT_FF), D ** -0.5),
        'moe_w_up': nrm(ks[25], (L, N_EXPERTS, D, EXPERT_FF), D ** -0.5),
        'moe_w_down': nrm(ks[26], (L, N_EXPERTS, EXPERT_FF, D), EXPERT_FF ** -0.5 * DEEPNORM_BETA),
    }


def reference(x, c, w_in, mla_q_norm, mla_kv_norm, mla_w_uq, mla_w_ukv, gdn_conv, gdn_a_log,
              gdn_dt_bias, gdn_out_norm, pool_w, pool_scale, w_out, w_mod, b_mod, ln1_g, ln1_b,
              ln2_g, ln2_b, router_w_group, router_b_group, router_w_expert, router_b_expert,
              moe_w_gate, moe_w_up, moe_w_down):
    cos, sin = rope_tables(x.shape[1])
    c_act = jax.nn.silu(c)
    for l in range(DEPTH):
        mod = c_act @ w_mod[l] + b_mod[l]
        sh1, sc1, gt1, sh2, sc2, gt2 = [m[:, None, :] for m in jnp.split(mod, 6, axis=-1)]
        h = x * (1.0 + sc1) + sh1
        y = hybrid_mixer(h, w_in[l], mla_q_norm[l], mla_kv_norm[l], mla_w_uq[l], mla_w_ukv[l],
                         gdn_conv[l], gdn_a_log[l], gdn_dt_bias[l], gdn_out_norm[l],
                         pool_w[l], pool_scale[l], w_out[l], cos, sin)
        x = layer_norm(DEEPNORM_ALPHA * x + (1.0 + gt1) * y, ln1_g[l], ln1_b[l])
        h = x * (1.0 + sc2) + sh2
        y = hierarchical_moe(h, router_w_group[l], router_b_group[l], router_w_expert[l],
                             router_b_expert[l], moe_w_gate[l], moe_w_up[l], moe_w_down[l])
        x = layer_norm(DEEPNORM_ALPHA * x + (1.0 + gt2) * y, ln2_g[l], ln2_b[l])
    return x
```

```python
import functools
import math

import jax
import jax.numpy as jnp
from jax import lax
from jax.experimental import pallas as pl
from jax.experimental.pallas import tpu as pltpu

F32 = jnp.float32
BF16 = jnp.bfloat16

D_MODEL = 1024
CHUNK = 64
MLA_HEADS = 8
MLA_NOPE = 64
MLA_ROPE = 32
MLA_V = 64
MLA_Q_RANK = 256
MLA_KV_RANK = 128
MLA_WIDTH = MLA_HEADS * MLA_V
ROPE_THETA = 10000.0
GDN_HEADS = 4
GDN_DIM = 64
GDN_WIDTH = GDN_HEADS * GDN_DIM
GDN_CONV = 4
POOL_WIDTH = 256
POOL_WINDOWS = (2, 4, 8, 16)
POOL_GROUP_DIM = 64
N_GROUPS = 4
EXPERTS_PER_GROUP = 8
N_EXPERTS = N_GROUPS * EXPERTS_PER_GROUP
TOP_K = 2
EXPERT_FF = 256
LN_EPS = 1e-5
RMS_EPS = 1e-6

LANE = 128
HEAD_PAD = 128
MOE_ROWS = 256
VMEM_LIMIT = 48 * 1024 * 1024

_C_GQ, _C_GK, _C_GV, _C_GZ = 0, 256, 512, 768
_C_A, _C_B = 1024, 1280
_C_CQ, _C_POOL = 1536, 1792
_C_CKV, _C_KRA, _C_KRB = 2048, 2176, 2304
IN_COLS = 2432


def _cparams(sem):
    return pltpu.CompilerParams(dimension_semantics=sem, vmem_limit_bytes=VMEM_LIMIT)


def _sigmoid(x):
    return 1.0 / (1.0 + jnp.exp(-x))


def _split2(x):
    hi = x.astype(BF16)
    lo = (x - hi.astype(F32)).astype(BF16)
    return hi, lo


def _split3(x):
    hi = x.astype(BF16)
    r = x - hi.astype(F32)
    mid = r.astype(BF16)
    lo = (r - mid.astype(F32)).astype(BF16)
    return hi, mid, lo


def _dot(a, b):
    return jnp.dot(a, b, preferred_element_type=F32)


def _dot_nt(a, b):
    return lax.dot_general(a, b, (((1,), (1,)), ((), ())), preferred_element_type=F32)


def _dot_tn(a, b):
    return lax.dot_general(a, b, (((0,), (0,)), ((), ())), preferred_element_type=F32)


def _mod_body(c_ref, w_ref, b_ref, o_ref):
    c = c_ref[...]
    ca = c * _sigmoid(c)
    o_ref[0] = _dot(ca.astype(BF16), w_ref[0].astype(BF16)) + b_ref[0]


def _modulation(c, w_mod, b_mod):
    L, D, N = w_mod.shape
    B = c.shape[0]
    tn = 1024
    return pl.pallas_call(
        _mod_body,
        grid=(L, N // tn),
        in_specs=[pl.BlockSpec((B, D), lambda l, j: (0, 0)),
                  pl.BlockSpec((1, D, tn), lambda l, j: (l, 0, j)),
                  pl.BlockSpec((1, 1, tn), lambda l, j: (l, 0, j))],
        out_specs=pl.BlockSpec((1, B, tn), lambda l, j: (l, 0, j)),
        out_shape=jax.ShapeDtypeStruct((L, B, N), F32),
        compiler_params=_cparams(("arbitrary", "arbitrary")),
        name="modulation",
    )(c, w_mod, b_mod.reshape(L, 1, N))


def _inproj_body(x_ref, sh_ref, sc_ref, w_ref, o_ref):
    h = x_ref[...] * (1.0 + sc_ref[0]) + sh_ref[0]
    o_ref[...] = _dot(h.astype(BF16), w_ref[0])


def _inproj(x2, mod_l, w_in2, l, S, tm):
    T, D = x2.shape
    nS = S // tm
    return pl.pallas_call(
        _inproj_body,
        grid=(T // tm,),
        in_specs=[pl.BlockSpec((tm, D), lambda i: (i, 0)),
                  pl.BlockSpec((1, 1, D), lambda i: (i // nS, 0, 0)),
                  pl.BlockSpec((1, 1, D), lambda i: (i // nS, 0, 1)),
                  pl.BlockSpec((1, D, IN_COLS), lambda i: (l, 0, 0))],
        out_specs=pl.BlockSpec((tm, IN_COLS), lambda i: (i, 0)),
        out_shape=jax.ShapeDtypeStruct((T, IN_COLS), F32),
        compiler_params=_cparams(("arbitrary",)),
        name="inproj",
    )(x2, mod_l, mod_l, w_in2)


def _mla_proj_body(cq_ref, ckv_ref, kra_ref, krb_ref, cosq_ref, sinq_ref, cosk_ref, sink_ref,
                   gq_ref, gkv_ref, wq_ref, wkv_ref, q_out, k_out, v_out):
    cq = cq_ref[...]
    qn = cq * lax.rsqrt(jnp.mean(cq * cq, axis=-1, keepdims=True) + RMS_EPS) * gq_ref[0]
    q2 = _dot(qn.astype(BF16), wq_ref[0])
    ckv = ckv_ref[...]
    kvn = ckv * lax.rsqrt(jnp.mean(ckv * ckv, axis=-1, keepdims=True) + RMS_EPS) * gkv_ref[0]
    kv2 = _dot(kvn.astype(BF16), wkv_ref[0])
    cq_t, sq_t = cosq_ref[...], sinq_ref[...]
    krope = kra_ref[...] * cosk_ref[...] + krb_ref[...] * sink_ref[...]
    hw = MLA_HEADS * HEAD_PAD
    for h in range(MLA_HEADS):
        a, b = h * HEAD_PAD, (h + 1) * HEAD_PAD
        q_out[:, a:b] = (q2[:, a:b] * cq_t + q2[:, hw + a:hw + b] * sq_t).astype(BF16)
        k_out[:, a:b] = (kv2[:, a:b] + krope).astype(BF16)
    v_out[...] = kv2[:, hw:].astype(BF16)


def _mla_proj(proj, tabs, gq, gkv, wq2, wkv2, l, S, ts):
    T = proj.shape[0]
    nS = S // ts
    hw = MLA_HEADS * HEAD_PAD
    tab_spec = pl.BlockSpec((ts, LANE), lambda i: (i % nS, 0))
    return pl.pallas_call(
        _mla_proj_body,
        grid=(T // ts,),
        in_specs=[pl.BlockSpec((ts, 256), lambda i: (i, _C_CQ // 256)),
                  pl.BlockSpec((ts, 128), lambda i: (i, _C_CKV // 128)),
                  pl.BlockSpec((ts, 128), lambda i: (i, _C_KRA // 128)),
                  pl.BlockSpec((ts, 128), lambda i: (i, _C_KRB // 128)),
                  tab_spec, tab_spec, tab_spec, tab_spec,
                  pl.BlockSpec((1, 1, MLA_Q_RANK), lambda i: (l, 0, 0)),
                  pl.BlockSpec((1, 1, MLA_KV_RANK), lambda i: (l, 0, 0)),
                  pl.BlockSpec((1, MLA_Q_RANK, 2 * hw), lambda i: (l, 0, 0)),
                  pl.BlockSpec((1, MLA_KV_RANK, 2 * hw), lambda i: (l, 0, 0))],
        out_specs=[pl.BlockSpec((ts, hw), lambda i: (i, 0))] * 3,
        out_shape=[jax.ShapeDtypeStruct((T, hw), BF16)] * 3,
        compiler_params=_cparams(("arbitrary",)),
        name="mla_proj",
    )(proj, proj, proj, proj, *tabs, gq, gkv, wq2, wkv2)


def _attn_body(q_ref, k_ref, v_ref, o_ref, *, tq):
    qi = pl.program_id(2)
    row_c = lax.broadcasted_iota(jnp.int32, (tq, tq), 0) // CHUNK
    col_c = lax.broadcasted_iota(jnp.int32, (tq, tq), 1) // CHUNK
    allowed = col_c <= row_c
    out = None
    for hh in range(2):
        a, b = hh * HEAD_PAD, (hh + 1) * HEAD_PAD
        q = q_ref[:, a:b]

        def update(carry, k, v, mask):
            m, l, acc = carry
            s = _dot_nt(q, k)
            if mask is not None:
                s = jnp.where(mask, s, -jnp.inf)
            m_new = jnp.maximum(m, jnp.max(s, axis=-1, keepdims=True))
            alpha = jnp.exp(m - m_new)
            p = jnp.exp(s - m_new)
            l = alpha * l + jnp.sum(p, axis=-1, keepdims=True)
            acc = alpha * acc + _dot(p.astype(BF16), v)
            return m_new, l, acc

        def step(j, carry):
            r0 = pl.multiple_of(j * tq, tq)
            return update(carry, k_ref[pl.ds(r0, tq), a:b], v_ref[pl.ds(r0, tq), a:b], None)

        init = (jnp.full((tq, 1), -jnp.inf, F32), jnp.zeros((tq, 1), F32), jnp.zeros((tq, HEAD_PAD), F32))
        carry = lax.fori_loop(0, qi, step, init)
        r0 = pl.multiple_of(qi * tq, tq)
        m, l, acc = update(carry, k_ref[pl.ds(r0, tq), a:b], v_ref[pl.ds(r0, tq), a:b], allowed)
        o_h = acc * (1.0 / l)
        out = o_h if out is None else out + o_h
    o_ref[...] = out.astype(BF16)


def _attention(q, k, v, B, S, tq):
    T = q.shape[0]
    nq = S // tq
    pairs = MLA_HEADS // 2
    return pl.pallas_call(
        functools.partial(_attn_body, tq=tq),
        grid=(B, pairs, nq),
        in_specs=[pl.BlockSpec((tq, 2 * HEAD_PAD), lambda b, p, i: (b * nq + i, p)),
                  pl.BlockSpec((S, 2 * HEAD_PAD), lambda b, p, i: (b, p)),
                  pl.BlockSpec((S, 2 * HEAD_PAD), lambda b, p, i: (b, p))],
        out_specs=pl.BlockSpec((tq, 2 * MLA_V), lambda b, p, i: (b * nq + i, p)),
        out_shape=jax.ShapeDtypeStruct((T, MLA_WIDTH), BF16),
        compiler_params=_cparams(("arbitrary", "arbitrary", "arbitrary")),
        name="mla_attention",
    )(q, k, v)


def _pool_body(p_ref, w_ref, sc_ref, o_ref, prev_ref, *, ts):
    j = pl.program_id(1)
    hist = 16

    @pl.when(j == 0)
    def _():
        prev_ref[...] = jnp.zeros_like(prev_ref)

    cur = p_ref[...]
    x = jnp.concatenate([prev_ref[...], cur], axis=0)
    s1 = x + pltpu.roll(x, 1, axis=0)
    s2 = s1 + pltpu.roll(s1, 2, axis=0)
    s4 = s2 + pltpu.roll(s2, 4, axis=0)
    s8 = s4 + pltpu.roll(s4, 8, axis=0)
    pos = (j * ts + lax.broadcasted_iota(jnp.int32, (ts, POOL_WIDTH), 0) + 1).astype(F32)
    lane = lax.broadcasted_iota(jnp.int32, (ts, POOL_WIDTH), 1)
    sums = (s1, s2, s4, s8)
    mean = None
    for gi, win in enumerate(POOL_WINDOWS):
        m_g = sums[gi][hist:] / jnp.minimum(pos, float(win))
        mean = m_g if mean is None else jnp.where(lane >= gi * POOL_GROUP_DIM, m_g, mean)
    delta = mean - cur
    o_ref[...] = (_dot(delta.astype(BF16), w_ref[0]) * sc_ref[0]).astype(BF16)
    prev_ref[...] = cur[ts - hist:]


def _pool(proj, w_bd, scale, l, B, S, ts):
    T = proj.shape[0]
    nS = S // ts
    return pl.pallas_call(
        functools.partial(_pool_body, ts=ts),
        grid=(B, nS),
        in_specs=[pl.BlockSpec((ts, POOL_WIDTH), lambda b, j: (b * nS + j, _C_POOL // 256)),
                  pl.BlockSpec((1, POOL_WIDTH, POOL_WIDTH), lambda b, j: (l, 0, 0)),
                  pl.BlockSpec((1, 1, POOL_WIDTH), lambda b, j: (l, 0, 0))],
        out_specs=pl.BlockSpec((ts, POOL_WIDTH), lambda b, j: (b * nS + j, 0)),
        out_shape=jax.ShapeDtypeStruct((T, POOL_WIDTH), BF16),
        scratch_shapes=[pltpu.VMEM((16, POOL_WIDTH), F32)],
        compiler_params=_cparams(("arbitrary", "arbitrary")),
        name="pool_mixer",
    )(proj, w_bd, scale)


def _head_sum(x, seg_ref):
    hi, lo = _split2(x)
    return _dot(hi, seg_ref[...]) + _dot(lo, seg_ref[...])


def _gdn_pre_body(q_ref, k_ref, v_ref, a_ref, b_ref, conv_ref, alog_ref, dt_ref, seg_ref,
                  qo_ref, ko_ref, vo_ref, go_ref, bo_ref, prev_ref, *, ts):
    j = pl.program_id(1)
    hist = 8

    @pl.when(j == 0)
    def _():
        prev_ref[...] = jnp.zeros_like(prev_ref)

    cw = conv_ref[0]
    outs = []
    for idx, ref in enumerate((q_ref, k_ref, v_ref)):
        cur = ref[...]
        a, b = idx * GDN_WIDTH, (idx + 1) * GDN_WIDTH
        x = jnp.concatenate([prev_ref[:, a:b], cur], axis=0)
        w = cw[:, a:b]
        y = (w[3:4] * x + w[2:3] * pltpu.roll(x, 1, axis=0) + w[1:2] * pltpu.roll(x, 2, axis=0)
             + w[0:1] * pltpu.roll(x, 3, axis=0))[hist:]
        outs.append(y * _sigmoid(y))
        prev_ref[:, a:b] = cur[ts - hist:]
    qc, kc, vc = outs
    qo_ref[...] = qc * lax.rsqrt(_head_sum(qc * qc, seg_ref) + RMS_EPS) * (GDN_DIM ** -0.5)
    ko_ref[...] = kc * lax.rsqrt(_head_sum(kc * kc, seg_ref) + RMS_EPS)
    vo_ref[...] = vc
    z = a_ref[...] + dt_ref[0]
    softplus = jnp.maximum(z, 0.0) + jnp.log(1.0 + jnp.exp(-jnp.abs(z)))
    go_ref[...] = -jnp.exp(alog_ref[0]) * softplus
    bo_ref[...] = _sigmoid(b_ref[...])


def _gdn_pre(proj, conv_w, alog_e, dt_e, seg, l, B, S, ts):
    T = proj.shape[0]
    nS = S // ts
    W = GDN_WIDTH

    def col(c):
        return pl.BlockSpec((ts, W), lambda b, j: (b * nS + j, c // W))

    return pl.pallas_call(
        functools.partial(_gdn_pre_body, ts=ts),
        grid=(B, nS),
        in_specs=[col(_C_GQ), col(_C_GK), col(_C_GV), col(_C_A), col(_C_B),
                  pl.BlockSpec((1, GDN_CONV, 3 * W), lambda b, j: (l, 0, 0)),
                  pl.BlockSpec((1, 1, W), lambda b, j: (l, 0, 0)),
                  pl.BlockSpec((1, 1, W), lambda b, j: (l, 0, 0)),
                  pl.BlockSpec((W, W), lambda b, j: (0, 0))],
        out_specs=[pl.BlockSpec((ts, W), lambda b, j: (b * nS + j, 0))] * 5,
        out_shape=[jax.ShapeDtypeStruct((T, W), F32)] * 5,
        scratch_shapes=[pltpu.VMEM((8, 3 * W), F32)],
        compiler_params=_cparams(("arbitrary", "arbitrary")),
        name="gdn_pre",
    )(proj, proj, proj, proj, proj, conv_w, alog_e, dt_e, seg)


def _gdn_body(q_ref, k_ref, v_ref, g_ref, b_ref, z_ref, og_ref, seg_ref, tri_ref,
              o_ref, state_ref, *, ts):
    j = pl.program_id(1)
    C, W, H = CHUNK, GDN_WIDTH, GDN_HEADS

    @pl.when(j == 0)
    def _():
        state_ref[...] = jnp.zeros_like(state_ref)

    lane = lax.broadcasted_iota(jnp.int32, (C, W), 1)
    row = lax.broadcasted_iota(jnp.int32, (C, W), 0)
    col_tok = lane % C
    incl = row >= col_tok
    strict = row > col_tok
    eye = (row == col_tok).astype(F32)
    head_masks = [(lane // C == h).astype(F32) for h in range(H)]
    bd_mask = (lax.broadcasted_iota(jnp.int32, (W, W), 0) // C
               == lax.broadcasted_iota(jnp.int32, (W, W), 1) // C)

    def expand(m):
        return jnp.concatenate([m * hm for hm in head_masks], axis=0)

    def chunk(c, _):
        r0 = pl.multiple_of(c * C, C)
        q = q_ref[pl.ds(r0, C), :]
        k = k_ref[pl.ds(r0, C), :]
        v = v_ref[pl.ds(r0, C), :]
        g = g_ref[pl.ds(r0, C), :]
        beta = b_ref[pl.ds(r0, C), :]
        g_hi, g_mid, g_lo = _split3(g)
        tri = tri_ref[...]
        gc = _dot(tri, g_hi) + _dot(tri, g_mid) + _dot(tri, g_lo)
        g_row = jnp.sum(gc * eye, axis=0, keepdims=True)
        g_last = gc[C - 1:C, :]
        diff = gc - g_row
        decay = jnp.where(incl, jnp.exp(jnp.where(incl, diff, 0.0)), 0.0)
        e_gc = jnp.exp(gc)
        kb = k * beta
        k_exp = expand(k).astype(BF16)
        aq = _dot_nt(jnp.concatenate([kb, q], axis=0).astype(BF16), k_exp)
        a_cat = jnp.where(strict, aq[:C] * decay, 0.0)
        qk_cat = jnp.where(incl, aq[C:] * decay, 0.0)
        x_cat = eye - a_cat
        a_hi, a_lo = _split2(a_cat)
        ab_hi, ab_lo = _split2(expand(a_cat))
        p_cat = _dot(a_hi, ab_hi) + _dot(a_lo, ab_hi) + _dot(a_hi, ab_lo)
        n_fac = int(math.log2(C)) - 1
        for r in range(n_fac):
            pb_hi, pb_lo = _split2(expand(p_cat))
            lhs = x_cat if r == n_fac - 1 else jnp.concatenate([x_cat, p_cat], axis=0)
            l_hi, l_lo = _split2(lhs)
            xp = _dot(l_hi, pb_hi) + _dot(l_lo, pb_hi) + _dot(l_hi, pb_lo)
            x_cat = x_cat + xp[:C]
            if r != n_fac - 1:
                p_cat = xp[C:]
        t_cat = x_cat.astype(BF16)
        u = _dot(t_cat, expand(v * beta).astype(BF16))
        w = _dot(t_cat, expand(kb * e_gc).astype(BF16))
        state = state_ref[...]
        wq = _dot(jnp.concatenate([w, q * e_gc], axis=0).astype(BF16), state.astype(BF16))
        v_new = u - wq[:C]
        o = wq[C:] + _dot(qk_cat.astype(BF16), expand(v_new).astype(BF16))
        k_dec = k * jnp.exp(g_last - gc)
        upd = _dot_tn(k_dec.astype(BF16), v_new.astype(BF16))
        state_ref[...] = state * jnp.exp(g_last) + jnp.where(bd_mask, upd, 0.0)
        ms = _head_sum(o * o, seg_ref) * (1.0 / GDN_DIM)
        z = z_ref[pl.ds(r0, C), :]
        y = o * lax.rsqrt(ms + RMS_EPS) * og_ref[0] * (z * _sigmoid(z))
        o_ref[pl.ds(r0, C), :] = y.astype(BF16)
        return 0

    lax.fori_loop(0, ts // C, chunk, 0)


def _gdn(qh, kh, vh, g, beta, proj, og, seg, tri, l, B, S, ts):
    T = qh.shape[0]
    nS = S // ts
    W = GDN_WIDTH
    row = pl.BlockSpec((ts, W), lambda b, j: (b * nS + j, 0))
    return pl.pallas_call(
        functools.partial(_gdn_body, ts=ts),
        grid=(B, nS),
        in_specs=[row, row, row, row, row,
                  pl.BlockSpec((ts, W), lambda b, j: (b * nS + j, _C_GZ // W)),
                  pl.BlockSpec((1, 1, W), lambda b, j: (l, 0, 0)),
                  pl.BlockSpec((W, W), lambda b, j: (0, 0)),
                  pl.BlockSpec((CHUNK, CHUNK), lambda b, j: (0, 0))],
        out_specs=pl.BlockSpec((ts, W), lambda b, j: (b * nS + j, 0)),
        out_shape=jax.ShapeDtypeStruct((T, W), BF16),
        scratch_shapes=[pltpu.VMEM((W, W), F32)],
        compiler_params=_cparams(("arbitrary", "arbitrary")),
        name="gdn_delta_rule",
    )(qh, kh, vh, g, beta, proj, og, seg, tri)


def _layer_norm(r, g, b):
    mu = jnp.mean(r, axis=-1, keepdims=True)
    d = r - mu
    var = jnp.mean(d * d, axis=-1, keepdims=True)
    return d * lax.rsqrt(var + LN_EPS) * g + b


def _outproj_body(ym_ref, yg_ref, yp_ref, w_ref, x_ref, gt_ref, lg_ref, lb_ref, o_ref, *, alpha):
    w = w_ref[0]
    y = (_dot(ym_ref[...], w[:MLA_WIDTH]) + _dot(yg_ref[...], w[MLA_WIDTH:MLA_WIDTH + GDN_WIDTH])
         + _dot(yp_ref[...], w[MLA_WIDTH + GDN_WIDTH:]))
    r = alpha * x_ref[...] + (1.0 + gt_ref[0]) * y
    o_ref[...] = _layer_norm(r, lg_ref[0], lb_ref[0])


def _outproj(y_mla, y_gdn, y_pool, w_out, x2, mod_l, ln_g, ln_b, l, S, tm, alpha):
    T, D = x2.shape
    nS = S // tm
    vec = pl.BlockSpec((1, 1, D), lambda i: (l, 0, 0))
    return pl.pallas_call(
        functools.partial(_outproj_body, alpha=alpha),
        grid=(T // tm,),
        in_specs=[pl.BlockSpec((tm, MLA_WIDTH), lambda i: (i, 0)),
                  pl.BlockSpec((tm, GDN_WIDTH), lambda i: (i, 0)),
                  pl.BlockSpec((tm, POOL_WIDTH), lambda i: (i, 0)),
                  pl.BlockSpec((1, D, D), lambda i: (l, 0, 0)),
                  pl.BlockSpec((tm, D), lambda i: (i, 0)),
                  pl.BlockSpec((1, 1, D), lambda i: (i // nS, 0, 2)),
                  vec, vec],
        out_specs=pl.BlockSpec((tm, D), lambda i: (i, 0)),
        out_shape=jax.ShapeDtypeStruct((T, D), F32),
        compiler_params=_cparams(("arbitrary",)),
        name="outproj_ln",
    )(y_mla, y_gdn, y_pool, w_out, x2, mod_l, ln_g, ln_b)


def _lane_first(cond, lane_f):
    return jnp.min(jnp.where(cond, lane_f, float(LANE)), axis=-1, keepdims=True)


def _router_body(x_ref, sh_ref, sc_ref, whi_ref, wlo_ref, br_ref, tri_ref,
                 h_out, meta_out, cnt_out, carry_ref):
    i = pl.program_id(0)

    @pl.when(i == 0)
    def _():
        carry_ref[...] = jnp.zeros_like(carry_ref)

    h = x_ref[...] * (1.0 + sc_ref[0]) + sh_ref[0]
    h_out[...] = h
    h_hi, h_lo = _split2(h)
    logits = _dot(h_hi, whi_ref[0]) + _dot(h_lo, whi_ref[0]) + _dot(h_hi, wlo_ref[0]) + br_ref[0]
    tm = logits.shape[0]
    lane = lax.broadcasted_iota(jnp.int32, (tm, LANE), 1)
    lane_f = lane.astype(F32)
    neg = -jnp.inf
    gl = jnp.where(lane < N_GROUPS, logits, neg)
    gmax = jnp.max(gl, axis=-1, keepdims=True)
    gsel = _lane_first(gl == gmax, lane_f)
    g_p = 1.0 / jnp.sum(jnp.exp(gl - gmax), axis=-1, keepdims=True)
    lo = N_GROUPS + EXPERTS_PER_GROUP * gsel
    el = jnp.where((lane_f >= lo) & (lane_f < lo + EXPERTS_PER_GROUP), logits, neg)
    m1 = jnp.max(el, axis=-1, keepdims=True)
    i1 = _lane_first(el == m1, lane_f)
    el2 = jnp.where(lane_f == i1, neg, el)
    m2 = jnp.max(el2, axis=-1, keepdims=True)
    i2 = _lane_first(el2 == m2, lane_f)
    t = jnp.exp(m2 - m1)
    w1 = g_p / (1.0 + t)
    w2 = g_p * t / (1.0 + t)
    e1 = i1 - N_GROUPS
    e2 = i2 - N_GROUPS
    hit1 = lane_f == e1
    hit2 = lane_f == e2
    onehot = (hit1 | hit2).astype(BF16)
    before = _dot(tri_ref[...], onehot) + carry_ref[0:1, :]
    r1 = jnp.sum(jnp.where(hit1, before, 0.0), axis=-1, keepdims=True)
    r2 = jnp.sum(jnp.where(hit2, before, 0.0), axis=-1, keepdims=True)
    total = carry_ref[0:1, :] + jnp.sum(onehot.astype(F32), axis=0, keepdims=True)
    carry_ref[...] = jnp.broadcast_to(total, carry_ref.shape)
    cnt_out[...] = jnp.broadcast_to(total, cnt_out.shape)
    meta = jnp.zeros((tm, LANE), F32)
    for idx, val in enumerate((e1, e2, r1, r2, w1, w2)):
        meta = jnp.where(lane == idx, val, meta)
    meta_out[...] = meta


def _router(x2, mod_l, w_hi, w_lo, b_r, tri, l, S, tm):
    T, D = x2.shape
    nS = S // tm
    return pl.pallas_call(
        _router_body,
        grid=(T // tm,),
        in_specs=[pl.BlockSpec((tm, D), lambda i: (i, 0)),
                  pl.BlockSpec((1, 1, D), lambda i: (i // nS, 0, 3)),
                  pl.BlockSpec((1, 1, D), lambda i: (i // nS, 0, 4)),
                  pl.BlockSpec((1, D, LANE), lambda i: (l, 0, 0)),
                  pl.BlockSpec((1, D, LANE), lambda i: (l, 0, 0)),
                  pl.BlockSpec((1, 1, LANE), lambda i: (l, 0, 0)),
                  pl.BlockSpec((tm, tm), lambda i: (0, 0))],
        out_specs=[pl.BlockSpec((tm, D), lambda i: (i, 0)),
                   pl.BlockSpec((tm, LANE), lambda i: (i, 0)),
                   pl.BlockSpec((8, LANE), lambda i: (0, 0))],
        out_shape=[jax.ShapeDtypeStruct((T, D), F32),
                   jax.ShapeDtypeStruct((T, LANE), F32),
                   jax.ShapeDtypeStruct((8, LANE), F32)],
        scratch_shapes=[pltpu.VMEM((8, LANE), F32)],
        compiler_params=_cparams(("arbitrary",)),
        name="router",
    )(x2, mod_l, mod_l, w_hi, w_lo, b_r, tri)


def _row_copy(src, s, dst, d, sem):
    return pltpu.make_async_copy(src.at[pl.ds(s, 1)], dst.at[pl.ds(d, 1)], sem)


def _dispatch_body(dest_ref, fill_ref, h_ref, xs_ref, zero_ref, sem, *, tm, n_fill):
    i = pl.program_id(0)

    @pl.when(i == 0)
    def _():
        zero_ref[...] = jnp.zeros_like(zero_ref)

        def fill(r, _):
            _row_copy(zero_ref, 0, xs_ref, fill_ref[r], sem).start()
            return 0

        lax.fori_loop(0, n_fill, fill, 0)

        def drain(r, _):
            _row_copy(zero_ref, 0, xs_ref, 0, sem).wait()
            return 0

        lax.fori_loop(0, n_fill, drain, 0)

    def issue(r, _):
        _row_copy(h_ref, r, xs_ref, dest_ref[2 * r], sem).start()
        _row_copy(h_ref, r, xs_ref, dest_ref[2 * r + 1], sem).start()
        return 0

    lax.fori_loop(0, tm, issue, 0)

    def drain2(r, _):
        _row_copy(h_ref, 0, xs_ref, 0, sem).wait()
        _row_copy(h_ref, 0, xs_ref, 0, sem).wait()
        return 0

    lax.fori_loop(0, tm, drain2, 0)


def _dispatch(h2, dest_flat, fill_rows, P, tm):
    T, D = h2.shape
    n_fill = fill_rows.shape[0]
    return pl.pallas_call(
        functools.partial(_dispatch_body, tm=tm, n_fill=n_fill),
        grid=(T // tm,),
        in_specs=[pl.BlockSpec((2 * tm,), lambda i: (i,), memory_space=pltpu.SMEM),
                  pl.BlockSpec((n_fill,), lambda i: (0,), memory_space=pltpu.SMEM),
                  pl.BlockSpec((tm, D), lambda i: (i, 0))],
        out_specs=pl.BlockSpec(memory_space=pl.ANY),
        out_shape=jax.ShapeDtypeStruct((P, D), F32),
        scratch_shapes=[pltpu.VMEM((8, D), F32), pltpu.SemaphoreType.DMA],
        compiler_params=_cparams(("arbitrary",)),
        name="moe_dispatch",
    )(dest_flat, fill_rows, h2)


def _ffn_body(be_ref, fl_ref, xs_ref, wg_ref, wu_ref, wd_ref, o_ref, wg_b, wu_b, wd_b):
    i = pl.program_id(0)
    flags = fl_ref[i]

    @pl.when((flags & 2) != 0)
    def _():
        wg_b[...] = wg_ref[0, 0].astype(BF16)
        wu_b[...] = wu_ref[0, 0].astype(BF16)
        wd_b[...] = wd_ref[0, 0].astype(BF16)

    @pl.when((flags & 1) != 0)
    def _():
        x = xs_ref[...].astype(BF16)
        g = _dot(x, wg_b[...])
        u = _dot(x, wu_b[...])
        a = g * _sigmoid(g) * u
        o_ref[...] = _dot(a.astype(BF16), wd_b[...])

    @pl.when((flags & 1) == 0)
    def _():
        o_ref[...] = jnp.zeros_like(o_ref)


def _ffn(xs, block_expert, block_flags, w_gate, w_up, w_down, l):
    P, D = xs.shape
    nb = P // MOE_ROWS
    FF = w_gate.shape[-1]
    grid_spec = pltpu.PrefetchScalarGridSpec(
        num_scalar_prefetch=2,
        grid=(nb,),
        in_specs=[pl.BlockSpec((MOE_ROWS, D), lambda i, be, fl: (i, 0)),
                  pl.BlockSpec((1, 1, D, FF), lambda i, be, fl: (l, be[i], 0, 0)),
                  pl.BlockSpec((1, 1, D, FF), lambda i, be, fl: (l, be[i], 0, 0)),
                  pl.BlockSpec((1, 1, FF, D), lambda i, be, fl: (l, be[i], 0, 0))],
        out_specs=pl.BlockSpec((MOE_ROWS, D), lambda i, be, fl: (i, 0)),
        scratch_shapes=[pltpu.VMEM((D, FF), BF16), pltpu.VMEM((D, FF), BF16), pltpu.VMEM((FF, D), BF16)],
    )
    return pl.pallas_call(
        _ffn_body,
        grid_spec=grid_spec,
        out_shape=jax.ShapeDtypeStruct((P, D), F32),
        compiler_params=_cparams(("arbitrary",)),
        name="moe_ffn",
    )(block_expert, block_flags, xs, w_gate, w_up, w_down)


def _combine_body(dest_ref, meta_ref, x_ref, gt_ref, lg_ref, lb_ref, ys_ref, o_ref,
                  y0_ref, y1_ref, sem, *, tm, alpha):
    def issue(r, _):
        _row_copy(ys_ref, dest_ref[2 * r], y0_ref, r, sem).start()
        _row_copy(ys_ref, dest_ref[2 * r + 1], y1_ref, r, sem).start()
        return 0

    lax.fori_loop(0, tm, issue, 0)

    def drain(r, _):
        _row_copy(ys_ref, 0, y0_ref, 0, sem).wait()
        _row_copy(ys_ref, 0, y1_ref, 0, sem).wait()
        return 0

    lax.fori_loop(0, tm, drain, 0)
    meta = meta_ref[...]
    y = meta[:, 4:5] * y0_ref[...] + meta[:, 5:6] * y1_ref[...]
    r = alpha * x_ref[...] + (1.0 + gt_ref[0]) * y
    o_ref[...] = _layer_norm(r, lg_ref[0], lb_ref[0])


def _combine(ys, dest_flat, meta, x2, mod_l, ln_g, ln_b, l, S, tm, alpha):
    T, D = x2.shape
    nS = S // tm
    vec = pl.BlockSpec((1, 1, D), lambda i: (l, 0, 0))
    return pl.pallas_call(
        functools.partial(_combine_body, tm=tm, alpha=alpha),
        grid=(T // tm,),
        in_specs=[pl.BlockSpec((2 * tm,), lambda i: (i,), memory_space=pltpu.SMEM),
                  pl.BlockSpec((tm, LANE), lambda i: (i, 0)),
                  pl.BlockSpec((tm, D), lambda i: (i, 0)),
                  pl.BlockSpec((1, 1, D), lambda i: (i // nS, 0, 5)),
                  vec, vec,
                  pl.BlockSpec(memory_space=pl.ANY)],
        out_specs=pl.BlockSpec((tm, D), lambda i: (i, 0)),
        out_shape=jax.ShapeDtypeStruct((T, D), F32),
        scratch_shapes=[pltpu.VMEM((tm, D), F32), pltpu.VMEM((tm, D), F32), pltpu.SemaphoreType.DMA],
        compiler_params=_cparams(("arbitrary",)),
        name="moe_combine_ln",
    )(dest_flat, meta, x2, mod_l, ln_g, ln_b, ys)


def _prep_w_in(w_in):
    L, D, _ = w_in.shape
    o = 0
    cq = w_in[..., o:o + MLA_Q_RANK]; o += MLA_Q_RANK
    ckv = w_in[..., o:o + MLA_KV_RANK]; o += MLA_KV_RANK
    kr = w_in[..., o:o + MLA_ROPE]; o += MLA_ROPE
    gq = w_in[..., o:o + GDN_WIDTH]; o += GDN_WIDTH
    gk = w_in[..., o:o + GDN_WIDTH]; o += GDN_WIDTH
    gv = w_in[..., o:o + GDN_WIDTH]; o += GDN_WIDTH
    gz = w_in[..., o:o + GDN_WIDTH]; o += GDN_WIDTH
    ga = w_in[..., o:o + GDN_HEADS]; o += GDN_HEADS
    gb = w_in[..., o:o + GDN_HEADS]; o += GDN_HEADS
    pw = w_in[..., o:o + POOL_WIDTH]
    half = MLA_ROPE // 2
    z = lambda n: jnp.zeros((L, D, n), w_in.dtype)
    kra = jnp.concatenate([z(MLA_NOPE), kr, z(HEAD_PAD - MLA_NOPE - MLA_ROPE)], axis=-1)
    krb = jnp.concatenate([z(MLA_NOPE), kr[..., half:], kr[..., :half], z(HEAD_PAD - MLA_NOPE - MLA_ROPE)], axis=-1)
    a_e = jnp.repeat(ga, GDN_DIM, axis=-1)
    b_e = jnp.repeat(gb, GDN_DIM, axis=-1)
    out = jnp.concatenate([gq, gk, gv, gz, a_e, b_e, cq, pw, ckv, kra, krb], axis=-1)
    assert out.shape[-1] == IN_COLS
    return out.astype(BF16)


def _prep_mla(w_uq, w_ukv):
    L = w_uq.shape[0]
    H, half = MLA_HEADS, MLA_ROPE // 2
    pad = HEAD_PAD - MLA_NOPE - MLA_ROPE
    q = w_uq.reshape(L, MLA_Q_RANK, H, MLA_NOPE + MLA_ROPE)
    nope, r1, r2 = q[..., :MLA_NOPE], q[..., MLA_NOPE:MLA_NOPE + half], q[..., MLA_NOPE + half:]
    zq = lambda n: jnp.zeros((L, MLA_Q_RANK, H, n), w_uq.dtype)
    plain = jnp.concatenate([nope, r1, r2, zq(pad)], axis=-1).reshape(L, MLA_Q_RANK, H * HEAD_PAD)
    partner = jnp.concatenate([zq(MLA_NOPE), r2, r1, zq(pad)], axis=-1).reshape(L, MLA_Q_RANK, H * HEAD_PAD)
    wq2 = jnp.concatenate([plain, partner], axis=-1).astype(BF16)
    kv = w_ukv.reshape(L, MLA_KV_RANK, H, MLA_NOPE + MLA_V)
    k_nope, v = kv[..., :MLA_NOPE], kv[..., MLA_NOPE:]
    zk = lambda n: jnp.zeros((L, MLA_KV_RANK, H, n), w_ukv.dtype)
    k_main = jnp.concatenate([k_nope, zk(HEAD_PAD - MLA_NOPE)], axis=-1).reshape(L, MLA_KV_RANK, H * HEAD_PAD)
    even = (jnp.arange(H) % 2 == 0)[None, None, :, None]
    v_pair = jnp.where(even, jnp.concatenate([v, zk(MLA_V)], axis=-1), jnp.concatenate([zk(MLA_V), v], axis=-1))
    wkv2 = jnp.concatenate([k_main, v_pair.reshape(L, MLA_KV_RANK, H * HEAD_PAD)], axis=-1).astype(BF16)
    return wq2, wkv2


def _rope_tables(S):
    half = MLA_ROPE // 2
    inv_freq = jnp.power(ROPE_THETA, -jnp.arange(0, MLA_ROPE, 2, dtype=F32) / MLA_ROPE)
    ang = jnp.arange(S, dtype=F32)[:, None] * inv_freq[None, :]
    cos, sin = jnp.cos(ang), jnp.sin(ang)
    pad = jnp.zeros((S, HEAD_PAD - MLA_NOPE - MLA_ROPE), F32)
    cos_t = jnp.concatenate([jnp.ones((S, MLA_NOPE), F32), cos, cos, pad], axis=-1)
    sin_t = jnp.concatenate([jnp.zeros((S, MLA_NOPE), F32), -sin, sin, pad], axis=-1)
    scale = (MLA_NOPE + MLA_ROPE) ** -0.5
    return cos_t * scale, sin_t * scale, cos_t, sin_t


def _block_diag(blocks):
    L, G, n, _ = blocks.shape
    eye = jnp.eye(G, dtype=blocks.dtype)
    return jnp.einsum('lgij,gh->lgihj', blocks, eye).reshape(L, G * n, G * n)


def kernel(x, c, w_in, mla_q_norm, mla_kv_norm, mla_w_uq, mla_w_ukv, gdn_conv, gdn_a_log, gdn_dt_bias, gdn_out_norm, pool_w, pool_scale, w_out, w_mod, b_mod, ln1_g, ln1_b, ln2_g, ln2_b, router_w_group, router_b_group, router_w_expert, router_b_expert, moe_w_gate, moe_w_up, moe_w_down):
    B, S, D = x.shape
    L = w_in.shape[0]
    T = B * S
    alpha = (2 * L) ** 0.25
    ts = min(512, S)
    tq = min(256, S)
    t_moe = min(256, S)
    assert D == D_MODEL and S % ts == 0 and S % tq == 0 and ts % CHUNK == 0

    w_in2 = _prep_w_in(w_in)
    wq2, wkv2 = _prep_mla(mla_w_uq, mla_w_ukv)
    tabs = _rope_tables(S)
    gq = mla_q_norm.reshape(L, 1, MLA_Q_RANK)
    gkv = mla_kv_norm.reshape(L, 1, MLA_KV_RANK)
    alog_e = jnp.repeat(gdn_a_log, GDN_DIM, axis=-1).reshape(L, 1, GDN_WIDTH)
    dt_e = jnp.repeat(gdn_dt_bias, GDN_DIM, axis=-1).reshape(L, 1, GDN_WIDTH)
    og_e = jnp.tile(gdn_out_norm, (1, GDN_HEADS)).reshape(L, 1, GDN_WIDTH)
    lane_head = jnp.arange(GDN_WIDTH) // GDN_DIM
    seg = (lane_head[:, None] == lane_head[None, :]).astype(BF16)
    tri_c = (jnp.arange(CHUNK)[:, None] >= jnp.arange(CHUNK)[None, :]).astype(BF16)
    pool_bd = _block_diag(pool_w).astype(BF16)
    pool_sc = pool_scale.reshape(L, 1, POOL_WIDTH)
    w_out_b = w_out.astype(BF16)
    w_r = jnp.concatenate([router_w_group, router_w_expert,
                           jnp.zeros((L, D, LANE - N_GROUPS - N_EXPERTS), F32)], axis=-1)
    w_r_hi = w_r.astype(BF16)
    w_r_lo = (w_r - w_r_hi.astype(F32)).astype(BF16)
    b_r = jnp.concatenate([router_b_group, router_b_expert,
                           jnp.zeros((L, LANE - N_GROUPS - N_EXPERTS), F32)], axis=-1).reshape(L, 1, LANE)
    tri_r = (jnp.arange(ts)[:, None] > jnp.arange(ts)[None, :]).astype(BF16)
    ln1g, ln1b = ln1_g.reshape(L, 1, D), ln1_b.reshape(L, 1, D)
    ln2g, ln2b = ln2_g.reshape(L, 1, D), ln2_b.reshape(L, 1, D)

    A = T * TOP_K
    nb = (A + N_EXPERTS * (MOE_ROWS - 1) + MOE_ROWS - 1) // MOE_ROWS
    P = nb * MOE_ROWS
    n_fill = P - A

    mod = _modulation(c, w_mod, b_mod)
    x2 = x.reshape(T, D)
    for l in range(L):
        mod_l = mod[l].reshape(B, 1, 6 * D)
        proj = _inproj(x2, mod_l, w_in2, l, S, ts)
        q, k, v = _mla_proj(proj, tabs, gq, gkv, wq2, wkv2, l, S, ts)
        y_mla = _attention(q, k, v, B, S, tq)
        qh, kh, vh, g, beta = _gdn_pre(proj, gdn_conv, alog_e, dt_e, seg, l, B, S, ts)
        y_gdn = _gdn(qh, kh, vh, g, beta, proj, og_e, seg, tri_c, l, B, S, ts)
        y_pool = _pool(proj, pool_bd, pool_sc, l, B, S, ts)
        x2 = _outproj(y_mla, y_gdn, y_pool, w_out_b, x2, mod_l, ln1g, ln1b, l, S, ts, alpha)

        h2, meta, cnt = _router(x2, mod_l, w_r_hi, w_r_lo, b_r, tri_r, l, S, ts)
        counts = cnt[0, :N_EXPERTS].astype(jnp.int32)
        padded = (counts + MOE_ROWS - 1) // MOE_ROWS * MOE_ROWS
        pad_end = jnp.cumsum(padded)
        pad_start = pad_end - padded
        experts = meta[:, 0:2].astype(jnp.int32)
        ranks = meta[:, 2:4].astype(jnp.int32)
        dest = (pad_start[experts] + ranks).reshape(A)
        blk0 = jnp.arange(nb, dtype=jnp.int32) * MOE_ROWS
        block_expert = jnp.clip(jnp.searchsorted(pad_end, blk0, side='right'), 0, N_EXPERTS - 1).astype(jnp.int32)
        used = (blk0 < pad_end[-1]).astype(jnp.int32)
        changed = jnp.concatenate([jnp.ones((1,), jnp.int32),
                                   (block_expert[1:] != block_expert[:-1]).astype(jnp.int32)])
        block_flags = used + 2 * changed
        pos = jnp.arange(P, dtype=jnp.int32)
        seg_of = jnp.clip(jnp.searchsorted(pad_end, pos, side='right'), 0, N_EXPERTS - 1)
        is_fill = (pos >= pad_start[seg_of] + counts[seg_of]) | (pos >= pad_end[-1])
        fill_rows = jnp.nonzero(is_fill, size=n_fill, fill_value=0)[0].astype(jnp.int32)

        xs = _dispatch(h2, dest, fill_rows, P, t_moe)
        ys = _ffn(xs, block_expert, block_flags, moe_w_gate, moe_w_up, moe_w_down, l)
        x2 = _combine(ys, dest, meta, x2, mod_l, ln2g, ln2b, l, S, t_moe, alpha)
    return x2.reshape(B, S, D)
```

```python
import functools
import math

import jax
import jax.numpy as jnp
from jax import lax
from jax.experimental import pallas as pl
from jax.experimental.pallas import tpu as pltpu

F32 = jnp.float32
BF16 = jnp.bfloat16

D_MODEL = 1024
CHUNK = 64
MLA_HEADS = 8
MLA_NOPE = 64
MLA_ROPE = 32
MLA_V = 64
MLA_Q_RANK = 256
MLA_KV_RANK = 128
MLA_WIDTH = MLA_HEADS * MLA_V
ROPE_THETA = 10000.0
GDN_HEADS = 4
GDN_DIM = 64
GDN_WIDTH = GDN_HEADS * GDN_DIM
GDN_CONV = 4
POOL_WIDTH = 256
POOL_WINDOWS = (2, 4, 8, 16)
POOL_GROUP_DIM = 64
N_GROUPS = 4
EXPERTS_PER_GROUP = 8
N_EXPERTS = N_GROUPS * EXPERTS_PER_GROUP
TOP_K = 2
EXPERT_FF = 256
LN_EPS = 1e-5
RMS_EPS = 1e-6

LANE = 128
HEAD_PAD = 128
PAIRS_PER_GROUP = EXPERTS_PER_GROUP * (EXPERTS_PER_GROUP - 1) // 2
N_BUCKETS = N_GROUPS * PAIRS_PER_GROUP
MOE_ROWS = 128
VMEM_LIMIT = 48 * 1024 * 1024

_C_GQ, _C_GK, _C_GV, _C_GZ = 0, 256, 512, 768
_C_A, _C_B = 1024, 1280
_C_CQ, _C_POOL = 1536, 1792
_C_CKV, _C_KRA, _C_KRB = 2048, 2176, 2304
IN_COLS = 2432


def _cparams(sem):
    return pltpu.CompilerParams(dimension_semantics=sem, vmem_limit_bytes=VMEM_LIMIT)


def _sigmoid(x):
    return 1.0 / (1.0 + jnp.exp(-x))


def _split2(x):
    hi = x.astype(BF16)
    lo = (x - hi.astype(F32)).astype(BF16)
    return hi, lo


def _split3(x):
    hi = x.astype(BF16)
    r = x - hi.astype(F32)
    mid = r.astype(BF16)
    lo = (r - mid.astype(F32)).astype(BF16)
    return hi, mid, lo


def _dot(a, b):
    return jnp.dot(a, b, preferred_element_type=F32)


def _dot_nt(a, b):
    return lax.dot_general(a, b, (((1,), (1,)), ((), ())), preferred_element_type=F32)


def _dot_tn(a, b):
    return lax.dot_general(a, b, (((0,), (0,)), ((), ())), preferred_element_type=F32)


def _mod_body(c_ref, w_ref, b_ref, o_ref):
    c = c_ref[...]
    ca = c * _sigmoid(c)
    o_ref[0] = _dot(ca.astype(BF16), w_ref[0].astype(BF16)) + b_ref[0]


def _modulation(c, w_mod, b_mod):
    L, D, N = w_mod.shape
    B = c.shape[0]
    tn = 1024
    return pl.pallas_call(
        _mod_body,
        grid=(L, N // tn),
        in_specs=[pl.BlockSpec((B, D), lambda l, j: (0, 0)),
                  pl.BlockSpec((1, D, tn), lambda l, j: (l, 0, j)),
                  pl.BlockSpec((1, 1, tn), lambda l, j: (l, 0, j))],
        out_specs=pl.BlockSpec((1, B, tn), lambda l, j: (l, 0, j)),
        out_shape=jax.ShapeDtypeStruct((L, B, N), F32),
        compiler_params=_cparams(("arbitrary", "arbitrary")),
        name="modulation",
    )(c, w_mod, b_mod.reshape(L, 1, N))


def _inproj_body(x_ref, sh_ref, sc_ref, w_ref, o_ref):
    h = x_ref[...] * (1.0 + sc_ref[0]) + sh_ref[0]
    o_ref[...] = _dot(h.astype(BF16), w_ref[0])


def _inproj(x2, mod_l, w_in2, l, S, tm):
    T, D = x2.shape
    nS = S // tm
    return pl.pallas_call(
        _inproj_body,
        grid=(T // tm,),
        in_specs=[pl.BlockSpec((tm, D), lambda i: (i, 0)),
                  pl.BlockSpec((1, 1, D), lambda i: (i // nS, 0, 0)),
                  pl.BlockSpec((1, 1, D), lambda i: (i // nS, 0, 1)),
                  pl.BlockSpec((1, D, IN_COLS), lambda i: (l, 0, 0))],
        out_specs=pl.BlockSpec((tm, IN_COLS), lambda i: (i, 0)),
        out_shape=jax.ShapeDtypeStruct((T, IN_COLS), F32),
        compiler_params=_cparams(("arbitrary",)),
        name="inproj",
    )(x2, mod_l, mod_l, w_in2)


def _mla_proj_body(cq_ref, ckv_ref, kra_ref, krb_ref, cosq_ref, sinq_ref, cosk_ref, sink_ref,
                   gq_ref, gkv_ref, wq_ref, wkv_ref, q_out, k_out, v_out):
    cq = cq_ref[...]
    qn = cq * lax.rsqrt(jnp.mean(cq * cq, axis=-1, keepdims=True) + RMS_EPS) * gq_ref[0]
    q2 = _dot(qn.astype(BF16), wq_ref[0])
    ckv = ckv_ref[...]
    kvn = ckv * lax.rsqrt(jnp.mean(ckv * ckv, axis=-1, keepdims=True) + RMS_EPS) * gkv_ref[0]
    kv2 = _dot(kvn.astype(BF16), wkv_ref[0])
    cq_t, sq_t = cosq_ref[...], sinq_ref[...]
    krope = kra_ref[...] * cosk_ref[...] + krb_ref[...] * sink_ref[...]
    hw = MLA_HEADS * HEAD_PAD
    for h in range(MLA_HEADS):
        a, b = h * HEAD_PAD, (h + 1) * HEAD_PAD
        q_out[:, a:b] = (q2[:, a:b] * cq_t + q2[:, hw + a:hw + b] * sq_t).astype(BF16)
        k_out[:, a:b] = (kv2[:, a:b] + krope).astype(BF16)
    vl = lax.broadcasted_iota(jnp.int32, (1, hw), 1)
    ones_lane = jnp.where((vl // HEAD_PAD) % 2 == 0, MLA_V, 0)
    v_out[...] = (kv2[:, hw:] + (vl % HEAD_PAD == ones_lane).astype(F32)).astype(BF16)


def _mla_proj(proj, tabs, gq, gkv, wq2, wkv2, l, S, ts):
    T = proj.shape[0]
    nS = S // ts
    hw = MLA_HEADS * HEAD_PAD
    tab_spec = pl.BlockSpec((ts, LANE), lambda i: (i % nS, 0))
    return pl.pallas_call(
        _mla_proj_body,
        grid=(T // ts,),
        in_specs=[pl.BlockSpec((ts, 256), lambda i: (i, _C_CQ // 256)),
                  pl.BlockSpec((ts, 128), lambda i: (i, _C_CKV // 128)),
                  pl.BlockSpec((ts, 128), lambda i: (i, _C_KRA // 128)),
                  pl.BlockSpec((ts, 128), lambda i: (i, _C_KRB // 128)),
                  tab_spec, tab_spec, tab_spec, tab_spec,
                  pl.BlockSpec((1, 1, MLA_Q_RANK), lambda i: (l, 0, 0)),
                  pl.BlockSpec((1, 1, MLA_KV_RANK), lambda i: (l, 0, 0)),
                  pl.BlockSpec((1, MLA_Q_RANK, 2 * hw), lambda i: (l, 0, 0)),
                  pl.BlockSpec((1, MLA_KV_RANK, 2 * hw), lambda i: (l, 0, 0))],
        out_specs=[pl.BlockSpec((ts, hw), lambda i: (i, 0))] * 3,
        out_shape=[jax.ShapeDtypeStruct((T, hw), BF16)] * 3,
        compiler_params=_cparams(("arbitrary",)),
        name="mla_proj",
    )(proj, proj, proj, proj, *tabs, gq, gkv, wq2, wkv2)


def _attn_body(q_ref, k_ref, v_ref, o_ref, *, tq):
    qi = pl.program_id(2)
    row_c = lax.broadcasted_iota(jnp.int32, (tq, tq), 0) // CHUNK
    col_c = lax.broadcasted_iota(jnp.int32, (tq, tq), 1) // CHUNK
    allowed = col_c <= row_c
    sl = [slice(hh * HEAD_PAD, (hh + 1) * HEAD_PAD) for hh in range(2)]
    qs = [q_ref[:, s_] for s_ in sl]

    def scores(r0):
        kk = k_ref[pl.ds(r0, tq), :]
        return tuple(_dot_nt(qs[hh], kk[:, sl[hh]]) for hh in range(2))

    def consume(s_pair, carry, r0, mask):
        vv = v_ref[pl.ds(r0, tq), :]
        hs = range(2)
        s = [s_pair[hh] if mask is None else jnp.where(mask, s_pair[hh], -jnp.inf) for hh in hs]
        m_new = [jnp.maximum(carry[hh][0], jnp.max(s[hh], axis=-1, keepdims=True)) for hh in hs]
        p = [jnp.exp((s[hh] - m_new[hh]).astype(BF16)) for hh in hs]
        pv = [_dot(p[hh], vv[:, sl[hh]]) for hh in hs]
        alpha = [jnp.exp(carry[hh][0] - m_new[hh]) for hh in hs]
        return tuple((m_new[hh], alpha[hh] * carry[hh][1] + pv[hh]) for hh in hs)

    def step(j, c):
        s_cur, carry = c
        s_next = scores(pl.multiple_of((j + 1) * tq, tq))
        return s_next, consume(s_cur, carry, pl.multiple_of(j * tq, tq), None)

    init = tuple((jnp.full((tq, 1), -jnp.inf, F32), jnp.zeros((tq, HEAD_PAD), F32)) for _ in range(2))
    s_last, carry = lax.fori_loop(0, qi, step, (scores(0), init))
    (_, acc0), (_, acc1) = consume(s_last, carry, pl.multiple_of(qi * tq, tq), allowed)
    lane = lax.broadcasted_iota(jnp.int32, (tq, HEAD_PAD), 1)
    o0 = jnp.where(lane < MLA_V, acc0 * (1.0 / acc0[:, MLA_V:MLA_V + 1]), 0.0)
    o1 = jnp.where(lane >= MLA_V, acc1 * (1.0 / acc1[:, 0:1]), 0.0)
    o_ref[...] = (o0 + o1).astype(BF16)


def _attention(q, k, v, B, S, tq):
    T = q.shape[0]
    nq = S // tq
    pairs = MLA_HEADS // 2
    return pl.pallas_call(
        functools.partial(_attn_body, tq=tq),
        grid=(B, pairs, nq),
        in_specs=[pl.BlockSpec((tq, 2 * HEAD_PAD), lambda b, p, i: (b * nq + i, p)),
                  pl.BlockSpec((S, 2 * HEAD_PAD), lambda b, p, i: (b, p)),
                  pl.BlockSpec((S, 2 * HEAD_PAD), lambda b, p, i: (b, p))],
        out_specs=pl.BlockSpec((tq, 2 * MLA_V), lambda b, p, i: (b * nq + i, p)),
        out_shape=jax.ShapeDtypeStruct((T, MLA_WIDTH), BF16),
        compiler_params=_cparams(("arbitrary", "arbitrary", "arbitrary")),
        name="mla_attention",
    )(q, k, v)


def _pool_body(p_ref, w_ref, sc_ref, o_ref, prev_ref, *, ts):
    j = pl.program_id(1)
    hist = 16

    @pl.when(j == 0)
    def _():
        prev_ref[...] = jnp.zeros_like(prev_ref)

    cur = p_ref[...]
    x = jnp.concatenate([prev_ref[...], cur], axis=0)
    s1 = x + pltpu.roll(x, 1, axis=0)
    s2 = s1 + pltpu.roll(s1, 2, axis=0)
    s4 = s2 + pltpu.roll(s2, 4, axis=0)
    s8 = s4 + pltpu.roll(s4, 8, axis=0)
    pos = (j * ts + lax.broadcasted_iota(jnp.int32, (ts, POOL_WIDTH), 0) + 1).astype(F32)
    lane = lax.broadcasted_iota(jnp.int32, (ts, POOL_WIDTH), 1)
    sums = (s1, s2, s4, s8)
    mean = None
    for gi, win in enumerate(POOL_WINDOWS):
        m_g = sums[gi][hist:] / jnp.minimum(pos, float(win))
        mean = m_g if mean is None else jnp.where(lane >= gi * POOL_GROUP_DIM, m_g, mean)
    delta = mean - cur
    o_ref[...] = (_dot(delta.astype(BF16), w_ref[0]) * sc_ref[0]).astype(BF16)
    prev_ref[...] = cur[ts - hist:]


def _pool(proj, w_bd, scale, l, B, S, ts):
    T = proj.shape[0]
    nS = S // ts
    return pl.pallas_call(
        functools.partial(_pool_body, ts=ts),
        grid=(B, nS),
        in_specs=[pl.BlockSpec((ts, POOL_WIDTH), lambda b, j: (b * nS + j, _C_POOL // 256)),
                  pl.BlockSpec((1, POOL_WIDTH, POOL_WIDTH), lambda b, j: (l, 0, 0)),
                  pl.BlockSpec((1, 1, POOL_WIDTH), lambda b, j: (l, 0, 0))],
        out_specs=pl.BlockSpec((ts, POOL_WIDTH), lambda b, j: (b * nS + j, 0)),
        out_shape=jax.ShapeDtypeStruct((T, POOL_WIDTH), BF16),
        scratch_shapes=[pltpu.VMEM((16, POOL_WIDTH), F32)],
        compiler_params=_cparams(("arbitrary", "arbitrary")),
        name="pool_mixer",
    )(proj, w_bd, scale)


def _head_sum(x, seg_ref):
    hi, lo = _split2(x)
    return _dot(hi, seg_ref[...]) + _dot(lo, seg_ref[...])


def _gdn_pre_body(q_ref, k_ref, v_ref, a_ref, b_ref, conv_ref, alog_ref, dt_ref, seg_ref,
                  qo_ref, ko_ref, vo_ref, go_ref, bo_ref, prev_ref, *, ts):
    j = pl.program_id(1)
    hist = 8

    @pl.when(j == 0)
    def _():
        prev_ref[...] = jnp.zeros_like(prev_ref)

    cw = conv_ref[0]
    outs = []
    for idx, ref in enumerate((q_ref, k_ref, v_ref)):
        cur = ref[...]
        a, b = idx * GDN_WIDTH, (idx + 1) * GDN_WIDTH
        x = jnp.concatenate([prev_ref[:, a:b], cur], axis=0)
        w = cw[:, a:b]
        y = (w[3:4] * x + w[2:3] * pltpu.roll(x, 1, axis=0) + w[1:2] * pltpu.roll(x, 2, axis=0)
             + w[0:1] * pltpu.roll(x, 3, axis=0))[hist:]
        outs.append(y * _sigmoid(y))
        prev_ref[:, a:b] = cur[ts - hist:]
    qc, kc, vc = outs
    qo_ref[...] = qc * lax.rsqrt(_head_sum(qc * qc, seg_ref) + RMS_EPS) * (GDN_DIM ** -0.5)
    ko_ref[...] = kc * lax.rsqrt(_head_sum(kc * kc, seg_ref) + RMS_EPS)
    vo_ref[...] = vc
    z = a_ref[...] + dt_ref[0]
    softplus = jnp.maximum(z, 0.0) + jnp.log(1.0 + jnp.exp(-jnp.abs(z)))
    go_ref[...] = -jnp.exp(alog_ref[0]) * softplus
    bo_ref[...] = _sigmoid(b_ref[...])


def _gdn_pre(proj, conv_w, alog_e, dt_e, seg, l, B, S, ts):
    T = proj.shape[0]
    nS = S // ts
    W = GDN_WIDTH

    def col(c):
        return pl.BlockSpec((ts, W), lambda b, j: (b * nS + j, c // W))

    return pl.pallas_call(
        functools.partial(_gdn_pre_body, ts=ts),
        grid=(B, nS),
        in_specs=[col(_C_GQ), col(_C_GK), col(_C_GV), col(_C_A), col(_C_B),
                  pl.BlockSpec((1, GDN_CONV, 3 * W), lambda b, j: (l, 0, 0)),
                  pl.BlockSpec((1, 1, W), lambda b, j: (l, 0, 0)),
                  pl.BlockSpec((1, 1, W), lambda b, j: (l, 0, 0)),
                  pl.BlockSpec((W, W), lambda b, j: (0, 0))],
        out_specs=[pl.BlockSpec((ts, W), lambda b, j: (b * nS + j, 0))] * 5,
        out_shape=[jax.ShapeDtypeStruct((T, W), F32)] * 5,
        scratch_shapes=[pltpu.VMEM((8, 3 * W), F32)],
        compiler_params=_cparams(("arbitrary", "arbitrary")),
        name="gdn_pre",
    )(proj, proj, proj, proj, proj, conv_w, alog_e, dt_e, seg)


def _gdn_body(q_ref, k_ref, v_ref, g_ref, b_ref, z_ref, og_ref, seg_ref, tri_ref,
              o_ref, state_ref, u_s, w_s, qk_s, qd_s, kd_s, gl_s, *, ts):
    j = pl.program_id(1)
    C, W, H = CHUNK, GDN_WIDTH, GDN_HEADS
    n_chunks = ts // C

    @pl.when(j == 0)
    def _():
        state_ref[...] = jnp.zeros_like(state_ref)

    lane = lax.broadcasted_iota(jnp.int32, (C, W), 1)
    row = lax.broadcasted_iota(jnp.int32, (C, W), 0)
    col_tok = lane % C
    incl = row >= col_tok
    strict = row > col_tok
    eye = (row == col_tok).astype(F32)
    head_masks = [(lane // C == h).astype(F32) for h in range(H)]
    bd_mask = (lax.broadcasted_iota(jnp.int32, (W, W), 0) // C
               == lax.broadcasted_iota(jnp.int32, (W, W), 1) // C)

    def expand(m):
        return jnp.concatenate([m * hm for hm in head_masks], axis=0)

    def split_dot(lhs, rhs):
        l_hi, l_lo = _split2(lhs)
        r_hi, r_lo = _split2(rhs)
        return _dot(l_hi, r_hi) + _dot(l_lo, r_hi) + _dot(l_hi, r_lo)

    def intra(i, _):
        cs = [i * INTRA_UNROLL + uu for uu in range(INTRA_UNROLL)]
        r0s = [pl.multiple_of(c * C, C) for c in cs]
        n = range(INTRA_UNROLL)
        tri = tri_ref[...]
        gs = [_split3(g_ref[pl.ds(r0, C), :]) for r0 in r0s]
        gc = [_dot(tri, g[0]) + _dot(tri, g[1]) + _dot(tri, g[2]) for g in gs]
        ks = [k_ref[pl.ds(r0, C), :] for r0 in r0s]
        qs = [q_ref[pl.ds(r0, C), :] for r0 in r0s]
        betas = [b_ref[pl.ds(r0, C), :] for r0 in r0s]
        kb = [ks[x] * betas[x] for x in n]
        aq = [_dot_nt(jnp.concatenate([kb[x], qs[x]], axis=0).astype(BF16), expand(ks[x]).astype(BF16)) for x in n]
        g_row = [jnp.sum(gc[x] * eye, axis=0, keepdims=True) for x in n]
        g_last = [gc[x][C - 1:C, :] for x in n]
        decay = [jnp.where(incl, jnp.exp(jnp.where(incl, gc[x] - g_row[x], 0.0)), 0.0) for x in n]
        e_gc = [jnp.exp(gc[x]) for x in n]
        a_cat = [jnp.where(strict, aq[x][:C] * decay[x], 0.0) for x in n]
        for x in n:
            r0 = r0s[x]
            qk_s[pl.ds(r0, C), :] = jnp.where(incl, aq[x][C:] * decay[x], 0.0).astype(BF16)
            qd_s[pl.ds(r0, C), :] = (qs[x] * e_gc[x]).astype(BF16)
            kd_s[pl.ds(r0, C), :] = (ks[x] * jnp.exp(g_last[x] - gc[x])).astype(BF16)
            gl_s[pl.ds(pl.multiple_of(cs[x] * 8, 8), 8), :] = jnp.broadcast_to(jnp.exp(g_last[x]), (8, W))
        x_cat = [eye - a_cat[x] for x in n]
        p_cat = [split_dot(a_cat[x], expand(a_cat[x])) for x in n]
        n_fac = int(math.log2(C)) - 1
        for r in range(n_fac):
            last = r == n_fac - 1
            xp = [split_dot(x_cat[x] if last else jnp.concatenate([x_cat[x], p_cat[x]], axis=0), expand(p_cat[x]))
                  for x in n]
            x_cat = [x_cat[x] + xp[x][:C] for x in n]
            if not last:
                p_cat = [xp[x][C:] for x in n]
        for x in n:
            r0 = r0s[x]
            t_cat = x_cat[x].astype(BF16)
            v = v_ref[pl.ds(r0, C), :]
            u_s[pl.ds(r0, C), :] = _dot(t_cat, expand(v * betas[x]).astype(BF16))
            w_s[pl.ds(r0, C), :] = _dot(t_cat, expand(kb[x] * e_gc[x]).astype(BF16)).astype(BF16)
        return 0

    lax.fori_loop(0, n_chunks // INTRA_UNROLL, intra, 0)

    def scan(c, _):
        r0 = pl.multiple_of(c * C, C)
        state = state_ref[...]
        wq = _dot(jnp.concatenate([w_s[pl.ds(r0, C), :], qd_s[pl.ds(r0, C), :]], axis=0), state.astype(BF16))
        v_new = u_s[pl.ds(r0, C), :] - wq[:C]
        o = wq[C:] + _dot(qk_s[pl.ds(r0, C), :], expand(v_new).astype(BF16))
        upd = _dot_tn(kd_s[pl.ds(r0, C), :], v_new.astype(BF16))
        g_l = gl_s[pl.ds(pl.multiple_of(c * 8, 8), 1), :]
        state_ref[...] = state * g_l + jnp.where(bd_mask, upd, 0.0)
        ms = _head_sum(o * o, seg_ref) * (1.0 / GDN_DIM)
        z = z_ref[pl.ds(r0, C), :]
        y = o * lax.rsqrt(ms + RMS_EPS) * og_ref[0] * (z * _sigmoid(z))
        o_ref[pl.ds(r0, C), :] = y.astype(BF16)
        return 0

    lax.fori_loop(0, n_chunks, scan, 0)


def _gdn(qh, kh, vh, g, beta, proj, og, seg, tri, l, B, S, ts):
    T = qh.shape[0]
    nS = S // ts
    W = GDN_WIDTH
    row = pl.BlockSpec((ts, W), lambda b, j: (b * nS + j, 0))
    return pl.pallas_call(
        functools.partial(_gdn_body, ts=ts),
        grid=(B, nS),
        in_specs=[row, row, row, row, row,
                  pl.BlockSpec((ts, W), lambda b, j: (b * nS + j, _C_GZ // W)),
                  pl.BlockSpec((1, 1, W), lambda b, j: (l, 0, 0)),
                  pl.BlockSpec((W, W), lambda b, j: (0, 0)),
                  pl.BlockSpec((CHUNK, CHUNK), lambda b, j: (0, 0))],
        out_specs=pl.BlockSpec((ts, W), lambda b, j: (b * nS + j, 0)),
        out_shape=jax.ShapeDtypeStruct((T, W), BF16),
        scratch_shapes=[pltpu.VMEM((W, W), F32), pltpu.VMEM((ts, W), F32)]
        + [pltpu.VMEM((ts, W), BF16)] * 4 + [pltpu.VMEM((8 * (ts // CHUNK), W), F32)],
        compiler_params=_cparams(("arbitrary", "arbitrary")),
        name="gdn_delta_rule",
    )(qh, kh, vh, g, beta, proj, og, seg, tri)


def _layer_norm(r, g, b):
    mu = jnp.mean(r, axis=-1, keepdims=True)
    d = r - mu
    var = jnp.mean(d * d, axis=-1, keepdims=True)
    return d * lax.rsqrt(var + LN_EPS) * g + b


def _outproj_body(ym_ref, yg_ref, yp_ref, w_ref, x_ref, gt_ref, lg_ref, lb_ref, o_ref, *, alpha):
    w = w_ref[0]
    y = (_dot(ym_ref[...], w[:MLA_WIDTH]) + _dot(yg_ref[...], w[MLA_WIDTH:MLA_WIDTH + GDN_WIDTH])
         + _dot(yp_ref[...], w[MLA_WIDTH + GDN_WIDTH:]))
    r = alpha * x_ref[...] + (1.0 + gt_ref[0]) * y
    o_ref[...] = _layer_norm(r, lg_ref[0], lb_ref[0])


def _outproj(y_mla, y_gdn, y_pool, w_out, x2, mod_l, ln_g, ln_b, l, S, tm, alpha):
    T, D = x2.shape
    nS = S // tm
    vec = pl.BlockSpec((1, 1, D), lambda i: (l, 0, 0))
    return pl.pallas_call(
        functools.partial(_outproj_body, alpha=alpha),
        grid=(T // tm,),
        in_specs=[pl.BlockSpec((tm, MLA_WIDTH), lambda i: (i, 0)),
                  pl.BlockSpec((tm, GDN_WIDTH), lambda i: (i, 0)),
                  pl.BlockSpec((tm, POOL_WIDTH), lambda i: (i, 0)),
                  pl.BlockSpec((1, D, D), lambda i: (l, 0, 0)),
                  pl.BlockSpec((tm, D), lambda i: (i, 0)),
                  pl.BlockSpec((1, 1, D), lambda i: (i // nS, 0, 2)),
                  vec, vec],
        out_specs=pl.BlockSpec((tm, D), lambda i: (i, 0)),
        out_shape=jax.ShapeDtypeStruct((T, D), F32),
        compiler_params=_cparams(("arbitrary",)),
        name="outproj_ln",
    )(y_mla, y_gdn, y_pool, w_out, x2, mod_l, ln_g, ln_b)


def _lane_first(cond, lane_f):
    return jnp.min(jnp.where(cond, lane_f, float(LANE)), axis=-1, keepdims=True)


def _router_body(x_ref, sh_ref, sc_ref, whi_ref, wlo_ref, br_ref, tri_ref,
                 hm_out, cnt_out, carry_ref):
    i = pl.program_id(0)

    @pl.when(i == 0)
    def _():
        carry_ref[...] = jnp.zeros_like(carry_ref)

    h = x_ref[...] * (1.0 + sc_ref[0]) + sh_ref[0]
    hm_out[:, :D_MODEL] = h
    h_hi, h_lo = _split2(h)
    logits = _dot(h_hi, whi_ref[0]) + _dot(h_lo, whi_ref[0]) + _dot(h_hi, wlo_ref[0]) + br_ref[0]
    tm = logits.shape[0]
    lane = lax.broadcasted_iota(jnp.int32, (tm, LANE), 1)
    lane_f = lane.astype(F32)
    neg = -jnp.inf
    gl = jnp.where(lane < N_GROUPS, logits, neg)
    gmax = jnp.max(gl, axis=-1, keepdims=True)
    gsel = _lane_first(gl == gmax, lane_f)
    g_p = 1.0 / jnp.sum(jnp.exp(gl - gmax), axis=-1, keepdims=True)
    lo = N_GROUPS + EXPERTS_PER_GROUP * gsel
    el = jnp.where((lane_f >= lo) & (lane_f < lo + EXPERTS_PER_GROUP), logits, neg)
    m1 = jnp.max(el, axis=-1, keepdims=True)
    i1 = _lane_first(el == m1, lane_f)
    el2 = jnp.where(lane_f == i1, neg, el)
    m2 = jnp.max(el2, axis=-1, keepdims=True)
    i2 = _lane_first(el2 == m2, lane_f)
    t = jnp.exp(m2 - m1)
    w1 = g_p / (1.0 + t)
    w2 = g_p * t / (1.0 + t)
    loc1 = i1 - lo
    loc2 = i2 - lo
    a_loc = jnp.minimum(loc1, loc2)
    b_loc = jnp.maximum(loc1, loc2)
    pair = a_loc * (2 * EXPERTS_PER_GROUP - 1 - a_loc) * 0.5 + (b_loc - a_loc - 1.0)
    bucket = gsel * PAIRS_PER_GROUP + pair
    first_is_a = loc1 < loc2
    w_a = jnp.where(first_is_a, w1, w2)
    w_b = jnp.where(first_is_a, w2, w1)
    hit = lane_f == bucket
    onehot = hit.astype(BF16)
    before = _dot(tri_ref[...], onehot) + carry_ref[0:1, :]
    rank = jnp.sum(jnp.where(hit, before, 0.0), axis=-1, keepdims=True)
    total = carry_ref[0:1, :] + jnp.sum(onehot.astype(F32), axis=0, keepdims=True)
    carry_ref[...] = jnp.broadcast_to(total, carry_ref.shape)
    cnt_out[...] = jnp.broadcast_to(total, cnt_out.shape)
    meta = jnp.zeros((tm, LANE), F32)
    for idx, val in enumerate((bucket, rank, w_a, w_b)):
        meta = jnp.where(lane == idx, val, meta)
    hm_out[:, D_MODEL:] = meta


def _router(x2, mod_l, w_hi, w_lo, b_r, tri, l, S, tm):
    T, D = x2.shape
    nS = S // tm
    return pl.pallas_call(
        _router_body,
        grid=(T // tm,),
        in_specs=[pl.BlockSpec((tm, D), lambda i: (i, 0)),
                  pl.BlockSpec((1, 1, D), lambda i: (i // nS, 0, 3)),
                  pl.BlockSpec((1, 1, D), lambda i: (i // nS, 0, 4)),
                  pl.BlockSpec((1, D, LANE), lambda i: (l, 0, 0)),
                  pl.BlockSpec((1, D, LANE), lambda i: (l, 0, 0)),
                  pl.BlockSpec((1, 1, LANE), lambda i: (l, 0, 0)),
                  pl.BlockSpec((tm, tm), lambda i: (0, 0))],
        out_specs=[pl.BlockSpec((tm, D + LANE), lambda i: (i, 0)),
                   pl.BlockSpec((8, LANE), lambda i: (0, 0))],
        out_shape=[jax.ShapeDtypeStruct((T, D + LANE), F32),
                   jax.ShapeDtypeStruct((8, LANE), F32)],
        scratch_shapes=[pltpu.VMEM((8, LANE), F32)],
        compiler_params=_cparams(("arbitrary",)),
        name="router",
    )(x2, mod_l, mod_l, w_hi, w_lo, b_r, tri)


def _row_copy(src, s, dst, d, sem):
    return pltpu.make_async_copy(src.at[pl.ds(s, 1)], dst.at[pl.ds(d, 1)], sem)


DMA_WAIT_UNROLL = 32
INTRA_UNROLL = 4


def _issue_rows(n, make_copy):
    def body(r, _):
        make_copy(r).start()
        return 0

    lax.fori_loop(0, n, body, 0, unroll=8)


def _wait_rows(n, make_copy):
    def body(_, c):
        for _u in range(DMA_WAIT_UNROLL):
            make_copy(0).wait()
        return c

    lax.fori_loop(0, n // DMA_WAIT_UNROLL, body, 0)


def _dispatch_body(dest_ref, hm_ref, xs_in_ref, xs_ref, sem, *, tm):
    del xs_in_ref
    copy = lambda r: _row_copy(hm_ref, r, xs_ref, dest_ref[r], sem)
    _issue_rows(tm, copy)
    _wait_rows(tm, lambda r: _row_copy(hm_ref, 0, xs_ref, 0, sem))


def _dispatch(hm, dest, xs_buf, tm):
    T, W = hm.shape
    return pl.pallas_call(
        functools.partial(_dispatch_body, tm=tm),
        grid=(T // tm,),
        in_specs=[pl.BlockSpec((tm,), lambda i: (i,), memory_space=pltpu.SMEM),
                  pl.BlockSpec((tm, W), lambda i: (i, 0)),
                  pl.BlockSpec(memory_space=pl.ANY)],
        out_specs=pl.BlockSpec(memory_space=pl.ANY),
        out_shape=jax.ShapeDtypeStruct(xs_buf.shape, F32),
        input_output_aliases={2: 0},
        scratch_shapes=[pltpu.SemaphoreType.DMA],
        compiler_params=_cparams(("arbitrary",)),
        name="moe_dispatch",
    )(dest, hm, xs_buf)


def _ffn_body(ea_ref, eb_ref, fl_ref, xs_ref, wga_ref, wua_ref, wda_ref, wgb_ref, wub_ref, wdb_ref,
              o_ref, w1_s, w2_s):
    i = pl.program_id(0)
    flags = fl_ref[i]
    FF = EXPERT_FF

    @pl.when((flags & 2) != 0)
    def _():
        w1_s[:, 0:FF] = wga_ref[0, 0].astype(BF16)
        w1_s[:, FF:2 * FF] = wua_ref[0, 0].astype(BF16)
        w2_s[0:FF, :] = wda_ref[0, 0].astype(BF16)

    @pl.when((flags & 4) != 0)
    def _():
        w1_s[:, 2 * FF:3 * FF] = wgb_ref[0, 0].astype(BF16)
        w1_s[:, 3 * FF:4 * FF] = wub_ref[0, 0].astype(BF16)
        w2_s[FF:2 * FF, :] = wdb_ref[0, 0].astype(BF16)

    @pl.when((flags & 1) != 0)
    def _():
        xm = xs_ref[...]
        gu = _dot(xm[:, :D_MODEL].astype(BF16), w1_s[...])
        w_a = xm[:, D_MODEL + 2:D_MODEL + 3]
        w_b = xm[:, D_MODEL + 3:D_MODEL + 4]
        g_a, u_a, g_b, u_b = gu[:, 0:FF], gu[:, FF:2 * FF], gu[:, 2 * FF:3 * FF], gu[:, 3 * FF:4 * FF]
        act = jnp.concatenate([g_a * _sigmoid(g_a) * u_a * w_a, g_b * _sigmoid(g_b) * u_b * w_b], axis=1)
        o_ref[...] = _dot(act.astype(BF16), w2_s[...])

    @pl.when((flags & 1) == 0)
    def _():
        o_ref[...] = jnp.zeros_like(o_ref)


def _ffn(xs, ea, eb, flags, w_gate, w_up, w_down, l):
    P, W = xs.shape
    D = D_MODEL
    nb = P // MOE_ROWS
    FF = EXPERT_FF
    wa_up = pl.BlockSpec((1, 1, D, FF), lambda i, ea, eb, fl: (l, ea[i], 0, 0))
    wa_dn = pl.BlockSpec((1, 1, FF, D), lambda i, ea, eb, fl: (l, ea[i], 0, 0))
    wb_up = pl.BlockSpec((1, 1, D, FF), lambda i, ea, eb, fl: (l, eb[i], 0, 0))
    wb_dn = pl.BlockSpec((1, 1, FF, D), lambda i, ea, eb, fl: (l, eb[i], 0, 0))
    grid_spec = pltpu.PrefetchScalarGridSpec(
        num_scalar_prefetch=3,
        grid=(nb,),
        in_specs=[pl.BlockSpec((MOE_ROWS, W), lambda i, ea, eb, fl: (i, 0)),
                  wa_up, wa_up, wa_dn, wb_up, wb_up, wb_dn],
        out_specs=pl.BlockSpec((MOE_ROWS, D), lambda i, ea, eb, fl: (i, 0)),
        scratch_shapes=[pltpu.VMEM((D, 4 * FF), BF16), pltpu.VMEM((2 * FF, D), BF16)],
    )
    return pl.pallas_call(
        _ffn_body,
        grid_spec=grid_spec,
        out_shape=jax.ShapeDtypeStruct((P, D), F32),
        compiler_params=_cparams(("arbitrary",)),
        name="moe_ffn",
    )(ea, eb, flags, xs, w_gate, w_up, w_down, w_gate, w_up, w_down)


def _combine_body(dcur_ref, dnext_ref, x_ref, gt_ref, lg_ref, lb_ref, ys_ref, o_ref,
                  y0_ref, y1_ref, sems, *, tm, n_steps, alpha):
    i = pl.program_id(0)
    bufs = (y0_ref, y1_ref)

    def issue(dref, slot):
        _issue_rows(tm, lambda r: _row_copy(ys_ref, dref[r], bufs[slot], r, sems.at[slot]))

    def finish(slot):
        _wait_rows(tm, lambda r: _row_copy(ys_ref, 0, bufs[slot], 0, sems.at[slot]))
        r = alpha * x_ref[...] + (1.0 + gt_ref[0]) * bufs[slot][...]
        o_ref[...] = _layer_norm(r, lg_ref[0], lb_ref[0])

    @pl.when(i == 0)
    def _():
        issue(dcur_ref, 0)

    has_next = i + 1 < n_steps
    even = (i % 2) == 0

    @pl.when(has_next & even)
    def _():
        issue(dnext_ref, 1)

    @pl.when(has_next & jnp.logical_not(even))
    def _():
        issue(dnext_ref, 0)

    @pl.when(even)
    def _():
        finish(0)

    @pl.when(jnp.logical_not(even))
    def _():
        finish(1)


def _combine(ys, dest, x2, mod_l, ln_g, ln_b, l, S, tm, alpha):
    T, D = x2.shape
    nS = S // tm
    n_steps = T // tm
    vec = pl.BlockSpec((1, 1, D), lambda i: (l, 0, 0))
    return pl.pallas_call(
        functools.partial(_combine_body, tm=tm, n_steps=n_steps, alpha=alpha),
        grid=(n_steps,),
        in_specs=[pl.BlockSpec((tm,), lambda i: (i,), memory_space=pltpu.SMEM),
                  pl.BlockSpec((tm,), lambda i: (jnp.minimum(i + 1, n_steps - 1),), memory_space=pltpu.SMEM),
                  pl.BlockSpec((tm, D), lambda i: (i, 0)),
                  pl.BlockSpec((1, 1, D), lambda i: (i // nS, 0, 5)),
                  vec, vec,
                  pl.BlockSpec(memory_space=pl.ANY)],
        out_specs=pl.BlockSpec((tm, D), lambda i: (i, 0)),
        out_shape=jax.ShapeDtypeStruct((T, D), F32),
        scratch_shapes=[pltpu.VMEM((tm, D), F32), pltpu.VMEM((tm, D), F32), pltpu.SemaphoreType.DMA((2,))],
        compiler_params=_cparams(("arbitrary",)),
        name="moe_combine_ln",
    )(dest, dest, x2, mod_l, ln_g, ln_b, ys)


def _prep_w_in(w_in):
    L, D, _ = w_in.shape
    o = 0
    cq = w_in[..., o:o + MLA_Q_RANK]; o += MLA_Q_RANK
    ckv = w_in[..., o:o + MLA_KV_RANK]; o += MLA_KV_RANK
    kr = w_in[..., o:o + MLA_ROPE]; o += MLA_ROPE
    gq = w_in[..., o:o + GDN_WIDTH]; o += GDN_WIDTH
    gk = w_in[..., o:o + GDN_WIDTH]; o += GDN_WIDTH
    gv = w_in[..., o:o + GDN_WIDTH]; o += GDN_WIDTH
    gz = w_in[..., o:o + GDN_WIDTH]; o += GDN_WIDTH
    ga = w_in[..., o:o + GDN_HEADS]; o += GDN_HEADS
    gb = w_in[..., o:o + GDN_HEADS]; o += GDN_HEADS
    pw = w_in[..., o:o + POOL_WIDTH]
    half = MLA_ROPE // 2
    z = lambda n: jnp.zeros((L, D, n), w_in.dtype)
    kra = jnp.concatenate([z(MLA_NOPE), kr, z(HEAD_PAD - MLA_NOPE - MLA_ROPE)], axis=-1)
    krb = jnp.concatenate([z(MLA_NOPE), kr[..., half:], kr[..., :half], z(HEAD_PAD - MLA_NOPE - MLA_ROPE)], axis=-1)
    a_e = jnp.repeat(ga, GDN_DIM, axis=-1)
    b_e = jnp.repeat(gb, GDN_DIM, axis=-1)
    out = jnp.concatenate([gq, gk, gv, gz, a_e, b_e, cq, pw, ckv, kra, krb], axis=-1)
    assert out.shape[-1] == IN_COLS
    return out.astype(BF16)


def _prep_mla(w_uq, w_ukv):
    L = w_uq.shape[0]
    H, half = MLA_HEADS, MLA_ROPE // 2
    pad = HEAD_PAD - MLA_NOPE - MLA_ROPE
    q = w_uq.reshape(L, MLA_Q_RANK, H, MLA_NOPE + MLA_ROPE)
    nope, r1, r2 = q[..., :MLA_NOPE], q[..., MLA_NOPE:MLA_NOPE + half], q[..., MLA_NOPE + half:]
    zq = lambda n: jnp.zeros((L, MLA_Q_RANK, H, n), w_uq.dtype)
    plain = jnp.concatenate([nope, r1, r2, zq(pad)], axis=-1).reshape(L, MLA_Q_RANK, H * HEAD_PAD)
    partner = jnp.concatenate([zq(MLA_NOPE), r2, r1, zq(pad)], axis=-1).reshape(L, MLA_Q_RANK, H * HEAD_PAD)
    wq2 = jnp.concatenate([plain, partner], axis=-1).astype(BF16)
    kv = w_ukv.reshape(L, MLA_KV_RANK, H, MLA_NOPE + MLA_V)
    k_nope, v = kv[..., :MLA_NOPE], kv[..., MLA_NOPE:]
    zk = lambda n: jnp.zeros((L, MLA_KV_RANK, H, n), w_ukv.dtype)
    k_main = jnp.concatenate([k_nope, zk(HEAD_PAD - MLA_NOPE)], axis=-1).reshape(L, MLA_KV_RANK, H * HEAD_PAD)
    even = (jnp.arange(H) % 2 == 0)[None, None, :, None]
    v_pair = jnp.where(even, jnp.concatenate([v, zk(MLA_V)], axis=-1), jnp.concatenate([zk(MLA_V), v], axis=-1))
    wkv2 = jnp.concatenate([k_main, v_pair.reshape(L, MLA_KV_RANK, H * HEAD_PAD)], axis=-1).astype(BF16)
    return wq2, wkv2


def _rope_tables(S):
    half = MLA_ROPE // 2
    inv_freq = jnp.power(ROPE_THETA, -jnp.arange(0, MLA_ROPE, 2, dtype=F32) / MLA_ROPE)
    ang = jnp.arange(S, dtype=F32)[:, None] * inv_freq[None, :]
    cos, sin = jnp.cos(ang), jnp.sin(ang)
    pad = jnp.zeros((S, HEAD_PAD - MLA_NOPE - MLA_ROPE), F32)
    cos_t = jnp.concatenate([jnp.ones((S, MLA_NOPE), F32), cos, cos, pad], axis=-1)
    sin_t = jnp.concatenate([jnp.zeros((S, MLA_NOPE), F32), -sin, sin, pad], axis=-1)
    scale = (MLA_NOPE + MLA_ROPE) ** -0.5
    return cos_t * scale, sin_t * scale, cos_t, sin_t


def _block_diag(blocks):
    L, G, n, _ = blocks.shape
    eye = jnp.eye(G, dtype=blocks.dtype)
    return jnp.einsum('lgij,gh->lgihj', blocks, eye).reshape(L, G * n, G * n)


def kernel(x, c, w_in, mla_q_norm, mla_kv_norm, mla_w_uq, mla_w_ukv, gdn_conv, gdn_a_log, gdn_dt_bias, gdn_out_norm, pool_w, pool_scale, w_out, w_mod, b_mod, ln1_g, ln1_b, ln2_g, ln2_b, router_w_group, router_b_group, router_w_expert, router_b_expert, moe_w_gate, moe_w_up, moe_w_down):
    B, S, D = x.shape
    L = w_in.shape[0]
    T = B * S
    alpha = (2 * L) ** 0.25
    ts = min(512, S)
    tq = min(256, S)
    t_moe = min(512, S)
    t_disp = min(1024, S)
    assert D == D_MODEL and S % ts == 0 and S % tq == 0 and ts % CHUNK == 0
    assert S % t_disp == 0 and t_moe % DMA_WAIT_UNROLL == 0

    w_in2 = _prep_w_in(w_in)
    wq2, wkv2 = _prep_mla(mla_w_uq, mla_w_ukv)
    tabs = _rope_tables(S)
    gq = mla_q_norm.reshape(L, 1, MLA_Q_RANK)
    gkv = mla_kv_norm.reshape(L, 1, MLA_KV_RANK)
    alog_e = jnp.repeat(gdn_a_log, GDN_DIM, axis=-1).reshape(L, 1, GDN_WIDTH)
    dt_e = jnp.repeat(gdn_dt_bias, GDN_DIM, axis=-1).reshape(L, 1, GDN_WIDTH)
    og_e = jnp.tile(gdn_out_norm, (1, GDN_HEADS)).reshape(L, 1, GDN_WIDTH)
    lane_head = jnp.arange(GDN_WIDTH) // GDN_DIM
    seg = (lane_head[:, None] == lane_head[None, :]).astype(BF16)
    tri_c = (jnp.arange(CHUNK)[:, None] >= jnp.arange(CHUNK)[None, :]).astype(BF16)
    pool_bd = _block_diag(pool_w).astype(BF16)
    pool_sc = pool_scale.reshape(L, 1, POOL_WIDTH)
    w_out_b = w_out.astype(BF16)
    w_r = jnp.concatenate([router_w_group, router_w_expert,
                           jnp.zeros((L, D, LANE - N_GROUPS - N_EXPERTS), F32)], axis=-1)
    w_r_hi = w_r.astype(BF16)
    w_r_lo = (w_r - w_r_hi.astype(F32)).astype(BF16)
    b_r = jnp.concatenate([router_b_group, router_b_expert,
                           jnp.zeros((L, LANE - N_GROUPS - N_EXPERTS), F32)], axis=-1).reshape(L, 1, LANE)
    tri_r = (jnp.arange(ts)[:, None] > jnp.arange(ts)[None, :]).astype(BF16)
    ln1g, ln1b = ln1_g.reshape(L, 1, D), ln1_b.reshape(L, 1, D)
    ln2g, ln2b = ln2_g.reshape(L, 1, D), ln2_b.reshape(L, 1, D)

    assert N_BUCKETS <= LANE
    nb = (T + N_BUCKETS * (MOE_ROWS - 1) + MOE_ROWS - 1) // MOE_ROWS
    P = nb * MOE_ROWS
    pa, pb = [], []
    for g_ in range(N_GROUPS):
        for a_ in range(EXPERTS_PER_GROUP):
            for b_ in range(a_ + 1, EXPERTS_PER_GROUP):
                pa.append(g_ * EXPERTS_PER_GROUP + a_)
                pb.append(g_ * EXPERTS_PER_GROUP + b_)
    bucket_a = jnp.asarray(pa, jnp.int32)
    bucket_b = jnp.asarray(pb, jnp.int32)
    bucket_ids = jnp.arange(N_BUCKETS, dtype=jnp.int32)
    blk0 = jnp.arange(nb, dtype=jnp.int32) * MOE_ROWS
    xs = jnp.zeros((P, D + LANE), F32)

    mod = _modulation(c, w_mod, b_mod)
    x2 = x.reshape(T, D)
    for l in range(L):
        mod_l = mod[l].reshape(B, 1, 6 * D)
        proj = _inproj(x2, mod_l, w_in2, l, S, ts)
        q, k, v = _mla_proj(proj, tabs, gq, gkv, wq2, wkv2, l, S, ts)
        y_mla = _attention(q, k, v, B, S, tq)
        qh, kh, vh, g, beta = _gdn_pre(proj, gdn_conv, alog_e, dt_e, seg, l, B, S, ts)
        y_gdn = _gdn(qh, kh, vh, g, beta, proj, og_e, seg, tri_c, l, B, S, ts)
        y_pool = _pool(proj, pool_bd, pool_sc, l, B, S, ts)
        x2 = _outproj(y_mla, y_gdn, y_pool, w_out_b, x2, mod_l, ln1g, ln1b, l, S, ts, alpha)

        hm, cnt = _router(x2, mod_l, w_r_hi, w_r_lo, b_r, tri_r, l, S, ts)
        counts = cnt[0, :N_BUCKETS].astype(jnp.int32)
        padded = (counts + MOE_ROWS - 1) // MOE_ROWS * MOE_ROWS
        pad_end = jnp.cumsum(padded)
        pad_start = pad_end - padded
        bucket = hm[:, D].astype(jnp.int32)
        rank = hm[:, D + 1].astype(jnp.int32)
        dest = jnp.sum(jnp.where(bucket[:, None] == bucket_ids[None, :], pad_start[None, :], 0), axis=1) + rank
        block_bucket = jnp.minimum(jnp.sum((pad_end[None, :] <= blk0[:, None]).astype(jnp.int32), axis=1),
                                   N_BUCKETS - 1)
        onehot_b = (block_bucket[:, None] == bucket_ids[None, :]).astype(jnp.int32)
        ea = jnp.sum(onehot_b * bucket_a[None, :], axis=1)
        eb = jnp.sum(onehot_b * bucket_b[None, :], axis=1)
        used = (blk0 < pad_end[-1]).astype(jnp.int32)
        first = jnp.ones((1,), jnp.int32)
        ch_a = jnp.concatenate([first, (ea[1:] != ea[:-1]).astype(jnp.int32)])
        ch_b = jnp.concatenate([first, (eb[1:] != eb[:-1]).astype(jnp.int32)])
        flags = used + 2 * ch_a + 4 * ch_b

        xs = _dispatch(hm, dest, xs, t_disp)
        ys = _ffn(xs, ea, eb, flags, moe_w_gate, moe_w_up, moe_w_down, l)
        x2 = _combine(ys, dest, x2, mod_l, ln2g, ln2b, l, S, t_moe, alpha)
    return x2.reshape(B, S, D)
```

```python
import functools
import math

import jax
import jax.numpy as jnp
from jax import lax
from jax.experimental import pallas as pl
from jax.experimental.pallas import tpu as pltpu

F32 = jnp.float32
BF16 = jnp.bfloat16

D_MODEL = 1024
CHUNK = 64
MLA_HEADS = 8
MLA_NOPE = 64
MLA_ROPE = 32
MLA_V = 64
MLA_Q_RANK = 256
MLA_KV_RANK = 128
MLA_WIDTH = MLA_HEADS * MLA_V
ROPE_THETA = 10000.0
GDN_HEADS = 4
GDN_DIM = 64
GDN_WIDTH = GDN_HEADS * GDN_DIM
GDN_CONV = 4
POOL_WIDTH = 256
POOL_WINDOWS = (2, 4, 8, 16)
POOL_GROUP_DIM = 64
N_GROUPS = 4
EXPERTS_PER_GROUP = 8
N_EXPERTS = N_GROUPS * EXPERTS_PER_GROUP
TOP_K = 2
EXPERT_FF = 256
LN_EPS = 1e-5
RMS_EPS = 1e-6

LANE = 128
HEAD_PAD = 128
PAIRS_PER_GROUP = EXPERTS_PER_GROUP * (EXPERTS_PER_GROUP - 1) // 2
N_BUCKETS = N_GROUPS * PAIRS_PER_GROUP
MOE_ROWS = 128
VMEM_LIMIT = 48 * 1024 * 1024

_C_GQ, _C_GK, _C_GV, _C_GZ = 0, 256, 512, 768
_C_A, _C_B = 1024, 1280
_C_CQ, _C_POOL = 1536, 1792
_C_CKV, _C_KRA, _C_KRB = 2048, 2176, 2304
IN_COLS = 2432


def _cparams(sem):
    return pltpu.CompilerParams(dimension_semantics=sem, vmem_limit_bytes=VMEM_LIMIT)


def _sigmoid(x):
    return 1.0 / (1.0 + jnp.exp(-x))


def _split2(x):
    hi = x.astype(BF16)
    lo = (x - hi.astype(F32)).astype(BF16)
    return hi, lo


def _split3(x):
    hi = x.astype(BF16)
    r = x - hi.astype(F32)
    mid = r.astype(BF16)
    lo = (r - mid.astype(F32)).astype(BF16)
    return hi, mid, lo


def _dot(a, b):
    return jnp.dot(a, b, preferred_element_type=F32)


def _dot_nt(a, b):
    return lax.dot_general(a, b, (((1,), (1,)), ((), ())), preferred_element_type=F32)


def _dot_tn(a, b):
    return lax.dot_general(a, b, (((0,), (0,)), ((), ())), preferred_element_type=F32)


def _mod_body(c_ref, w_ref, b_ref, o_ref):
    c = c_ref[...]
    ca = c * _sigmoid(c)
    o_ref[0] = _dot(ca.astype(BF16), w_ref[0].astype(BF16)) + b_ref[0]


def _modulation(c, w_mod, b_mod):
    L, D, N = w_mod.shape
    B = c.shape[0]
    tn = 1024
    return pl.pallas_call(
        _mod_body,
        grid=(L, N // tn),
        in_specs=[pl.BlockSpec((B, D), lambda l, j: (0, 0)),
                  pl.BlockSpec((1, D, tn), lambda l, j: (l, 0, j)),
                  pl.BlockSpec((1, 1, tn), lambda l, j: (l, 0, j))],
        out_specs=pl.BlockSpec((1, B, tn), lambda l, j: (l, 0, j)),
        out_shape=jax.ShapeDtypeStruct((L, B, N), F32),
        compiler_params=_cparams(("arbitrary", "arbitrary")),
        name="modulation",
    )(c, w_mod, b_mod.reshape(L, 1, N))


def _inproj_body(x_ref, sh_ref, sc_ref, w_ref, o_ref):
    h = x_ref[...] * (1.0 + sc_ref[0]) + sh_ref[0]
    o_ref[...] = _dot(h.astype(BF16), w_ref[0])


def _inproj(x2, mod_l, w_in2, l, S, tm):
    T, D = x2.shape
    nS = S // tm
    return pl.pallas_call(
        _inproj_body,
        grid=(T // tm,),
        in_specs=[pl.BlockSpec((tm, D), lambda i: (i, 0)),
                  pl.BlockSpec((1, 1, D), lambda i: (i // nS, 0, 0)),
                  pl.BlockSpec((1, 1, D), lambda i: (i // nS, 0, 1)),
                  pl.BlockSpec((1, D, IN_COLS), lambda i: (l, 0, 0))],
        out_specs=pl.BlockSpec((tm, IN_COLS), lambda i: (i, 0)),
        out_shape=jax.ShapeDtypeStruct((T, IN_COLS), F32),
        compiler_params=_cparams(("arbitrary",)),
        name="inproj",
    )(x2, mod_l, mod_l, w_in2)


def _mla_proj_body(cq_ref, ckv_ref, kra_ref, krb_ref, cosq_ref, sinq_ref, cosk_ref, sink_ref,
                   gq_ref, gkv_ref, wq_ref, wkv_ref, q_out, k_out, v_out):
    cq = cq_ref[...]
    qn = cq * lax.rsqrt(jnp.mean(cq * cq, axis=-1, keepdims=True) + RMS_EPS) * gq_ref[0]
    q2 = _dot(qn.astype(BF16), wq_ref[0])
    ckv = ckv_ref[...]
    kvn = ckv * lax.rsqrt(jnp.mean(ckv * ckv, axis=-1, keepdims=True) + RMS_EPS) * gkv_ref[0]
    kv2 = _dot(kvn.astype(BF16), wkv_ref[0])
    cq_t, sq_t = cosq_ref[...], sinq_ref[...]
    krope = kra_ref[...] * cosk_ref[...] + krb_ref[...] * sink_ref[...]
    hw = MLA_HEADS * HEAD_PAD
    for h in range(MLA_HEADS):
        a, b = h * HEAD_PAD, (h + 1) * HEAD_PAD
        q_out[:, a:b] = (q2[:, a:b] * cq_t + q2[:, hw + a:hw + b] * sq_t).astype(BF16)
        k_out[:, a:b] = (kv2[:, a:b] + krope).astype(BF16)
    vl = lax.broadcasted_iota(jnp.int32, (1, hw), 1)
    ones_lane = jnp.where((vl // HEAD_PAD) % 2 == 0, MLA_V, 0)
    v_out[...] = (kv2[:, hw:] + (vl % HEAD_PAD == ones_lane).astype(F32)).astype(BF16)


def _mla_proj(proj, tabs, gq, gkv, wq2, wkv2, l, S, ts):
    T = proj.shape[0]
    nS = S // ts
    hw = MLA_HEADS * HEAD_PAD
    tab_spec = pl.BlockSpec((ts, LANE), lambda i: (i % nS, 0))
    return pl.pallas_call(
        _mla_proj_body,
        grid=(T // ts,),
        in_specs=[pl.BlockSpec((ts, 256), lambda i: (i, _C_CQ // 256)),
                  pl.BlockSpec((ts, 128), lambda i: (i, _C_CKV // 128)),
                  pl.BlockSpec((ts, 128), lambda i: (i, _C_KRA // 128)),
                  pl.BlockSpec((ts, 128), lambda i: (i, _C_KRB // 128)),
                  tab_spec, tab_spec, tab_spec, tab_spec,
                  pl.BlockSpec((1, 1, MLA_Q_RANK), lambda i: (l, 0, 0)),
                  pl.BlockSpec((1, 1, MLA_KV_RANK), lambda i: (l, 0, 0)),
                  pl.BlockSpec((1, MLA_Q_RANK, 2 * hw), lambda i: (l, 0, 0)),
                  pl.BlockSpec((1, MLA_KV_RANK, 2 * hw), lambda i: (l, 0, 0))],
        out_specs=[pl.BlockSpec((ts, hw), lambda i: (i, 0))] * 3,
        out_shape=[jax.ShapeDtypeStruct((T, hw), BF16)] * 3,
        compiler_params=_cparams(("arbitrary",)),
        name="mla_proj",
    )(proj, proj, proj, proj, *tabs, gq, gkv, wq2, wkv2)


def _attn_body(qi_tab, kj_tab, q_ref, k_ref, v_ref, o_ref, m_s, acc_s, *, tq, n_q, n_tiles):
    row_c = lax.broadcasted_iota(jnp.int32, (tq, tq), 0) // CHUNK
    col_c = lax.broadcasted_iota(jnp.int32, (tq, tq), 1) // CHUNK
    allowed = col_c <= row_c
    lane = lax.broadcasted_iota(jnp.int32, (tq, HEAD_PAD), 1)
    sl = [slice(hh * HEAD_PAD, (hh + 1) * HEAD_PAD) for hh in range(2)]
    hs = range(2)

    def rows_of(tile):
        return pl.ds(pl.multiple_of(tile * tq, tq), tq)

    def stage_a(t):
        q = q_ref[rows_of(qi_tab[t]), :]
        kk = k_ref[rows_of(kj_tab[t]), :]
        return tuple(_dot_nt(q[:, sl[hh]], kk[:, sl[hh]]) for hh in hs)

    def stage_b(t, s_pair):
        qi, kj = qi_tab[t], kj_tab[t]
        rows = rows_of(qi)
        keep = jnp.logical_or(allowed, kj < qi)
        s = [jnp.where(keep, s_pair[hh], -jnp.inf) for hh in hs]
        m_prev = [jnp.where(kj == 0, -jnp.inf, m_s[hh, rows, :]) for hh in hs]
        m_new = [jnp.maximum(m_prev[hh], jnp.max(s[hh], axis=-1, keepdims=True)) for hh in hs]
        p = tuple(jnp.exp((s[hh] - jnp.concatenate([m_new[hh]] * (tq // HEAD_PAD), axis=1)).astype(BF16))
                  for hh in hs)
        alpha = tuple(jnp.exp(m_prev[hh] - m_new[hh]) for hh in hs)
        for hh in hs:
            m_s[hh, rows, :] = m_new[hh]
        return p, alpha

    def stage_c(t, p, alpha):
        qi, kj = qi_tab[t], kj_tab[t]
        rows = rows_of(qi)
        vv = v_ref[rows_of(kj), :]
        pv = [_dot(p[hh], vv[:, sl[hh]]) for hh in hs]
        for hh in hs:
            acc_s[hh, rows, :] = alpha[hh] * jnp.where(kj == 0, 0.0, acc_s[hh, rows, :]) + pv[hh]

    def body(t, c):
        s_next, (p, alpha) = c
        s_after = stage_a(t + 2)
        stage_c(t, p, alpha)
        return s_after, stage_b(t + 1, s_next)

    pb0 = stage_b(0, stage_a(0))
    s_last, pb = lax.fori_loop(0, n_tiles - 2, body, (stage_a(1), pb0))
    pb_last = stage_b(n_tiles - 1, s_last)
    stage_c(n_tiles - 2, *pb)
    stage_c(n_tiles - 1, *pb_last)

    def normalize(i, _):
        rows = rows_of(i)
        acc0, acc1 = acc_s[0, rows, :], acc_s[1, rows, :]
        o0 = jnp.where(lane < MLA_V, acc0 * (1.0 / acc0[:, MLA_V:MLA_V + 1]), 0.0)
        o1 = jnp.where(lane >= MLA_V, acc1 * (1.0 / acc1[:, 0:1]), 0.0)
        o_ref[rows, :] = (o0 + o1).astype(BF16)
        return 0

    lax.fori_loop(0, n_q, normalize, 0)


def _attention(q, k, v, B, S, tq):
    T = q.shape[0]
    nq = S // tq
    pairs = MLA_HEADS // 2
    tiles = [(i, j) for i in range(nq) for j in range(i + 1)]
    assert len(tiles) >= 2
    qi_tab = jnp.asarray([t[0] for t in tiles], jnp.int32)
    kj_tab = jnp.asarray([t[1] for t in tiles], jnp.int32)
    seq = pl.BlockSpec((S, 2 * HEAD_PAD), lambda b, p, qt, kt: (b, p))
    grid_spec = pltpu.PrefetchScalarGridSpec(
        num_scalar_prefetch=2,
        grid=(B, pairs),
        in_specs=[seq, seq, seq],
        out_specs=pl.BlockSpec((S, 2 * MLA_V), lambda b, p, qt, kt: (b, p)),
        scratch_shapes=[pltpu.VMEM((2, S, HEAD_PAD), F32), pltpu.VMEM((2, S, HEAD_PAD), F32)],
    )
    return pl.pallas_call(
        functools.partial(_attn_body, tq=tq, n_q=nq, n_tiles=len(tiles)),
        grid_spec=grid_spec,
        out_shape=jax.ShapeDtypeStruct((T, MLA_WIDTH), BF16),
        compiler_params=_cparams(("arbitrary", "arbitrary")),
        name="mla_attention",
    )(qi_tab, kj_tab, q, k, v)


def _pool_body(p_ref, w_ref, sc_ref, o_ref, prev_ref, *, ts):
    j = pl.program_id(1)
    hist = 16

    @pl.when(j == 0)
    def _():
        prev_ref[...] = jnp.zeros_like(prev_ref)

    cur = p_ref[...]
    x = jnp.concatenate([prev_ref[...], cur], axis=0)
    s1 = x + pltpu.roll(x, 1, axis=0)
    s2 = s1 + pltpu.roll(s1, 2, axis=0)
    s4 = s2 + pltpu.roll(s2, 4, axis=0)
    s8 = s4 + pltpu.roll(s4, 8, axis=0)
    pos = (j * ts + lax.broadcasted_iota(jnp.int32, (ts, POOL_WIDTH), 0) + 1).astype(F32)
    lane = lax.broadcasted_iota(jnp.int32, (ts, POOL_WIDTH), 1)
    sums = (s1, s2, s4, s8)
    mean = None
    for gi, win in enumerate(POOL_WINDOWS):
        m_g = sums[gi][hist:] / jnp.minimum(pos, float(win))
        mean = m_g if mean is None else jnp.where(lane >= gi * POOL_GROUP_DIM, m_g, mean)
    delta = mean - cur
    o_ref[...] = (_dot(delta.astype(BF16), w_ref[0]) * sc_ref[0]).astype(BF16)
    prev_ref[...] = cur[ts - hist:]


def _pool(proj, w_bd, scale, l, B, S, ts):
    T = proj.shape[0]
    nS = S // ts
    return pl.pallas_call(
        functools.partial(_pool_body, ts=ts),
        grid=(B, nS),
        in_specs=[pl.BlockSpec((ts, POOL_WIDTH), lambda b, j: (b * nS + j, _C_POOL // 256)),
                  pl.BlockSpec((1, POOL_WIDTH, POOL_WIDTH), lambda b, j: (l, 0, 0)),
                  pl.BlockSpec((1, 1, POOL_WIDTH), lambda b, j: (l, 0, 0))],
        out_specs=pl.BlockSpec((ts, POOL_WIDTH), lambda b, j: (b * nS + j, 0)),
        out_shape=jax.ShapeDtypeStruct((T, POOL_WIDTH), BF16),
        scratch_shapes=[pltpu.VMEM((16, POOL_WIDTH), F32)],
        compiler_params=_cparams(("arbitrary", "arbitrary")),
        name="pool_mixer",
    )(proj, w_bd, scale)


def _head_sum(x, seg_ref):
    hi, lo = _split2(x)
    return _dot(hi, seg_ref[...]) + _dot(lo, seg_ref[...])


def _gdn_pre_body(q_ref, k_ref, v_ref, a_ref, b_ref, conv_ref, alog_ref, dt_ref, seg_ref,
                  qo_ref, ko_ref, vo_ref, go_ref, bo_ref, prev_ref, *, ts):
    j = pl.program_id(1)
    hist = 8

    @pl.when(j == 0)
    def _():
        prev_ref[...] = jnp.zeros_like(prev_ref)

    cw = conv_ref[0]
    outs = []
    for idx, ref in enumerate((q_ref, k_ref, v_ref)):
        cur = ref[...]
        a, b = idx * GDN_WIDTH, (idx + 1) * GDN_WIDTH
        x = jnp.concatenate([prev_ref[:, a:b], cur], axis=0)
        w = cw[:, a:b]
        y = (w[3:4] * x + w[2:3] * pltpu.roll(x, 1, axis=0) + w[1:2] * pltpu.roll(x, 2, axis=0)
             + w[0:1] * pltpu.roll(x, 3, axis=0))[hist:]
        outs.append(y * _sigmoid(y))
        prev_ref[:, a:b] = cur[ts - hist:]
    qc, kc, vc = outs
    qo_ref[...] = qc * lax.rsqrt(_head_sum(qc * qc, seg_ref) + RMS_EPS) * (GDN_DIM ** -0.5)
    ko_ref[...] = kc * lax.rsqrt(_head_sum(kc * kc, seg_ref) + RMS_EPS)
    vo_ref[...] = vc
    z = a_ref[...] + dt_ref[0]
    softplus = jnp.maximum(z, 0.0) + jnp.log(1.0 + jnp.exp(-jnp.abs(z)))
    go_ref[...] = -jnp.exp(alog_ref[0]) * softplus
    bo_ref[...] = _sigmoid(b_ref[...])


def _gdn_pre(proj, conv_w, alog_e, dt_e, seg, l, B, S, ts):
    T = proj.shape[0]
    nS = S // ts
    W = GDN_WIDTH

    def col(c):
        return pl.BlockSpec((ts, W), lambda b, j: (b * nS + j, c // W))

    return pl.pallas_call(
        functools.partial(_gdn_pre_body, ts=ts),
        grid=(B, nS),
        in_specs=[col(_C_GQ), col(_C_GK), col(_C_GV), col(_C_A), col(_C_B),
                  pl.BlockSpec((1, GDN_CONV, 3 * W), lambda b, j: (l, 0, 0)),
                  pl.BlockSpec((1, 1, W), lambda b, j: (l, 0, 0)),
                  pl.BlockSpec((1, 1, W), lambda b, j: (l, 0, 0)),
                  pl.BlockSpec((W, W), lambda b, j: (0, 0))],
        out_specs=[pl.BlockSpec((ts, W), lambda b, j: (b * nS + j, 0))] * 5,
        out_shape=[jax.ShapeDtypeStruct((T, W), F32)] * 5,
        scratch_shapes=[pltpu.VMEM((8, 3 * W), F32)],
        compiler_params=_cparams(("arbitrary", "arbitrary")),
        name="gdn_pre",
    )(proj, proj, proj, proj, proj, conv_w, alog_e, dt_e, seg)


def _gdn_body(q_ref, k_ref, v_ref, g_ref, b_ref, z_ref, og_ref, seg_ref, tri_ref,
              o_ref, state_ref, u_s, w_s, qk_s, qd_s, kd_s, gl_s, *, ts):
    j = pl.program_id(1)
    C, W, H = CHUNK, GDN_WIDTH, GDN_HEADS
    n_chunks = ts // C

    @pl.when(j == 0)
    def _():
        state_ref[...] = jnp.zeros_like(state_ref)

    lane = lax.broadcasted_iota(jnp.int32, (C, W), 1)
    row = lax.broadcasted_iota(jnp.int32, (C, W), 0)
    col_tok = lane % C
    incl = row >= col_tok
    strict = row > col_tok
    eye = (row == col_tok).astype(F32)
    head_masks = [(lane // C == h).astype(F32) for h in range(H)]
    bd_mask = (lax.broadcasted_iota(jnp.int32, (W, W), 0) // C
               == lax.broadcasted_iota(jnp.int32, (W, W), 1) // C)

    def expand(m):
        return jnp.concatenate([m * hm for hm in head_masks], axis=0)

    def split_dot(lhs, rhs):
        l_hi, l_lo = _split2(lhs)
        r_hi, r_lo = _split2(rhs)
        return _dot(l_hi, r_hi) + _dot(l_lo, r_hi) + _dot(l_hi, r_lo)

    def intra(i, _):
        cs = [i * INTRA_UNROLL + uu for uu in range(INTRA_UNROLL)]
        r0s = [pl.multiple_of(c * C, C) for c in cs]
        n = range(INTRA_UNROLL)
        tri = tri_ref[...]
        gs = [_split3(g_ref[pl.ds(r0, C), :]) for r0 in r0s]
        gc = [_dot(tri, g[0]) + _dot(tri, g[1]) + _dot(tri, g[2]) for g in gs]
        ks = [k_ref[pl.ds(r0, C), :] for r0 in r0s]
        qs = [q_ref[pl.ds(r0, C), :] for r0 in r0s]
        betas = [b_ref[pl.ds(r0, C), :] for r0 in r0s]
        kb = [ks[x] * betas[x] for x in n]
        aq = [_dot_nt(jnp.concatenate([kb[x], qs[x]], axis=0).astype(BF16), expand(ks[x]).astype(BF16)) for x in n]
        g_row = [jnp.sum(gc[x] * eye, axis=0, keepdims=True) for x in n]
        g_last = [gc[x][C - 1:C, :] for x in n]
        decay = [jnp.where(incl, jnp.exp(jnp.where(incl, gc[x] - g_row[x], 0.0)), 0.0) for x in n]
        e_gc = [jnp.exp(gc[x]) for x in n]
        a_cat = [jnp.where(strict, aq[x][:C] * decay[x], 0.0) for x in n]
        for x in n:
            r0 = r0s[x]
            qk_s[pl.ds(r0, C), :] = jnp.where(incl, aq[x][C:] * decay[x], 0.0).astype(BF16)
            qd_s[pl.ds(r0, C), :] = (qs[x] * e_gc[x]).astype(BF16)
            kd_s[pl.ds(r0, C), :] = (ks[x] * jnp.exp(g_last[x] - gc[x])).astype(BF16)
            gl_s[pl.ds(pl.multiple_of(cs[x] * 8, 8), 8), :] = jnp.broadcast_to(jnp.exp(g_last[x]), (8, W))
        x_cat = [eye - a_cat[x] for x in n]
        p_cat = [split_dot(a_cat[x], expand(a_cat[x])) for x in n]
        n_fac = int(math.log2(C)) - 1
        for r in range(n_fac):
            last = r == n_fac - 1
            xp = [split_dot(x_cat[x] if last else jnp.concatenate([x_cat[x], p_cat[x]], axis=0), expand(p_cat[x]))
                  for x in n]
            x_cat = [x_cat[x] + xp[x][:C] for x in n]
            if not last:
                p_cat = [xp[x][C:] for x in n]
        for x in n:
            r0 = r0s[x]
            t_cat = x_cat[x].astype(BF16)
            v = v_ref[pl.ds(r0, C), :]
            u_s[pl.ds(r0, C), :] = _dot(t_cat, expand(v * betas[x]).astype(BF16))
            w_s[pl.ds(r0, C), :] = _dot(t_cat, expand(kb[x] * e_gc[x]).astype(BF16)).astype(BF16)
        return 0

    lax.fori_loop(0, n_chunks // INTRA_UNROLL, intra, 0)

    def scan(c, _):
        r0 = pl.multiple_of(c * C, C)
        state = state_ref[...]
        wq = _dot(jnp.concatenate([w_s[pl.ds(r0, C), :], qd_s[pl.ds(r0, C), :]], axis=0), state.astype(BF16))
        v_new = u_s[pl.ds(r0, C), :] - wq[:C]
        o = wq[C:] + _dot(qk_s[pl.ds(r0, C), :], expand(v_new).astype(BF16))
        upd = _dot_tn(kd_s[pl.ds(r0, C), :], v_new.astype(BF16))
        g_l = gl_s[pl.ds(pl.multiple_of(c * 8, 8), 1), :]
        state_ref[...] = state * g_l + jnp.where(bd_mask, upd, 0.0)
        ms = _head_sum(o * o, seg_ref) * (1.0 / GDN_DIM)
        z = z_ref[pl.ds(r0, C), :]
        y = o * lax.rsqrt(ms + RMS_EPS) * og_ref[0] * (z * _sigmoid(z))
        o_ref[pl.ds(r0, C), :] = y.astype(BF16)
        return 0

    lax.fori_loop(0, n_chunks, scan, 0)


def _gdn(qh, kh, vh, g, beta, proj, og, seg, tri, l, B, S, ts):
    T = qh.shape[0]
    nS = S // ts
    W = GDN_WIDTH
    row = pl.BlockSpec((ts, W), lambda b, j: (b * nS + j, 0))
    return pl.pallas_call(
        functools.partial(_gdn_body, ts=ts),
        grid=(B, nS),
        in_specs=[row, row, row, row, row,
                  pl.BlockSpec((ts, W), lambda b, j: (b * nS + j, _C_GZ // W)),
                  pl.BlockSpec((1, 1, W), lambda b, j: (l, 0, 0)),
                  pl.BlockSpec((W, W), lambda b, j: (0, 0)),
                  pl.BlockSpec((CHUNK, CHUNK), lambda b, j: (0, 0))],
        out_specs=pl.BlockSpec((ts, W), lambda b, j: (b * nS + j, 0)),
        out_shape=jax.ShapeDtypeStruct((T, W), BF16),
        scratch_shapes=[pltpu.VMEM((W, W), F32), pltpu.VMEM((ts, W), F32)]
        + [pltpu.VMEM((ts, W), BF16)] * 4 + [pltpu.VMEM((8 * (ts // CHUNK), W), F32)],
        compiler_params=_cparams(("arbitrary", "arbitrary")),
        name="gdn_delta_rule",
    )(qh, kh, vh, g, beta, proj, og, seg, tri)


def _layer_norm(r, g, b):
    mu = jnp.mean(r, axis=-1, keepdims=True)
    d = r - mu
    var = jnp.mean(d * d, axis=-1, keepdims=True)
    return d * lax.rsqrt(var + LN_EPS) * g + b


def _outproj_body(ym_ref, yg_ref, yp_ref, w_ref, x_ref, gt_ref, lg_ref, lb_ref, o_ref, *, alpha):
    w = w_ref[0]
    y = (_dot(ym_ref[...], w[:MLA_WIDTH]) + _dot(yg_ref[...], w[MLA_WIDTH:MLA_WIDTH + GDN_WIDTH])
         + _dot(yp_ref[...], w[MLA_WIDTH + GDN_WIDTH:]))
    r = alpha * x_ref[...] + (1.0 + gt_ref[0]) * y
    o_ref[...] = _layer_norm(r, lg_ref[0], lb_ref[0])


def _outproj(y_mla, y_gdn, y_pool, w_out, x2, mod_l, ln_g, ln_b, l, S, tm, alpha):
    T, D = x2.shape
    nS = S // tm
    vec = pl.BlockSpec((1, 1, D), lambda i: (l, 0, 0))
    return pl.pallas_call(
        functools.partial(_outproj_body, alpha=alpha),
        grid=(T // tm,),
        in_specs=[pl.BlockSpec((tm, MLA_WIDTH), lambda i: (i, 0)),
                  pl.BlockSpec((tm, GDN_WIDTH), lambda i: (i, 0)),
                  pl.BlockSpec((tm, POOL_WIDTH), lambda i: (i, 0)),
                  pl.BlockSpec((1, D, D), lambda i: (l, 0, 0)),
                  pl.BlockSpec((tm, D), lambda i: (i, 0)),
                  pl.BlockSpec((1, 1, D), lambda i: (i // nS, 0, 2)),
                  vec, vec],
        out_specs=pl.BlockSpec((tm, D), lambda i: (i, 0)),
        out_shape=jax.ShapeDtypeStruct((T, D), F32),
        compiler_params=_cparams(("arbitrary",)),
        name="outproj_ln",
    )(y_mla, y_gdn, y_pool, w_out, x2, mod_l, ln_g, ln_b)


def _lane_first(cond, lane_f):
    return jnp.min(jnp.where(cond, lane_f, float(LANE)), axis=-1, keepdims=True)


def _router_body(x_ref, sh_ref, sc_ref, whi_ref, wlo_ref, br_ref, tri_ref,
                 hm_out, cnt_out, carry_ref):
    i = pl.program_id(0)

    @pl.when(i == 0)
    def _():
        carry_ref[...] = jnp.zeros_like(carry_ref)

    h = x_ref[...] * (1.0 + sc_ref[0]) + sh_ref[0]
    hm_out[:, :D_MODEL] = h
    h_hi, h_lo = _split2(h)
    logits = _dot(h_hi, whi_ref[0]) + _dot(h_lo, whi_ref[0]) + _dot(h_hi, wlo_ref[0]) + br_ref[0]
    tm = logits.shape[0]
    lane = lax.broadcasted_iota(jnp.int32, (tm, LANE), 1)
    lane_f = lane.astype(F32)
    neg = -jnp.inf
    gl = jnp.where(lane < N_GROUPS, logits, neg)
    gmax = jnp.max(gl, axis=-1, keepdims=True)
    gsel = _lane_first(gl == gmax, lane_f)
    g_p = 1.0 / jnp.sum(jnp.exp(gl - gmax), axis=-1, keepdims=True)
    lo = N_GROUPS + EXPERTS_PER_GROUP * gsel
    el = jnp.where((lane_f >= lo) & (lane_f < lo + EXPERTS_PER_GROUP), logits, neg)
    m1 = jnp.max(el, axis=-1, keepdims=True)
    i1 = _lane_first(el == m1, lane_f)
    el2 = jnp.where(lane_f == i1, neg, el)
    m2 = jnp.max(el2, axis=-1, keepdims=True)
    i2 = _lane_first(el2 == m2, lane_f)
    t = jnp.exp(m2 - m1)
    w1 = g_p / (1.0 + t)
    w2 = g_p * t / (1.0 + t)
    loc1 = i1 - lo
    loc2 = i2 - lo
    a_loc = jnp.minimum(loc1, loc2)
    b_loc = jnp.maximum(loc1, loc2)
    pair = a_loc * (2 * EXPERTS_PER_GROUP - 1 - a_loc) * 0.5 + (b_loc - a_loc - 1.0)
    bucket = gsel * PAIRS_PER_GROUP + pair
    first_is_a = loc1 < loc2
    w_a = jnp.where(first_is_a, w1, w2)
    w_b = jnp.where(first_is_a, w2, w1)
    hit = lane_f == bucket
    onehot = hit.astype(BF16)
    before = _dot(tri_ref[...], onehot) + carry_ref[0:1, :]
    rank = jnp.sum(jnp.where(hit, before, 0.0), axis=-1, keepdims=True)
    total = carry_ref[0:1, :] + jnp.sum(onehot.astype(F32), axis=0, keepdims=True)
    carry_ref[...] = jnp.broadcast_to(total, carry_ref.shape)
    cnt_out[...] = jnp.broadcast_to(total, cnt_out.shape)
    meta = jnp.zeros((tm, LANE), F32)
    for idx, val in enumerate((bucket, rank, w_a, w_b)):
        meta = jnp.where(lane == idx, val, meta)
    hm_out[:, D_MODEL:] = meta


def _router(x2, mod_l, w_hi, w_lo, b_r, tri, l, S, tm):
    T, D = x2.shape
    nS = S // tm
    return pl.pallas_call(
        _router_body,
        grid=(T // tm,),
        in_specs=[pl.BlockSpec((tm, D), lambda i: (i, 0)),
                  pl.BlockSpec((1, 1, D), lambda i: (i // nS, 0, 3)),
                  pl.BlockSpec((1, 1, D), lambda i: (i // nS, 0, 4)),
                  pl.BlockSpec((1, D, LANE), lambda i: (l, 0, 0)),
                  pl.BlockSpec((1, D, LANE), lambda i: (l, 0, 0)),
                  pl.BlockSpec((1, 1, LANE), lambda i: (l, 0, 0)),
                  pl.BlockSpec((tm, tm), lambda i: (0, 0))],
        out_specs=[pl.BlockSpec((tm, D + LANE), lambda i: (i, 0)),
                   pl.BlockSpec((8, LANE), lambda i: (0, 0))],
        out_shape=[jax.ShapeDtypeStruct((T, D + LANE), F32),
                   jax.ShapeDtypeStruct((8, LANE), F32)],
        scratch_shapes=[pltpu.VMEM((8, LANE), F32)],
        compiler_params=_cparams(("arbitrary",)),
        name="router",
    )(x2, mod_l, mod_l, w_hi, w_lo, b_r, tri)


def _row_copy(src, s, dst, d, sem):
    return pltpu.make_async_copy(src.at[pl.ds(s, 1)], dst.at[pl.ds(d, 1)], sem)


DMA_WAIT_UNROLL = 32
INTRA_UNROLL = 4


def _issue_rows(n, make_copy):
    def body(r, _):
        make_copy(r).start()
        return 0

    lax.fori_loop(0, n, body, 0, unroll=8)


def _wait_rows(n, make_copy):
    def body(_, c):
        for _u in range(DMA_WAIT_UNROLL):
            make_copy(0).wait()
        return c

    lax.fori_loop(0, n // DMA_WAIT_UNROLL, body, 0)


def _dispatch_body(dest_ref, hm_ref, xs_in_ref, xs_ref, sem, *, tm):
    del xs_in_ref
    copy = lambda r: _row_copy(hm_ref, r, xs_ref, dest_ref[r], sem)
    _issue_rows(tm, copy)
    _wait_rows(tm, lambda r: _row_copy(hm_ref, 0, xs_ref, 0, sem))


def _dispatch(hm, dest, xs_buf, tm):
    T, W = hm.shape
    return pl.pallas_call(
        functools.partial(_dispatch_body, tm=tm),
        grid=(T // tm,),
        in_specs=[pl.BlockSpec((tm,), lambda i: (i,), memory_space=pltpu.SMEM),
                  pl.BlockSpec((tm, W), lambda i: (i, 0)),
                  pl.BlockSpec(memory_space=pl.ANY)],
        out_specs=pl.BlockSpec(memory_space=pl.ANY),
        out_shape=jax.ShapeDtypeStruct(xs_buf.shape, F32),
        input_output_aliases={2: 0},
        scratch_shapes=[pltpu.SemaphoreType.DMA],
        compiler_params=_cparams(("arbitrary",)),
        name="moe_dispatch",
    )(dest, hm, xs_buf)


def _ffn_body(grp_ref, ea_ref, eb_ref, used_ref, xs_ref, wg_ref, wu_ref, wd_ref, o_ref):
    i = pl.program_id(0)

    @pl.when(used_ref[i] != 0)
    def _():
        a, b = ea_ref[i], eb_ref[i]
        xm = xs_ref[...]
        x = xm[:, :D_MODEL].astype(BF16)
        w_a = xm[:, D_MODEL + 2:D_MODEL + 3]
        w_b = xm[:, D_MODEL + 3:D_MODEL + 4]
        g_a, u_a = _dot(x, wg_ref[0, 0, a]), _dot(x, wu_ref[0, 0, a])
        g_b, u_b = _dot(x, wg_ref[0, 0, b]), _dot(x, wu_ref[0, 0, b])
        act_a = (g_a * _sigmoid(g_a) * u_a * w_a).astype(BF16)
        act_b = (g_b * _sigmoid(g_b) * u_b * w_b).astype(BF16)
        o_ref[...] = _dot(act_a, wd_ref[0, 0, a]) + _dot(act_b, wd_ref[0, 0, b])

    @pl.when(used_ref[i] == 0)
    def _():
        o_ref[...] = jnp.zeros_like(o_ref)


def _ffn(xs, grp, ea, eb, used, w_gate, w_up, w_down, l):
    P, W = xs.shape
    D = D_MODEL
    nb = P // MOE_ROWS
    FF = EXPERT_FF
    E = EXPERTS_PER_GROUP
    w_up_spec = pl.BlockSpec((1, 1, E, D, FF), lambda i, g, ea, eb, u: (l, g[i], 0, 0, 0))
    w_dn_spec = pl.BlockSpec((1, 1, E, FF, D), lambda i, g, ea, eb, u: (l, g[i], 0, 0, 0))
    grid_spec = pltpu.PrefetchScalarGridSpec(
        num_scalar_prefetch=4,
        grid=(nb,),
        in_specs=[pl.BlockSpec((MOE_ROWS, W), lambda i, g, ea, eb, u: (i, 0)),
                  w_up_spec, w_up_spec, w_dn_spec],
        out_specs=pl.BlockSpec((MOE_ROWS, D), lambda i, g, ea, eb, u: (i, 0)),
    )
    return pl.pallas_call(
        _ffn_body,
        grid_spec=grid_spec,
        out_shape=jax.ShapeDtypeStruct((P, D), F32),
        compiler_params=_cparams(("arbitrary",)),
        name="moe_ffn",
    )(grp, ea, eb, used, xs, w_gate, w_up, w_down)


def _combine_body(dcur_ref, dnext_ref, x_ref, gt_ref, lg_ref, lb_ref, ys_ref, o_ref,
                  y0_ref, y1_ref, sems, *, tm, n_steps, alpha):
    i = pl.program_id(0)
    bufs = (y0_ref, y1_ref)

    def issue(dref, slot):
        _issue_rows(tm, lambda r: _row_copy(ys_ref, dref[r], bufs[slot], r, sems.at[slot]))

    def finish(slot):
        _wait_rows(tm, lambda r: _row_copy(ys_ref, 0, bufs[slot], 0, sems.at[slot]))
        r = alpha * x_ref[...] + (1.0 + gt_ref[0]) * bufs[slot][...]
        o_ref[...] = _layer_norm(r, lg_ref[0], lb_ref[0])

    @pl.when(i == 0)
    def _():
        issue(dcur_ref, 0)

    has_next = i + 1 < n_steps
    even = (i % 2) == 0

    @pl.when(has_next & even)
    def _():
        issue(dnext_ref, 1)

    @pl.when(has_next & jnp.logical_not(even))
    def _():
        issue(dnext_ref, 0)

    @pl.when(even)
    def _():
        finish(0)

    @pl.when(jnp.logical_not(even))
    def _():
        finish(1)


def _combine(ys, dest, x2, mod_l, ln_g, ln_b, l, S, tm, alpha):
    T, D = x2.shape
    nS = S // tm
    n_steps = T // tm
    vec = pl.BlockSpec((1, 1, D), lambda i: (l, 0, 0))
    return pl.pallas_call(
        functools.partial(_combine_body, tm=tm, n_steps=n_steps, alpha=alpha),
        grid=(n_steps,),
        in_specs=[pl.BlockSpec((tm,), lambda i: (i,), memory_space=pltpu.SMEM),
                  pl.BlockSpec((tm,), lambda i: (jnp.minimum(i + 1, n_steps - 1),), memory_space=pltpu.SMEM),
                  pl.BlockSpec((tm, D), lambda i: (i, 0)),
                  pl.BlockSpec((1, 1, D), lambda i: (i // nS, 0, 5)),
                  vec, vec,
                  pl.BlockSpec(memory_space=pl.ANY)],
        out_specs=pl.BlockSpec((tm, D), lambda i: (i, 0)),
        out_shape=jax.ShapeDtypeStruct((T, D), F32),
        scratch_shapes=[pltpu.VMEM((tm, D), F32), pltpu.VMEM((tm, D), F32), pltpu.SemaphoreType.DMA((2,))],
        compiler_params=_cparams(("arbitrary",)),
        name="moe_combine_ln",
    )(dest, dest, x2, mod_l, ln_g, ln_b, ys)


def _prep_w_in(w_in):
    L, D, _ = w_in.shape
    o = 0
    cq = w_in[..., o:o + MLA_Q_RANK]; o += MLA_Q_RANK
    ckv = w_in[..., o:o + MLA_KV_RANK]; o += MLA_KV_RANK
    kr = w_in[..., o:o + MLA_ROPE]; o += MLA_ROPE
    gq = w_in[..., o:o + GDN_WIDTH]; o += GDN_WIDTH
    gk = w_in[..., o:o + GDN_WIDTH]; o += GDN_WIDTH
    gv = w_in[..., o:o + GDN_WIDTH]; o += GDN_WIDTH
    gz = w_in[..., o:o + GDN_WIDTH]; o += GDN_WIDTH
    ga = w_in[..., o:o + GDN_HEADS]; o += GDN_HEADS
    gb = w_in[..., o:o + GDN_HEADS]; o += GDN_HEADS
    pw = w_in[..., o:o + POOL_WIDTH]
    half = MLA_ROPE // 2
    z = lambda n: jnp.zeros((L, D, n), w_in.dtype)
    kra = jnp.concatenate([z(MLA_NOPE), kr, z(HEAD_PAD - MLA_NOPE - MLA_ROPE)], axis=-1)
    krb = jnp.concatenate([z(MLA_NOPE), kr[..., half:], kr[..., :half], z(HEAD_PAD - MLA_NOPE - MLA_ROPE)], axis=-1)
    a_e = jnp.repeat(ga, GDN_DIM, axis=-1)
    b_e = jnp.repeat(gb, GDN_DIM, axis=-1)
    out = jnp.concatenate([gq, gk, gv, gz, a_e, b_e, cq, pw, ckv, kra, krb], axis=-1)
    assert out.shape[-1] == IN_COLS
    return out.astype(BF16)


def _prep_mla(w_uq, w_ukv):
    L = w_uq.shape[0]
    H, half = MLA_HEADS, MLA_ROPE // 2
    pad = HEAD_PAD - MLA_NOPE - MLA_ROPE
    q = w_uq.reshape(L, MLA_Q_RANK, H, MLA_NOPE + MLA_ROPE)
    nope, r1, r2 = q[..., :MLA_NOPE], q[..., MLA_NOPE:MLA_NOPE + half], q[..., MLA_NOPE + half:]
    zq = lambda n: jnp.zeros((L, MLA_Q_RANK, H, n), w_uq.dtype)
    plain = jnp.concatenate([nope, r1, r2, zq(pad)], axis=-1).reshape(L, MLA_Q_RANK, H * HEAD_PAD)
    partner = jnp.concatenate([zq(MLA_NOPE), r2, r1, zq(pad)], axis=-1).reshape(L, MLA_Q_RANK, H * HEAD_PAD)
    wq2 = jnp.concatenate([plain, partner], axis=-1).astype(BF16)
    kv = w_ukv.reshape(L, MLA_KV_RANK, H, MLA_NOPE + MLA_V)
    k_nope, v = kv[..., :MLA_NOPE], kv[..., MLA_NOPE:]
    zk = lambda n: jnp.zeros((L, MLA_KV_RANK, H, n), w_ukv.dtype)
    k_main = jnp.concatenate([k_nope, zk(HEAD_PAD - MLA_NOPE)], axis=-1).reshape(L, MLA_KV_RANK, H * HEAD_PAD)
    even = (jnp.arange(H) % 2 == 0)[None, None, :, None]
    v_pair = jnp.where(even, jnp.concatenate([v, zk(MLA_V)], axis=-1), jnp.concatenate([zk(MLA_V), v], axis=-1))
    wkv2 = jnp.concatenate([k_main, v_pair.reshape(L, MLA_KV_RANK, H * HEAD_PAD)], axis=-1).astype(BF16)
    return wq2, wkv2


def _rope_tables(S):
    half = MLA_ROPE // 2
    inv_freq = jnp.power(ROPE_THETA, -jnp.arange(0, MLA_ROPE, 2, dtype=F32) / MLA_ROPE)
    ang = jnp.arange(S, dtype=F32)[:, None] * inv_freq[None, :]
    cos, sin = jnp.cos(ang), jnp.sin(ang)
    pad = jnp.zeros((S, HEAD_PAD - MLA_NOPE - MLA_ROPE), F32)
    cos_t = jnp.concatenate([jnp.ones((S, MLA_NOPE), F32), cos, cos, pad], axis=-1)
    sin_t = jnp.concatenate([jnp.zeros((S, MLA_NOPE), F32), -sin, sin, pad], axis=-1)
    scale = (MLA_NOPE + MLA_ROPE) ** -0.5
    return cos_t * scale, sin_t * scale, cos_t, sin_t


def _block_diag(blocks):
    L, G, n, _ = blocks.shape
    eye = jnp.eye(G, dtype=blocks.dtype)
    return jnp.einsum('lgij,gh->lgihj', blocks, eye).reshape(L, G * n, G * n)


def kernel(x, c, w_in, mla_q_norm, mla_kv_norm, mla_w_uq, mla_w_ukv, gdn_conv, gdn_a_log, gdn_dt_bias, gdn_out_norm, pool_w, pool_scale, w_out, w_mod, b_mod, ln1_g, ln1_b, ln2_g, ln2_b, router_w_group, router_b_group, router_w_expert, router_b_expert, moe_w_gate, moe_w_up, moe_w_down):
    B, S, D = x.shape
    L = w_in.shape[0]
    T = B * S
    alpha = (2 * L) ** 0.25
    ts = min(512, S)
    tq = min(256, S)
    t_moe = min(512, S)
    t_disp = min(1024, S)
    assert D == D_MODEL and S % ts == 0 and S % tq == 0 and ts % CHUNK == 0
    assert S % t_disp == 0 and t_moe % DMA_WAIT_UNROLL == 0

    w_in2 = _prep_w_in(w_in)
    wq2, wkv2 = _prep_mla(mla_w_uq, mla_w_ukv)
    tabs = _rope_tables(S)
    gq = mla_q_norm.reshape(L, 1, MLA_Q_RANK)
    gkv = mla_kv_norm.reshape(L, 1, MLA_KV_RANK)
    alog_e = jnp.repeat(gdn_a_log, GDN_DIM, axis=-1).reshape(L, 1, GDN_WIDTH)
    dt_e = jnp.repeat(gdn_dt_bias, GDN_DIM, axis=-1).reshape(L, 1, GDN_WIDTH)
    og_e = jnp.tile(gdn_out_norm, (1, GDN_HEADS)).reshape(L, 1, GDN_WIDTH)
    lane_head = jnp.arange(GDN_WIDTH) // GDN_DIM
    seg = (lane_head[:, None] == lane_head[None, :]).astype(BF16)
    tri_c = (jnp.arange(CHUNK)[:, None] >= jnp.arange(CHUNK)[None, :]).astype(BF16)
    pool_bd = _block_diag(pool_w).astype(BF16)
    pool_sc = pool_scale.reshape(L, 1, POOL_WIDTH)
    w_out_b = w_out.astype(BF16)
    w_r = jnp.concatenate([router_w_group, router_w_expert,
                           jnp.zeros((L, D, LANE - N_GROUPS - N_EXPERTS), F32)], axis=-1)
    w_r_hi = w_r.astype(BF16)
    w_r_lo = (w_r - w_r_hi.astype(F32)).astype(BF16)
    b_r = jnp.concatenate([router_b_group, router_b_expert,
                           jnp.zeros((L, LANE - N_GROUPS - N_EXPERTS), F32)], axis=-1).reshape(L, 1, LANE)
    tri_r = (jnp.arange(ts)[:, None] > jnp.arange(ts)[None, :]).astype(BF16)
    ln1g, ln1b = ln1_g.reshape(L, 1, D), ln1_b.reshape(L, 1, D)
    ln2g, ln2b = ln2_g.reshape(L, 1, D), ln2_b.reshape(L, 1, D)

    assert N_BUCKETS <= LANE
    nb = (T + N_BUCKETS * (MOE_ROWS - 1) + MOE_ROWS - 1) // MOE_ROWS
    P = nb * MOE_ROWS
    pg, pa, pb = [], [], []
    for g_ in range(N_GROUPS):
        for a_ in range(EXPERTS_PER_GROUP):
            for b_ in range(a_ + 1, EXPERTS_PER_GROUP):
                pg.append(g_)
                pa.append(a_)
                pb.append(b_)
    bucket_g = jnp.asarray(pg, jnp.int32)
    bucket_a = jnp.asarray(pa, jnp.int32)
    bucket_b = jnp.asarray(pb, jnp.int32)
    grouped = lambda w: w.astype(BF16).reshape((L, N_GROUPS, EXPERTS_PER_GROUP) + w.shape[2:])
    wg_b, wu_b, wd_b = grouped(moe_w_gate), grouped(moe_w_up), grouped(moe_w_down)
    bucket_ids = jnp.arange(N_BUCKETS, dtype=jnp.int32)
    blk0 = jnp.arange(nb, dtype=jnp.int32) * MOE_ROWS
    xs = jnp.zeros((P, D + LANE), F32)

    mod = _modulation(c, w_mod, b_mod)
    x2 = x.reshape(T, D)
    for l in range(L):
        mod_l = mod[l].reshape(B, 1, 6 * D)
        proj = _inproj(x2, mod_l, w_in2, l, S, ts)
        q, k, v = _mla_proj(proj, tabs, gq, gkv, wq2, wkv2, l, S, ts)
        y_mla = _attention(q, k, v, B, S, tq)
        qh, kh, vh, g, beta = _gdn_pre(proj, gdn_conv, alog_e, dt_e, seg, l, B, S, ts)
        y_gdn = _gdn(qh, kh, vh, g, beta, proj, og_e, seg, tri_c, l, B, S, ts)
        y_pool = _pool(proj, pool_bd, pool_sc, l, B, S, ts)
        x2 = _outproj(y_mla, y_gdn, y_pool, w_out_b, x2, mod_l, ln1g, ln1b, l, S, ts, alpha)

        hm, cnt = _router(x2, mod_l, w_r_hi, w_r_lo, b_r, tri_r, l, S, ts)
        counts = cnt[0, :N_BUCKETS].astype(jnp.int32)
        padded = (counts + MOE_ROWS - 1) // MOE_ROWS * MOE_ROWS
        pad_end = jnp.cumsum(padded)
        pad_start = pad_end - padded
        bucket = hm[:, D].astype(jnp.int32)
        rank = hm[:, D + 1].astype(jnp.int32)
        dest = jnp.sum(jnp.where(bucket[:, None] == bucket_ids[None, :], pad_start[None, :], 0), axis=1) + rank
        block_bucket = jnp.minimum(jnp.sum((pad_end[None, :] <= blk0[:, None]).astype(jnp.int32), axis=1),
                                   N_BUCKETS - 1)
        onehot_b = (block_bucket[:, None] == bucket_ids[None, :]).astype(jnp.int32)
        grp = jnp.sum(onehot_b * bucket_g[None, :], axis=1)
        ea = jnp.sum(onehot_b * bucket_a[None, :], axis=1)
        eb = jnp.sum(onehot_b * bucket_b[None, :], axis=1)
        used = (blk0 < pad_end[-1]).astype(jnp.int32)

        xs = _dispatch(hm, dest, xs, t_disp)
        ys = _ffn(xs, grp, ea, eb, used, wg_b, wu_b, wd_b, l)
        x2 = _combine(ys, dest, x2, mod_l, ln2g, ln2b, l, S, t_moe, alpha)
    return x2.reshape(B, S, D)
```

```python
import functools
import math

import jax
import jax.numpy as jnp
from jax import lax
from jax.experimental import pallas as pl
from jax.experimental.pallas import tpu as pltpu

F32 = jnp.float32
BF16 = jnp.bfloat16

D_MODEL = 1024
CHUNK = 64
MLA_HEADS = 8
MLA_NOPE = 64
MLA_ROPE = 32
MLA_V = 64
MLA_Q_RANK = 256
MLA_KV_RANK = 128
MLA_WIDTH = MLA_HEADS * MLA_V
ROPE_THETA = 10000.0
GDN_HEADS = 4
GDN_DIM = 64
GDN_WIDTH = GDN_HEADS * GDN_DIM
GDN_CONV = 4
POOL_WIDTH = 256
POOL_WINDOWS = (2, 4, 8, 16)
POOL_GROUP_DIM = 64
N_GROUPS = 4
EXPERTS_PER_GROUP = 8
N_EXPERTS = N_GROUPS * EXPERTS_PER_GROUP
TOP_K = 2
EXPERT_FF = 256
LN_EPS = 1e-5
RMS_EPS = 1e-6

LANE = 128
HEAD_PAD = 128
PAIRS_PER_GROUP = EXPERTS_PER_GROUP * (EXPERTS_PER_GROUP - 1) // 2
N_BUCKETS = N_GROUPS * PAIRS_PER_GROUP
MOE_ROWS = 128
VMEM_LIMIT = 48 * 1024 * 1024

_C_GQ, _C_GK, _C_GV, _C_GZ = 0, 256, 512, 768
_C_A, _C_B = 1024, 1280
_C_CQ, _C_POOL = 1536, 1792
_C_CKV, _C_KRA, _C_KRB = 2048, 2176, 2304
IN_COLS = 2432


def _cparams(sem):
    return pltpu.CompilerParams(dimension_semantics=sem, vmem_limit_bytes=VMEM_LIMIT)


def _sigmoid(x):
    return 1.0 / (1.0 + jnp.exp(-x))


def _split2(x):
    hi = x.astype(BF16)
    lo = (x - hi.astype(F32)).astype(BF16)
    return hi, lo


def _split3(x):
    hi = x.astype(BF16)
    r = x - hi.astype(F32)
    mid = r.astype(BF16)
    lo = (r - mid.astype(F32)).astype(BF16)
    return hi, mid, lo


def _dot(a, b):
    return jnp.dot(a, b, preferred_element_type=F32)


def _dot_nt(a, b):
    return lax.dot_general(a, b, (((1,), (1,)), ((), ())), preferred_element_type=F32)


def _dot_tn(a, b):
    return lax.dot_general(a, b, (((0,), (0,)), ((), ())), preferred_element_type=F32)


def _mod_body(c_ref, w_ref, b_ref, o_ref):
    c = c_ref[...]
    ca = c * _sigmoid(c)
    o_ref[0] = _dot(ca.astype(BF16), w_ref[0].astype(BF16)) + b_ref[0]


def _modulation(c, w_mod, b_mod):
    L, D, N = w_mod.shape
    B = c.shape[0]
    tn = 1024
    return pl.pallas_call(
        _mod_body,
        grid=(L, N // tn),
        in_specs=[pl.BlockSpec((B, D), lambda l, j: (0, 0)),
                  pl.BlockSpec((1, D, tn), lambda l, j: (l, 0, j)),
                  pl.BlockSpec((1, 1, tn), lambda l, j: (l, 0, j))],
        out_specs=pl.BlockSpec((1, B, tn), lambda l, j: (l, 0, j)),
        out_shape=jax.ShapeDtypeStruct((L, B, N), F32),
        compiler_params=_cparams(("arbitrary", "arbitrary")),
        name="modulation",
    )(c, w_mod, b_mod.reshape(L, 1, N))


def _inproj_body(x_ref, sh_ref, sc_ref, w_ref, o_ref):
    h = x_ref[...] * (1.0 + sc_ref[0]) + sh_ref[0]
    o_ref[...] = _dot(h.astype(BF16), w_ref[0])


def _inproj(x2, mod_l, w_in2, l, S, tm):
    T, D = x2.shape
    nS = S // tm
    return pl.pallas_call(
        _inproj_body,
        grid=(T // tm,),
        in_specs=[pl.BlockSpec((tm, D), lambda i: (i, 0)),
                  pl.BlockSpec((1, 1, D), lambda i: (i // nS, 0, 0)),
                  pl.BlockSpec((1, 1, D), lambda i: (i // nS, 0, 1)),
                  pl.BlockSpec((1, D, IN_COLS), lambda i: (l, 0, 0))],
        out_specs=pl.BlockSpec((tm, IN_COLS), lambda i: (i, 0)),
        out_shape=jax.ShapeDtypeStruct((T, IN_COLS), F32),
        compiler_params=_cparams(("arbitrary",)),
        name="inproj",
    )(x2, mod_l, mod_l, w_in2)


def _mla_proj_body(cq_ref, ckv_ref, kra_ref, krb_ref, cosq_ref, sinq_ref, cosk_ref, sink_ref,
                   gq_ref, gkv_ref, wq_ref, wkv_ref, q_out, k_out, v_out):
    cq = cq_ref[...]
    qn = cq * lax.rsqrt(jnp.mean(cq * cq, axis=-1, keepdims=True) + RMS_EPS) * gq_ref[0]
    q2 = _dot(qn.astype(BF16), wq_ref[0])
    ckv = ckv_ref[...]
    kvn = ckv * lax.rsqrt(jnp.mean(ckv * ckv, axis=-1, keepdims=True) + RMS_EPS) * gkv_ref[0]
    kv2 = _dot(kvn.astype(BF16), wkv_ref[0])
    cq_t, sq_t = cosq_ref[...], sinq_ref[...]
    krope = kra_ref[...] * cosk_ref[...] + krb_ref[...] * sink_ref[...]
    hw = MLA_HEADS * HEAD_PAD
    for h in range(MLA_HEADS):
        a, b = h * HEAD_PAD, (h + 1) * HEAD_PAD
        q_out[:, a:b] = (q2[:, a:b] * cq_t + q2[:, hw + a:hw + b] * sq_t).astype(BF16)
        k_out[:, a:b] = (kv2[:, a:b] + krope).astype(BF16)
    vl = lax.broadcasted_iota(jnp.int32, (1, hw), 1)
    ones_lane = jnp.where((vl // HEAD_PAD) % 2 == 0, MLA_V, 0)
    v_out[...] = (kv2[:, hw:] + (vl % HEAD_PAD == ones_lane).astype(F32)).astype(BF16)


def _mla_proj(proj, tabs, gq, gkv, wq2, wkv2, l, S, ts):
    T = proj.shape[0]
    nS = S // ts
    hw = MLA_HEADS * HEAD_PAD
    tab_spec = pl.BlockSpec((ts, LANE), lambda i: (i % nS, 0))
    return pl.pallas_call(
        _mla_proj_body,
        grid=(T // ts,),
        in_specs=[pl.BlockSpec((ts, 256), lambda i: (i, _C_CQ // 256)),
                  pl.BlockSpec((ts, 128), lambda i: (i, _C_CKV // 128)),
                  pl.BlockSpec((ts, 128), lambda i: (i, _C_KRA // 128)),
                  pl.BlockSpec((ts, 128), lambda i: (i, _C_KRB // 128)),
                  tab_spec, tab_spec, tab_spec, tab_spec,
                  pl.BlockSpec((1, 1, MLA_Q_RANK), lambda i: (l, 0, 0)),
                  pl.BlockSpec((1, 1, MLA_KV_RANK), lambda i: (l, 0, 0)),
                  pl.BlockSpec((1, MLA_Q_RANK, 2 * hw), lambda i: (l, 0, 0)),
                  pl.BlockSpec((1, MLA_KV_RANK, 2 * hw), lambda i: (l, 0, 0))],
        out_specs=[pl.BlockSpec((ts, hw), lambda i: (i, 0))] * 3,
        out_shape=[jax.ShapeDtypeStruct((T, hw), BF16)] * 3,
        compiler_params=_cparams(("arbitrary",)),
        name="mla_proj",
    )(proj, proj, proj, proj, *tabs, gq, gkv, wq2, wkv2)


def _attn_body(qi_tab, kj_tab, q_ref, k_ref, v_ref, o_ref, m_s, acc_s, *, tq, n_q, n_tiles):
    row_c = lax.broadcasted_iota(jnp.int32, (tq, tq), 0) // CHUNK
    col_c = lax.broadcasted_iota(jnp.int32, (tq, tq), 1) // CHUNK
    allowed = col_c <= row_c
    lane = lax.broadcasted_iota(jnp.int32, (tq, HEAD_PAD), 1)
    sl = [slice(hh * HEAD_PAD, (hh + 1) * HEAD_PAD) for hh in range(2)]
    hs = range(2)

    def rows_of(tile):
        return pl.ds(pl.multiple_of(tile * tq, tq), tq)

    def stage_a(t):
        q = q_ref[rows_of(qi_tab[t]), :]
        kk = k_ref[rows_of(kj_tab[t]), :]
        return tuple(_dot_nt(q[:, sl[hh]], kk[:, sl[hh]]).astype(BF16) for hh in hs)

    def stage_b(t, s_pair):
        qi, kj = qi_tab[t], kj_tab[t]
        rows = rows_of(qi)
        keep = jnp.logical_or(allowed, kj < qi)
        neg = jnp.asarray(-jnp.inf, BF16)
        s = [jnp.where(keep, s_pair[hh], neg) for hh in hs]
        m_prev = [jnp.where(kj == 0, neg, m_s[hh, rows, :]) for hh in hs]
        m_new = [jnp.maximum(m_prev[hh], jnp.max(s[hh], axis=-1, keepdims=True)) for hh in hs]
        p = tuple(jnp.exp(s[hh] - jnp.concatenate([m_new[hh]] * (tq // HEAD_PAD), axis=1)) for hh in hs)
        alpha = tuple(jnp.exp(m_prev[hh] - m_new[hh]) for hh in hs)
        for hh in hs:
            m_s[hh, rows, :] = m_new[hh]
        return p, alpha

    def stage_c(t, p, alpha):
        qi, kj = qi_tab[t], kj_tab[t]
        rows = rows_of(qi)
        vv = v_ref[rows_of(kj), :]
        pv = [_dot(p[hh], vv[:, sl[hh]]) for hh in hs]
        for hh in hs:
            acc_s[hh, rows, :] = alpha[hh].astype(F32) * jnp.where(kj == 0, 0.0, acc_s[hh, rows, :]) + pv[hh]

    def body(t, c):
        s_next, (p, alpha) = c
        s_after = stage_a(t + 2)
        stage_c(t, p, alpha)
        return s_after, stage_b(t + 1, s_next)

    pb0 = stage_b(0, stage_a(0))
    s_last, pb = lax.fori_loop(0, n_tiles - 2, body, (stage_a(1), pb0))
    pb_last = stage_b(n_tiles - 1, s_last)
    stage_c(n_tiles - 2, *pb)
    stage_c(n_tiles - 1, *pb_last)

    def normalize(i, _):
        rows = rows_of(i)
        acc0, acc1 = acc_s[0, rows, :], acc_s[1, rows, :]
        o0 = jnp.where(lane < MLA_V, acc0 * (1.0 / acc0[:, MLA_V:MLA_V + 1]), 0.0)
        o1 = jnp.where(lane >= MLA_V, acc1 * (1.0 / acc1[:, 0:1]), 0.0)
        o_ref[rows, :] = (o0 + o1).astype(BF16)
        return 0

    lax.fori_loop(0, n_q, normalize, 0)


def _attention(q, k, v, B, S, tq):
    T = q.shape[0]
    nq = S // tq
    pairs = MLA_HEADS // 2
    tiles = [(i, j) for i in range(nq) for j in range(i + 1)]
    assert len(tiles) >= 2
    qi_tab = jnp.asarray([t[0] for t in tiles], jnp.int32)
    kj_tab = jnp.asarray([t[1] for t in tiles], jnp.int32)
    seq = pl.BlockSpec((S, 2 * HEAD_PAD), lambda b, p, qt, kt: (b, p))
    grid_spec = pltpu.PrefetchScalarGridSpec(
        num_scalar_prefetch=2,
        grid=(B, pairs),
        in_specs=[seq, seq, seq],
        out_specs=pl.BlockSpec((S, 2 * MLA_V), lambda b, p, qt, kt: (b, p)),
        scratch_shapes=[pltpu.VMEM((2, S, HEAD_PAD), BF16), pltpu.VMEM((2, S, HEAD_PAD), F32)],
    )
    return pl.pallas_call(
        functools.partial(_attn_body, tq=tq, n_q=nq, n_tiles=len(tiles)),
        grid_spec=grid_spec,
        out_shape=jax.ShapeDtypeStruct((T, MLA_WIDTH), BF16),
        compiler_params=_cparams(("arbitrary", "arbitrary")),
        name="mla_attention",
    )(qi_tab, kj_tab, q, k, v)


def _pool_body(p_ref, w_ref, sc_ref, o_ref, prev_ref, *, ts):
    j = pl.program_id(1)
    hist = 16

    @pl.when(j == 0)
    def _():
        prev_ref[...] = jnp.zeros_like(prev_ref)

    cur = p_ref[...]
    x = jnp.concatenate([prev_ref[...], cur], axis=0)
    s1 = x + pltpu.roll(x, 1, axis=0)
    s2 = s1 + pltpu.roll(s1, 2, axis=0)
    s4 = s2 + pltpu.roll(s2, 4, axis=0)
    s8 = s4 + pltpu.roll(s4, 8, axis=0)
    pos = (j * ts + lax.broadcasted_iota(jnp.int32, (ts, POOL_WIDTH), 0) + 1).astype(F32)
    lane = lax.broadcasted_iota(jnp.int32, (ts, POOL_WIDTH), 1)
    sums = (s1, s2, s4, s8)
    mean = None
    for gi, win in enumerate(POOL_WINDOWS):
        m_g = sums[gi][hist:] / jnp.minimum(pos, float(win))
        mean = m_g if mean is None else jnp.where(lane >= gi * POOL_GROUP_DIM, m_g, mean)
    delta = mean - cur
    o_ref[...] = (_dot(delta.astype(BF16), w_ref[0]) * sc_ref[0]).astype(BF16)
    prev_ref[...] = cur[ts - hist:]


def _pool(proj, w_bd, scale, l, B, S, ts):
    T = proj.shape[0]
    nS = S // ts
    return pl.pallas_call(
        functools.partial(_pool_body, ts=ts),
        grid=(B, nS),
        in_specs=[pl.BlockSpec((ts, POOL_WIDTH), lambda b, j: (b * nS + j, _C_POOL // 256)),
                  pl.BlockSpec((1, POOL_WIDTH, POOL_WIDTH), lambda b, j: (l, 0, 0)),
                  pl.BlockSpec((1, 1, POOL_WIDTH), lambda b, j: (l, 0, 0))],
        out_specs=pl.BlockSpec((ts, POOL_WIDTH), lambda b, j: (b * nS + j, 0)),
        out_shape=jax.ShapeDtypeStruct((T, POOL_WIDTH), BF16),
        scratch_shapes=[pltpu.VMEM((16, POOL_WIDTH), F32)],
        compiler_params=_cparams(("arbitrary", "arbitrary")),
        name="pool_mixer",
    )(proj, w_bd, scale)


def _head_sum(x, seg_ref):
    hi, lo = _split2(x)
    return _dot(hi, seg_ref[...]) + _dot(lo, seg_ref[...])


def _gdn_pre_body(q_ref, k_ref, v_ref, a_ref, b_ref, conv_ref, alog_ref, dt_ref, seg_ref,
                  qo_ref, ko_ref, vo_ref, go_ref, bo_ref, prev_ref, *, ts):
    j = pl.program_id(1)
    hist = 8

    @pl.when(j == 0)
    def _():
        prev_ref[...] = jnp.zeros_like(prev_ref)

    cw = conv_ref[0]
    outs = []
    for idx, ref in enumerate((q_ref, k_ref, v_ref)):
        cur = ref[...]
        a, b = idx * GDN_WIDTH, (idx + 1) * GDN_WIDTH
        x = jnp.concatenate([prev_ref[:, a:b], cur], axis=0)
        w = cw[:, a:b]
        y = (w[3:4] * x + w[2:3] * pltpu.roll(x, 1, axis=0) + w[1:2] * pltpu.roll(x, 2, axis=0)
             + w[0:1] * pltpu.roll(x, 3, axis=0))[hist:]
        outs.append(y * _sigmoid(y))
        prev_ref[:, a:b] = cur[ts - hist:]
    qc, kc, vc = outs
    qo_ref[...] = qc * lax.rsqrt(_head_sum(qc * qc, seg_ref) + RMS_EPS) * (GDN_DIM ** -0.5)
    ko_ref[...] = kc * lax.rsqrt(_head_sum(kc * kc, seg_ref) + RMS_EPS)
    vo_ref[...] = vc
    z = a_ref[...] + dt_ref[0]
    softplus = jnp.maximum(z, 0.0) + jnp.log(1.0 + jnp.exp(-jnp.abs(z)))
    go_ref[...] = -jnp.exp(alog_ref[0]) * softplus
    bo_ref[...] = _sigmoid(b_ref[...])


def _gdn_pre(proj, conv_w, alog_e, dt_e, seg, l, B, S, ts):
    T = proj.shape[0]
    nS = S // ts
    W = GDN_WIDTH

    def col(c):
        return pl.BlockSpec((ts, W), lambda b, j: (b * nS + j, c // W))

    return pl.pallas_call(
        functools.partial(_gdn_pre_body, ts=ts),
        grid=(B, nS),
        in_specs=[col(_C_GQ), col(_C_GK), col(_C_GV), col(_C_A), col(_C_B),
                  pl.BlockSpec((1, GDN_CONV, 3 * W), lambda b, j: (l, 0, 0)),
                  pl.BlockSpec((1, 1, W), lambda b, j: (l, 0, 0)),
                  pl.BlockSpec((1, 1, W), lambda b, j: (l, 0, 0)),
                  pl.BlockSpec((W, W), lambda b, j: (0, 0))],
        out_specs=[pl.BlockSpec((ts, W), lambda b, j: (b * nS + j, 0))] * 5,
        out_shape=[jax.ShapeDtypeStruct((T, W), F32)] * 5,
        scratch_shapes=[pltpu.VMEM((8, 3 * W), F32)],
        compiler_params=_cparams(("arbitrary", "arbitrary")),
        name="gdn_pre",
    )(proj, proj, proj, proj, proj, conv_w, alog_e, dt_e, seg)


def _gdn_body(q_ref, k_ref, v_ref, g_ref, b_ref, z_ref, og_ref, seg_ref, tri_ref,
              o_ref, state_ref, u_s, w_s, qk_s, qd_s, kd_s, gl_s, *, ts):
    j = pl.program_id(1)
    C, W, H = CHUNK, GDN_WIDTH, GDN_HEADS
    n_chunks = ts // C
    groups = n_chunks // INTRA_UNROLL

    @pl.when(j == 0)
    def _():
        state_ref[...] = jnp.zeros_like(state_ref)

    lane = lax.broadcasted_iota(jnp.int32, (C, W), 1)
    row = lax.broadcasted_iota(jnp.int32, (C, W), 0)
    col_tok = lane % C
    incl = row >= col_tok
    strict = row > col_tok
    eye = (row == col_tok).astype(F32)
    head_masks = [(lane // C == h).astype(F32) for h in range(H)]
    bd_mask = (lax.broadcasted_iota(jnp.int32, (W, W), 0) // C
               == lax.broadcasted_iota(jnp.int32, (W, W), 1) // C)

    def expand(m):
        return jnp.concatenate([m * hm for hm in head_masks], axis=0)

    def bdot(lhs, rhs):
        return _dot(lhs.astype(BF16), rhs.astype(BF16))

    def intra(i, _):
        sq = i // groups
        cs = [(i % groups) * INTRA_UNROLL + uu for uu in range(INTRA_UNROLL)]
        r0s = [pl.multiple_of(c * C, C) for c in cs]
        n = range(INTRA_UNROLL)
        tri = tri_ref[...]
        gs = [_split3(g_ref[sq, pl.ds(r0, C), :]) for r0 in r0s]
        gc = [_dot(tri, g[0]) + _dot(tri, g[1]) + _dot(tri, g[2]) for g in gs]
        ks = [k_ref[sq, pl.ds(r0, C), :] for r0 in r0s]
        qs = [q_ref[sq, pl.ds(r0, C), :] for r0 in r0s]
        betas = [b_ref[sq, pl.ds(r0, C), :] for r0 in r0s]
        kb = [ks[x] * betas[x] for x in n]
        aq = [_dot_nt(jnp.concatenate([kb[x], qs[x]], axis=0).astype(BF16), expand(ks[x]).astype(BF16)) for x in n]
        g_row = [jnp.sum(gc[x] * eye, axis=0, keepdims=True) for x in n]
        g_last = [gc[x][C - 1:C, :] for x in n]
        decay = [jnp.where(incl, jnp.exp(jnp.where(incl, gc[x] - g_row[x], 0.0)), 0.0) for x in n]
        e_gc = [jnp.exp(gc[x]) for x in n]
        a_cat = [jnp.where(strict, aq[x][:C] * decay[x], 0.0) for x in n]
        for x in n:
            r0 = r0s[x]
            qk_s[sq, pl.ds(r0, C), :] = jnp.where(incl, aq[x][C:] * decay[x], 0.0).astype(BF16)
            qd_s[sq, pl.ds(r0, C), :] = (qs[x] * e_gc[x]).astype(BF16)
            kd_s[sq, pl.ds(r0, C), :] = (ks[x] * jnp.exp(g_last[x] - gc[x])).astype(BF16)
            gl_s[sq, pl.ds(pl.multiple_of(cs[x] * 8, 8), 8), :] = jnp.broadcast_to(jnp.exp(g_last[x]), (8, W))
        x_cat = [eye - a_cat[x] for x in n]
        p_cat = [bdot(a_cat[x], expand(a_cat[x])) for x in n]
        n_fac = int(math.log2(C)) - 1
        for r in range(n_fac):
            last = r == n_fac - 1
            xp = [bdot(x_cat[x] if last else jnp.concatenate([x_cat[x], p_cat[x]], axis=0), expand(p_cat[x]))
                  for x in n]
            x_cat = [x_cat[x] + xp[x][:C] for x in n]
            if not last:
                p_cat = [xp[x][C:] for x in n]
        for x in n:
            r0 = r0s[x]
            t_cat = x_cat[x].astype(BF16)
            v = v_ref[sq, pl.ds(r0, C), :]
            u_s[sq, pl.ds(r0, C), :] = _dot(t_cat, expand(v * betas[x]).astype(BF16))
            w_s[sq, pl.ds(r0, C), :] = _dot(t_cat, expand(kb[x] * e_gc[x]).astype(BF16)).astype(BF16)
        return 0

    lax.fori_loop(0, GDN_SEQS * groups, intra, 0)

    def scan(c, _):
        r0 = pl.multiple_of(c * C, C)
        sq = range(GDN_SEQS)
        state = [state_ref[s] for s in sq]
        wq = [_dot(jnp.concatenate([w_s[s, pl.ds(r0, C), :], qd_s[s, pl.ds(r0, C), :]], axis=0),
                   state[s].astype(BF16)) for s in sq]
        v_new = [u_s[s, pl.ds(r0, C), :] - wq[s][:C] for s in sq]
        upd = [_dot_tn(kd_s[s, pl.ds(r0, C), :], v_new[s].astype(BF16)) for s in sq]
        for s in sq:
            g_l = gl_s[s, pl.ds(pl.multiple_of(c * 8, 8), 1), :]
            state_ref[s] = state[s] * g_l + jnp.where(bd_mask, upd[s], 0.0)
        o = [wq[s][C:] + _dot(qk_s[s, pl.ds(r0, C), :], expand(v_new[s]).astype(BF16)) for s in sq]
        ms = [_head_sum(o[s] * o[s], seg_ref) * (1.0 / GDN_DIM) for s in sq]
        for s in sq:
            z = z_ref[s, pl.ds(r0, C), :]
            y = o[s] * lax.rsqrt(ms[s] + RMS_EPS) * og_ref[0] * (z * _sigmoid(z))
            o_ref[s, pl.ds(r0, C), :] = y.astype(BF16)
        return 0

    lax.fori_loop(0, n_chunks, scan, 0)


def _gdn(qh, kh, vh, g, beta, proj, og, seg, tri, l, B, S, ts):
    T = qh.shape[0]
    nS = S // ts
    W = GDN_WIDTH
    Q = GDN_SEQS
    seq3 = lambda a: a.reshape(B, S, a.shape[-1])
    row = pl.BlockSpec((Q, ts, W), lambda b, j: (b, j, 0))
    out = pl.pallas_call(
        functools.partial(_gdn_body, ts=ts),
        grid=(B // Q, nS),
        in_specs=[row, row, row, row, row,
                  pl.BlockSpec((Q, ts, W), lambda b, j: (b, j, _C_GZ // W)),
                  pl.BlockSpec((1, 1, W), lambda b, j: (l, 0, 0)),
                  pl.BlockSpec((W, W), lambda b, j: (0, 0)),
                  pl.BlockSpec((CHUNK, CHUNK), lambda b, j: (0, 0))],
        out_specs=pl.BlockSpec((Q, ts, W), lambda b, j: (b, j, 0)),
        out_shape=jax.ShapeDtypeStruct((B, S, W), BF16),
        scratch_shapes=[pltpu.VMEM((Q, W, W), F32), pltpu.VMEM((Q, ts, W), F32)]
        + [pltpu.VMEM((Q, ts, W), BF16)] * 4 + [pltpu.VMEM((Q, 8 * (ts // CHUNK), W), F32)],
        compiler_params=_cparams(("arbitrary", "arbitrary")),
        name="gdn_delta_rule",
    )(seq3(qh), seq3(kh), seq3(vh), seq3(g), seq3(beta), seq3(proj), og, seg, tri)
    return out.reshape(T, W)


def _layer_norm(r, g, b):
    mu = jnp.mean(r, axis=-1, keepdims=True)
    d = r - mu
    var = jnp.mean(d * d, axis=-1, keepdims=True)
    return d * lax.rsqrt(var + LN_EPS) * g + b


def _outproj_body(ym_ref, yg_ref, yp_ref, w_ref, x_ref, gt_ref, lg_ref, lb_ref, o_ref, *, alpha):
    w = w_ref[0]
    y = (_dot(ym_ref[...], w[:MLA_WIDTH]) + _dot(yg_ref[...], w[MLA_WIDTH:MLA_WIDTH + GDN_WIDTH])
         + _dot(yp_ref[...], w[MLA_WIDTH + GDN_WIDTH:]))
    r = alpha * x_ref[...] + (1.0 + gt_ref[0]) * y
    o_ref[...] = _layer_norm(r, lg_ref[0], lb_ref[0])


def _outproj(y_mla, y_gdn, y_pool, w_out, x2, mod_l, ln_g, ln_b, l, S, tm, alpha):
    T, D = x2.shape
    nS = S // tm
    vec = pl.BlockSpec((1, 1, D), lambda i: (l, 0, 0))
    return pl.pallas_call(
        functools.partial(_outproj_body, alpha=alpha),
        grid=(T // tm,),
        in_specs=[pl.BlockSpec((tm, MLA_WIDTH), lambda i: (i, 0)),
                  pl.BlockSpec((tm, GDN_WIDTH), lambda i: (i, 0)),
                  pl.BlockSpec((tm, POOL_WIDTH), lambda i: (i, 0)),
                  pl.BlockSpec((1, D, D), lambda i: (l, 0, 0)),
                  pl.BlockSpec((tm, D), lambda i: (i, 0)),
                  pl.BlockSpec((1, 1, D), lambda i: (i // nS, 0, 2)),
                  vec, vec],
        out_specs=pl.BlockSpec((tm, D), lambda i: (i, 0)),
        out_shape=jax.ShapeDtypeStruct((T, D), F32),
        compiler_params=_cparams(("arbitrary",)),
        name="outproj_ln",
    )(y_mla, y_gdn, y_pool, w_out, x2, mod_l, ln_g, ln_b)


def _lane_first(cond, lane_f):
    return jnp.min(jnp.where(cond, lane_f, float(LANE)), axis=-1, keepdims=True)


def _router_body(x_ref, sh_ref, sc_ref, whi_ref, wlo_ref, br_ref, tri_ref,
                 hm_out, cnt_out, carry_ref):
    i = pl.program_id(0)

    @pl.when(i == 0)
    def _():
        carry_ref[...] = jnp.zeros_like(carry_ref)

    h = x_ref[...] * (1.0 + sc_ref[0]) + sh_ref[0]
    hm_out[:, :D_MODEL] = h
    h_hi, h_lo = _split2(h)
    logits = _dot(h_hi, whi_ref[0]) + _dot(h_lo, whi_ref[0]) + _dot(h_hi, wlo_ref[0]) + br_ref[0]
    tm = logits.shape[0]
    lane = lax.broadcasted_iota(jnp.int32, (tm, LANE), 1)
    lane_f = lane.astype(F32)
    neg = -jnp.inf
    gl = jnp.where(lane < N_GROUPS, logits, neg)
    gmax = jnp.max(gl, axis=-1, keepdims=True)
    gsel = _lane_first(gl == gmax, lane_f)
    g_p = 1.0 / jnp.sum(jnp.exp(gl - gmax), axis=-1, keepdims=True)
    lo = N_GROUPS + EXPERTS_PER_GROUP * gsel
    el = jnp.where((lane_f >= lo) & (lane_f < lo + EXPERTS_PER_GROUP), logits, neg)
    m1 = jnp.max(el, axis=-1, keepdims=True)
    i1 = _lane_first(el == m1, lane_f)
    el2 = jnp.where(lane_f == i1, neg, el)
    m2 = jnp.max(el2, axis=-1, keepdims=True)
    i2 = _lane_first(el2 == m2, lane_f)
    t = jnp.exp(m2 - m1)
    w1 = g_p / (1.0 + t)
    w2 = g_p * t / (1.0 + t)
    loc1 = i1 - lo
    loc2 = i2 - lo
    a_loc = jnp.minimum(loc1, loc2)
    b_loc = jnp.maximum(loc1, loc2)
    pair = a_loc * (2 * EXPERTS_PER_GROUP - 1 - a_loc) * 0.5 + (b_loc - a_loc - 1.0)
    bucket = gsel * PAIRS_PER_GROUP + pair
    first_is_a = loc1 < loc2
    w_a = jnp.where(first_is_a, w1, w2)
    w_b = jnp.where(first_is_a, w2, w1)
    hit = lane_f == bucket
    onehot = hit.astype(BF16)
    before = _dot(tri_ref[...], onehot) + carry_ref[0:1, :]
    rank = jnp.sum(jnp.where(hit, before, 0.0), axis=-1, keepdims=True)
    total = carry_ref[0:1, :] + jnp.sum(onehot.astype(F32), axis=0, keepdims=True)
    carry_ref[...] = jnp.broadcast_to(total, carry_ref.shape)
    cnt_out[...] = jnp.broadcast_to(total, cnt_out.shape)
    meta = jnp.zeros((tm, LANE), F32)
    for idx, val in enumerate((bucket, rank, w_a, w_b)):
        meta = jnp.where(lane == idx, val, meta)
    hm_out[:, D_MODEL:] = meta


def _router(x2, mod_l, w_hi, w_lo, b_r, tri, l, S, tm):
    T, D = x2.shape
    nS = S // tm
    return pl.pallas_call(
        _router_body,
        grid=(T // tm,),
        in_specs=[pl.BlockSpec((tm, D), lambda i: (i, 0)),
                  pl.BlockSpec((1, 1, D), lambda i: (i // nS, 0, 3)),
                  pl.BlockSpec((1, 1, D), lambda i: (i // nS, 0, 4)),
                  pl.BlockSpec((1, D, LANE), lambda i: (l, 0, 0)),
                  pl.BlockSpec((1, D, LANE), lambda i: (l, 0, 0)),
                  pl.BlockSpec((1, 1, LANE), lambda i: (l, 0, 0)),
                  pl.BlockSpec((tm, tm), lambda i: (0, 0))],
        out_specs=[pl.BlockSpec((tm, D + LANE), lambda i: (i, 0)),
                   pl.BlockSpec((8, LANE), lambda i: (0, 0))],
        out_shape=[jax.ShapeDtypeStruct((T, D + LANE), F32),
                   jax.ShapeDtypeStruct((8, LANE), F32)],
        scratch_shapes=[pltpu.VMEM((8, LANE), F32)],
        compiler_params=_cparams(("arbitrary",)),
        name="router",
    )(x2, mod_l, mod_l, w_hi, w_lo, b_r, tri)


def _row_copy(src, s, dst, d, sem):
    return pltpu.make_async_copy(src.at[pl.ds(s, 1)], dst.at[pl.ds(d, 1)], sem)


DMA_WAIT_UNROLL = 32
INTRA_UNROLL = 4
GDN_SEQS = 2


def _issue_rows(n, make_copy):
    def body(r, _):
        make_copy(r).start()
        return 0

    lax.fori_loop(0, n, body, 0, unroll=8)


def _wait_rows(n, make_copy):
    def body(_, c):
        for _u in range(DMA_WAIT_UNROLL):
            make_copy(0).wait()
        return c

    lax.fori_loop(0, n // DMA_WAIT_UNROLL, body, 0)


def _dispatch_body(dest_ref, hm_ref, xs_in_ref, xs_ref, sem, *, tm):
    del xs_in_ref
    copy = lambda r: _row_copy(hm_ref, r, xs_ref, dest_ref[r], sem)
    _issue_rows(tm, copy)
    _wait_rows(tm, lambda r: _row_copy(hm_ref, 0, xs_ref, 0, sem))


def _dispatch(hm, dest, xs_buf, tm):
    T, W = hm.shape
    return pl.pallas_call(
        functools.partial(_dispatch_body, tm=tm),
        grid=(T // tm,),
        in_specs=[pl.BlockSpec((tm,), lambda i: (i,), memory_space=pltpu.SMEM),
                  pl.BlockSpec((tm, W), lambda i: (i, 0)),
                  pl.BlockSpec(memory_space=pl.ANY)],
        out_specs=pl.BlockSpec(memory_space=pl.ANY),
        out_shape=jax.ShapeDtypeStruct(xs_buf.shape, F32),
        input_output_aliases={2: 0},
        scratch_shapes=[pltpu.SemaphoreType.DMA],
        compiler_params=_cparams(("arbitrary",)),
        name="moe_dispatch",
    )(dest, hm, xs_buf)


def _ffn_body(grp_ref, ea_ref, eb_ref, used_ref, xs_ref, wg_ref, wu_ref, wd_ref, o_ref):
    i = pl.program_id(0)

    @pl.when(used_ref[i] != 0)
    def _():
        a, b = ea_ref[i], eb_ref[i]
        xm = xs_ref[...]
        x = xm[:, :D_MODEL].astype(BF16)
        w_a = xm[:, D_MODEL + 2:D_MODEL + 3]
        w_b = xm[:, D_MODEL + 3:D_MODEL + 4]
        g_a, u_a = _dot(x, wg_ref[0, 0, a]), _dot(x, wu_ref[0, 0, a])
        g_b, u_b = _dot(x, wg_ref[0, 0, b]), _dot(x, wu_ref[0, 0, b])
        act_a = (g_a * _sigmoid(g_a) * u_a * w_a).astype(BF16)
        act_b = (g_b * _sigmoid(g_b) * u_b * w_b).astype(BF16)
        o_ref[...] = _dot(act_a, wd_ref[0, 0, a]) + _dot(act_b, wd_ref[0, 0, b])

    @pl.when(used_ref[i] == 0)
    def _():
        o_ref[...] = jnp.zeros_like(o_ref)


def _ffn(xs, grp, ea, eb, used, w_gate, w_up, w_down, l):
    P, W = xs.shape
    D = D_MODEL
    nb = P // MOE_ROWS
    FF = EXPERT_FF
    E = EXPERTS_PER_GROUP
    w_up_spec = pl.BlockSpec((1, 1, E, D, FF), lambda i, g, ea, eb, u: (l, g[i], 0, 0, 0))
    w_dn_spec = pl.BlockSpec((1, 1, E, FF, D), lambda i, g, ea, eb, u: (l, g[i], 0, 0, 0))
    grid_spec = pltpu.PrefetchScalarGridSpec(
        num_scalar_prefetch=4,
        grid=(nb,),
        in_specs=[pl.BlockSpec((MOE_ROWS, W), lambda i, g, ea, eb, u: (i, 0)),
                  w_up_spec, w_up_spec, w_dn_spec],
        out_specs=pl.BlockSpec((MOE_ROWS, D), lambda i, g, ea, eb, u: (i, 0)),
    )
    return pl.pallas_call(
        _ffn_body,
        grid_spec=grid_spec,
        out_shape=jax.ShapeDtypeStruct((P, D), F32),
        compiler_params=_cparams(("arbitrary",)),
        name="moe_ffn",
    )(grp, ea, eb, used, xs, w_gate, w_up, w_down)


def _combine_body(dcur_ref, dnext_ref, x_ref, gt_ref, lg_ref, lb_ref, ys_ref, o_ref,
                  y0_ref, y1_ref, sems, *, tm, n_steps, alpha):
    i = pl.program_id(0)
    bufs = (y0_ref, y1_ref)

    def issue(dref, slot):
        _issue_rows(tm, lambda r: _row_copy(ys_ref, dref[r], bufs[slot], r, sems.at[slot]))

    def finish(slot):
        _wait_rows(tm, lambda r: _row_copy(ys_ref, 0, bufs[slot], 0, sems.at[slot]))
        r = alpha * x_ref[...] + (1.0 + gt_ref[0]) * bufs[slot][...]
        o_ref[...] = _layer_norm(r, lg_ref[0], lb_ref[0])

    @pl.when(i == 0)
    def _():
        issue(dcur_ref, 0)

    has_next = i + 1 < n_steps
    even = (i % 2) == 0

    @pl.when(has_next & even)
    def _():
        issue(dnext_ref, 1)

    @pl.when(has_next & jnp.logical_not(even))
    def _():
        issue(dnext_ref, 0)

    @pl.when(even)
    def _():
        finish(0)

    @pl.when(jnp.logical_not(even))
    def _():
        finish(1)


def _combine(ys, dest, x2, mod_l, ln_g, ln_b, l, S, tm, alpha):
    T, D = x2.shape
    nS = S // tm
    n_steps = T // tm
    vec = pl.BlockSpec((1, 1, D), lambda i: (l, 0, 0))
    return pl.pallas_call(
        functools.partial(_combine_body, tm=tm, n_steps=n_steps, alpha=alpha),
        grid=(n_steps,),
        in_specs=[pl.BlockSpec((tm,), lambda i: (i,), memory_space=pltpu.SMEM),
                  pl.BlockSpec((tm,), lambda i: (jnp.minimum(i + 1, n_steps - 1),), memory_space=pltpu.SMEM),
                  pl.BlockSpec((tm, D), lambda i: (i, 0)),
                  pl.BlockSpec((1, 1, D), lambda i: (i // nS, 0, 5)),
                  vec, vec,
                  pl.BlockSpec(memory_space=pl.ANY)],
        out_specs=pl.BlockSpec((tm, D), lambda i: (i, 0)),
        out_shape=jax.ShapeDtypeStruct((T, D), F32),
        scratch_shapes=[pltpu.VMEM((tm, D), F32), pltpu.VMEM((tm, D), F32), pltpu.SemaphoreType.DMA((2,))],
        compiler_params=_cparams(("arbitrary",)),
        name="moe_combine_ln",
    )(dest, dest, x2, mod_l, ln_g, ln_b, ys)


def _prep_w_in(w_in):
    L, D, _ = w_in.shape
    o = 0
    cq = w_in[..., o:o + MLA_Q_RANK]; o += MLA_Q_RANK
    ckv = w_in[..., o:o + MLA_KV_RANK]; o += MLA_KV_RANK
    kr = w_in[..., o:o + MLA_ROPE]; o += MLA_ROPE
    gq = w_in[..., o:o + GDN_WIDTH]; o += GDN_WIDTH
    gk = w_in[..., o:o + GDN_WIDTH]; o += GDN_WIDTH
    gv = w_in[..., o:o + GDN_WIDTH]; o += GDN_WIDTH
    gz = w_in[..., o:o + GDN_WIDTH]; o += GDN_WIDTH
    ga = w_in[..., o:o + GDN_HEADS]; o += GDN_HEADS
    gb = w_in[..., o:o + GDN_HEADS]; o += GDN_HEADS
    pw = w_in[..., o:o + POOL_WIDTH]
    half = MLA_ROPE // 2
    z = lambda n: jnp.zeros((L, D, n), w_in.dtype)
    kra = jnp.concatenate([z(MLA_NOPE), kr, z(HEAD_PAD - MLA_NOPE - MLA_ROPE)], axis=-1)
    krb = jnp.concatenate([z(MLA_NOPE), kr[..., half:], kr[..., :half], z(HEAD_PAD - MLA_NOPE - MLA_ROPE)], axis=-1)
    a_e = jnp.repeat(ga, GDN_DIM, axis=-1)
    b_e = jnp.repeat(gb, GDN_DIM, axis=-1)
    out = jnp.concatenate([gq, gk, gv, gz, a_e, b_e, cq, pw, ckv, kra, krb], axis=-1)
    assert out.shape[-1] == IN_COLS
    return out.astype(BF16)


def _prep_mla(w_uq, w_ukv):
    L = w_uq.shape[0]
    H, half = MLA_HEADS, MLA_ROPE // 2
    pad = HEAD_PAD - MLA_NOPE - MLA_ROPE
    q = w_uq.reshape(L, MLA_Q_RANK, H, MLA_NOPE + MLA_ROPE)
    nope, r1, r2 = q[..., :MLA_NOPE], q[..., MLA_NOPE:MLA_NOPE + half], q[..., MLA_NOPE + half:]
    zq = lambda n: jnp.zeros((L, MLA_Q_RANK, H, n), w_uq.dtype)
    plain = jnp.concatenate([nope, r1, r2, zq(pad)], axis=-1).reshape(L, MLA_Q_RANK, H * HEAD_PAD)
    partner = jnp.concatenate([zq(MLA_NOPE), r2, r1, zq(pad)], axis=-1).reshape(L, MLA_Q_RANK, H * HEAD_PAD)
    wq2 = jnp.concatenate([plain, partner], axis=-1).astype(BF16)
    kv = w_ukv.reshape(L, MLA_KV_RANK, H, MLA_NOPE + MLA_V)
    k_nope, v = kv[..., :MLA_NOPE], kv[..., MLA_NOPE:]
    zk = lambda n: jnp.zeros((L, MLA_KV_RANK, H, n), w_ukv.dtype)
    k_main = jnp.concatenate([k_nope, zk(HEAD_PAD - MLA_NOPE)], axis=-1).reshape(L, MLA_KV_RANK, H * HEAD_PAD)
    even = (jnp.arange(H) % 2 == 0)[None, None, :, None]
    v_pair = jnp.where(even, jnp.concatenate([v, zk(MLA_V)], axis=-1), jnp.concatenate([zk(MLA_V), v], axis=-1))
    wkv2 = jnp.concatenate([k_main, v_pair.reshape(L, MLA_KV_RANK, H * HEAD_PAD)], axis=-1).astype(BF16)
    return wq2, wkv2


def _rope_tables(S):
    half = MLA_ROPE // 2
    inv_freq = jnp.power(ROPE_THETA, -jnp.arange(0, MLA_ROPE, 2, dtype=F32) / MLA_ROPE)
    ang = jnp.arange(S, dtype=F32)[:, None] * inv_freq[None, :]
    cos, sin = jnp.cos(ang), jnp.sin(ang)
    pad = jnp.zeros((S, HEAD_PAD - MLA_NOPE - MLA_ROPE), F32)
    cos_t = jnp.concatenate([jnp.ones((S, MLA_NOPE), F32), cos, cos, pad], axis=-1)
    sin_t = jnp.concatenate([jnp.zeros((S, MLA_NOPE), F32), -sin, sin, pad], axis=-1)
    scale = (MLA_NOPE + MLA_ROPE) ** -0.5
    return cos_t * scale, sin_t * scale, cos_t, sin_t


def _block_diag(blocks):
    L, G, n, _ = blocks.shape
    eye = jnp.eye(G, dtype=blocks.dtype)
    return jnp.einsum('lgij,gh->lgihj', blocks, eye).reshape(L, G * n, G * n)


def kernel(x, c, w_in, mla_q_norm, mla_kv_norm, mla_w_uq, mla_w_ukv, gdn_conv, gdn_a_log, gdn_dt_bias, gdn_out_norm, pool_w, pool_scale, w_out, w_mod, b_mod, ln1_g, ln1_b, ln2_g, ln2_b, router_w_group, router_b_group, router_w_expert, router_b_expert, moe_w_gate, moe_w_up, moe_w_down):
    B, S, D = x.shape
    L = w_in.shape[0]
    T = B * S
    alpha = (2 * L) ** 0.25
    ts = min(512, S)
    tq = min(256, S)
    t_moe = min(512, S)
    t_disp = min(1024, S)
    assert D == D_MODEL and S % ts == 0 and S % tq == 0 and ts % CHUNK == 0
    assert S % t_disp == 0 and t_moe % DMA_WAIT_UNROLL == 0

    w_in2 = _prep_w_in(w_in)
    wq2, wkv2 = _prep_mla(mla_w_uq, mla_w_ukv)
    tabs = _rope_tables(S)
    gq = mla_q_norm.reshape(L, 1, MLA_Q_RANK)
    gkv = mla_kv_norm.reshape(L, 1, MLA_KV_RANK)
    alog_e = jnp.repeat(gdn_a_log, GDN_DIM, axis=-1).reshape(L, 1, GDN_WIDTH)
    dt_e = jnp.repeat(gdn_dt_bias, GDN_DIM, axis=-1).reshape(L, 1, GDN_WIDTH)
    og_e = jnp.tile(gdn_out_norm, (1, GDN_HEADS)).reshape(L, 1, GDN_WIDTH)
    lane_head = jnp.arange(GDN_WIDTH) // GDN_DIM
    seg = (lane_head[:, None] == lane_head[None, :]).astype(BF16)
    tri_c = (jnp.arange(CHUNK)[:, None] >= jnp.arange(CHUNK)[None, :]).astype(BF16)
    pool_bd = _block_diag(pool_w).astype(BF16)
    pool_sc = pool_scale.reshape(L, 1, POOL_WIDTH)
    w_out_b = w_out.astype(BF16)
    w_r = jnp.concatenate([router_w_group, router_w_expert,
                           jnp.zeros((L, D, LANE - N_GROUPS - N_EXPERTS), F32)], axis=-1)
    w_r_hi = w_r.astype(BF16)
    w_r_lo = (w_r - w_r_hi.astype(F32)).astype(BF16)
    b_r = jnp.concatenate([router_b_group, router_b_expert,
                           jnp.zeros((L, LANE - N_GROUPS - N_EXPERTS), F32)], axis=-1).reshape(L, 1, LANE)
    tri_r = (jnp.arange(ts)[:, None] > jnp.arange(ts)[None, :]).astype(BF16)
    ln1g, ln1b = ln1_g.reshape(L, 1, D), ln1_b.reshape(L, 1, D)
    ln2g, ln2b = ln2_g.reshape(L, 1, D), ln2_b.reshape(L, 1, D)

    assert N_BUCKETS <= LANE
    nb = (T + N_BUCKETS * (MOE_ROWS - 1) + MOE_ROWS - 1) // MOE_ROWS
    P = nb * MOE_ROWS
    pg, pa, pb = [], [], []
    for g_ in range(N_GROUPS):
        for a_ in range(EXPERTS_PER_GROUP):
            for b_ in range(a_ + 1, EXPERTS_PER_GROUP):
                pg.append(g_)
                pa.append(a_)
                pb.append(b_)
    bucket_g = jnp.asarray(pg, jnp.int32)
    bucket_a = jnp.asarray(pa, jnp.int32)
    bucket_b = jnp.asarray(pb, jnp.int32)
    grouped = lambda w: w.astype(BF16).reshape((L, N_GROUPS, EXPERTS_PER_GROUP) + w.shape[2:])
    wg_b, wu_b, wd_b = grouped(moe_w_gate), grouped(moe_w_up), grouped(moe_w_down)
    bucket_ids = jnp.arange(N_BUCKETS, dtype=jnp.int32)
    blk0 = jnp.arange(nb, dtype=jnp.int32) * MOE_ROWS
    xs = jnp.zeros((P, D + LANE), F32)

    mod = _modulation(c, w_mod, b_mod)
    x2 = x.reshape(T, D)
    for l in range(L):
        mod_l = mod[l].reshape(B, 1, 6 * D)
        proj = _inproj(x2, mod_l, w_in2, l, S, ts)
        q, k, v = _mla_proj(proj, tabs, gq, gkv, wq2, wkv2, l, S, ts)
        y_mla = _attention(q, k, v, B, S, tq)
        qh, kh, vh, g, beta = _gdn_pre(proj, gdn_conv, alog_e, dt_e, seg, l, B, S, ts)
        y_gdn = _gdn(qh, kh, vh, g, beta, proj, og_e, seg, tri_c, l, B, S, ts)
        y_pool = _pool(proj, pool_bd, pool_sc, l, B, S, ts)
        x2 = _outproj(y_mla, y_gdn, y_pool, w_out_b, x2, mod_l, ln1g, ln1b, l, S, ts, alpha)

        hm, cnt = _router(x2, mod_l, w_r_hi, w_r_lo, b_r, tri_r, l, S, ts)
        counts = cnt[0, :N_BUCKETS].astype(jnp.int32)
        padded = (counts + MOE_ROWS - 1) // MOE_ROWS * MOE_ROWS
        pad_end = jnp.cumsum(padded)
        pad_start = pad_end - padded
        bucket = hm[:, D].astype(jnp.int32)
        rank = hm[:, D + 1].astype(jnp.int32)
        dest = jnp.sum(jnp.where(bucket[:, None] == bucket_ids[None, :], pad_start[None, :], 0), axis=1) + rank
        block_bucket = jnp.minimum(jnp.sum((pad_end[None, :] <= blk0[:, None]).astype(jnp.int32), axis=1),
                                   N_BUCKETS - 1)
        onehot_b = (block_bucket[:, None] == bucket_ids[None, :]).astype(jnp.int32)
        grp = jnp.sum(onehot_b * bucket_g[None, :], axis=1)
        ea = jnp.sum(onehot_b * bucket_a[None, :], axis=1)
        eb = jnp.sum(onehot_b * bucket_b[None, :], axis=1)
        used = (blk0 < pad_end[-1]).astype(jnp.int32)

        xs = _dispatch(hm, dest, xs, t_disp)
        ys = _ffn(xs, grp, ea, eb, used, wg_b, wu_b, wd_b, l)
        x2 = _combine(ys, dest, x2, mod_l, ln2g, ln2b, l, S, t_moe, alpha)
    return x2.reshape(B, S, D)
```

```python
import functools
import math

import jax
import jax.numpy as jnp
from jax import lax
from jax.experimental import pallas as pl
from jax.experimental.pallas import tpu as pltpu

F32 = jnp.float32
BF16 = jnp.bfloat16

D_MODEL = 1024
CHUNK = 64
MLA_HEADS = 8
MLA_NOPE = 64
MLA_ROPE = 32
MLA_V = 64
MLA_Q_RANK = 256
MLA_KV_RANK = 128
MLA_WIDTH = MLA_HEADS * MLA_V
ROPE_THETA = 10000.0
GDN_HEADS = 4
GDN_DIM = 64
GDN_WIDTH = GDN_HEADS * GDN_DIM
GDN_CONV = 4
POOL_WIDTH = 256
POOL_WINDOWS = (2, 4, 8, 16)
POOL_GROUP_DIM = 64
N_GROUPS = 4
EXPERTS_PER_GROUP = 8
N_EXPERTS = N_GROUPS * EXPERTS_PER_GROUP
TOP_K = 2
EXPERT_FF = 256
LN_EPS = 1e-5
RMS_EPS = 1e-6

LANE = 128
HEAD_PAD = 128
PAIRS_PER_GROUP = EXPERTS_PER_GROUP * (EXPERTS_PER_GROUP - 1) // 2
N_BUCKETS = N_GROUPS * PAIRS_PER_GROUP
MOE_ROWS = 128
VMEM_LIMIT = 48 * 1024 * 1024

_C_GQ, _C_GK, _C_GV, _C_GZ = 0, 256, 512, 768
_C_A, _C_B = 1024, 1280
_C_CQ, _C_POOL = 1536, 1792
_C_CKV, _C_KRA, _C_KRB = 2048, 2176, 2304
IN_COLS = 2432


def _cparams(sem):
    return pltpu.CompilerParams(dimension_semantics=sem, vmem_limit_bytes=VMEM_LIMIT)


def _sigmoid(x):
    return 1.0 / (1.0 + jnp.exp(-x))


def _split2(x):
    hi = x.astype(BF16)
    lo = (x - hi.astype(F32)).astype(BF16)
    return hi, lo


def _split3(x):
    hi = x.astype(BF16)
    r = x - hi.astype(F32)
    mid = r.astype(BF16)
    lo = (r - mid.astype(F32)).astype(BF16)
    return hi, mid, lo


def _dot(a, b):
    return jnp.dot(a, b, preferred_element_type=F32)


def _dot_nt(a, b):
    return lax.dot_general(a, b, (((1,), (1,)), ((), ())), preferred_element_type=F32)


def _dot_tn(a, b):
    return lax.dot_general(a, b, (((0,), (0,)), ((), ())), preferred_element_type=F32)


def _mod_body(c_ref, w_ref, b_ref, o_ref):
    c = c_ref[...]
    ca = c * _sigmoid(c)
    o_ref[0] = _dot(ca.astype(BF16), w_ref[0].astype(BF16)) + b_ref[0]


def _modulation(c, w_mod, b_mod):
    L, D, N = w_mod.shape
    B = c.shape[0]
    tn = 1024
    return pl.pallas_call(
        _mod_body,
        grid=(L, N // tn),
        in_specs=[pl.BlockSpec((B, D), lambda l, j: (0, 0)),
                  pl.BlockSpec((1, D, tn), lambda l, j: (l, 0, j)),
                  pl.BlockSpec((1, 1, tn), lambda l, j: (l, 0, j))],
        out_specs=pl.BlockSpec((1, B, tn), lambda l, j: (l, 0, j)),
        out_shape=jax.ShapeDtypeStruct((L, B, N), F32),
        compiler_params=_cparams(("arbitrary", "arbitrary")),
        name="modulation",
    )(c, w_mod, b_mod.reshape(L, 1, N))


def _inproj_body(x_ref, sh_ref, sc_ref, w_ref, o_ref):
    h = x_ref[...] * (1.0 + sc_ref[0]) + sh_ref[0]
    o_ref[...] = _dot(h.astype(BF16), w_ref[0])


def _inproj(x2, mod_l, w_in2, l, S, tm):
    T, D = x2.shape
    nS = S // tm
    return pl.pallas_call(
        _inproj_body,
        grid=(T // tm,),
        in_specs=[pl.BlockSpec((tm, D), lambda i: (i, 0)),
                  pl.BlockSpec((1, 1, D), lambda i: (i // nS, 0, 0)),
                  pl.BlockSpec((1, 1, D), lambda i: (i // nS, 0, 1)),
                  pl.BlockSpec((1, D, IN_COLS), lambda i: (l, 0, 0))],
        out_specs=pl.BlockSpec((tm, IN_COLS), lambda i: (i, 0)),
        out_shape=jax.ShapeDtypeStruct((T, IN_COLS), F32),
        compiler_params=_cparams(("arbitrary",)),
        name="inproj",
    )(x2, mod_l, mod_l, w_in2)


def _mla_proj_body(cq_ref, ckv_ref, kra_ref, krb_ref, cosq_ref, sinq_ref, cosk_ref, sink_ref,
                   gq_ref, gkv_ref, wq_ref, wkv_ref, q_out, k_out, v_out):
    cq = cq_ref[...]
    qn = cq * lax.rsqrt(jnp.mean(cq * cq, axis=-1, keepdims=True) + RMS_EPS) * gq_ref[0]
    q2 = _dot(qn.astype(BF16), wq_ref[0])
    ckv = ckv_ref[...]
    kvn = ckv * lax.rsqrt(jnp.mean(ckv * ckv, axis=-1, keepdims=True) + RMS_EPS) * gkv_ref[0]
    kv2 = _dot(kvn.astype(BF16), wkv_ref[0])
    cq_t, sq_t = cosq_ref[...], sinq_ref[...]
    krope = kra_ref[...] * cosk_ref[...] + krb_ref[...] * sink_ref[...]
    hw = MLA_HEADS * HEAD_PAD
    for h in range(MLA_HEADS):
        a, b = h * HEAD_PAD, (h + 1) * HEAD_PAD
        q_out[:, a:b] = (q2[:, a:b] * cq_t + q2[:, hw + a:hw + b] * sq_t).astype(BF16)
        k_out[:, a:b] = (kv2[:, a:b] + krope).astype(BF16)
    vl = lax.broadcasted_iota(jnp.int32, (1, hw), 1)
    ones_lane = jnp.where((vl // HEAD_PAD) % 2 == 0, MLA_V, 0)
    v_out[...] = (kv2[:, hw:] + (vl % HEAD_PAD == ones_lane).astype(F32)).astype(BF16)


def _mla_proj(proj, tabs, gq, gkv, wq2, wkv2, l, S, ts):
    T = proj.shape[0]
    nS = S // ts
    hw = MLA_HEADS * HEAD_PAD
    tab_spec = pl.BlockSpec((ts, LANE), lambda i: (i % nS, 0))
    return pl.pallas_call(
        _mla_proj_body,
        grid=(T // ts,),
        in_specs=[pl.BlockSpec((ts, 256), lambda i: (i, _C_CQ // 256)),
                  pl.BlockSpec((ts, 128), lambda i: (i, _C_CKV // 128)),
                  pl.BlockSpec((ts, 128), lambda i: (i, _C_KRA // 128)),
                  pl.BlockSpec((ts, 128), lambda i: (i, _C_KRB // 128)),
                  tab_spec, tab_spec, tab_spec, tab_spec,
                  pl.BlockSpec((1, 1, MLA_Q_RANK), lambda i: (l, 0, 0)),
                  pl.BlockSpec((1, 1, MLA_KV_RANK), lambda i: (l, 0, 0)),
                  pl.BlockSpec((1, MLA_Q_RANK, 2 * hw), lambda i: (l, 0, 0)),
                  pl.BlockSpec((1, MLA_KV_RANK, 2 * hw), lambda i: (l, 0, 0))],
        out_specs=[pl.BlockSpec((ts, hw), lambda i: (i, 0))] * 3,
        out_shape=[jax.ShapeDtypeStruct((T, hw), BF16)] * 3,
        compiler_params=_cparams(("arbitrary",)),
        name="mla_proj",
    )(proj, proj, proj, proj, *tabs, gq, gkv, wq2, wkv2)


def _attn_body(qi_tab, kj_tab, q_ref, k_ref, v_ref, o_ref, m_s, acc_s, *, tq, n_q, n_tiles):
    row_c = lax.broadcasted_iota(jnp.int32, (tq, tq), 0) // CHUNK
    col_c = lax.broadcasted_iota(jnp.int32, (tq, tq), 1) // CHUNK
    allowed = col_c <= row_c
    lane = lax.broadcasted_iota(jnp.int32, (tq, HEAD_PAD), 1)
    sl = [slice(hh * HEAD_PAD, (hh + 1) * HEAD_PAD) for hh in range(2)]
    hs = range(2)

    def rows_of(tile):
        return pl.ds(pl.multiple_of(tile * tq, tq), tq)

    def stage_a(t):
        q = q_ref[rows_of(qi_tab[t]), :]
        kk = k_ref[rows_of(kj_tab[t]), :]
        return tuple(_dot_nt(q[:, sl[hh]], kk[:, sl[hh]]).astype(BF16) for hh in hs)

    def stage_b(t, s_pair):
        qi, kj = qi_tab[t], kj_tab[t]
        rows = rows_of(qi)
        keep = jnp.logical_or(allowed, kj < qi)
        neg = jnp.asarray(-jnp.inf, BF16)
        s = [jnp.where(keep, s_pair[hh], neg) for hh in hs]
        m_prev = [m_s[hh, rows, :] for hh in hs]
        m_new = [jnp.maximum(m_prev[hh], jnp.max(s[hh], axis=-1, keepdims=True)) for hh in hs]
        p = tuple(jnp.exp(s[hh] - jnp.concatenate([m_new[hh]] * (tq // HEAD_PAD), axis=1)) for hh in hs)
        alpha = tuple(jnp.exp(m_prev[hh] - m_new[hh]) for hh in hs)
        for hh in hs:
            m_s[hh, rows, :] = m_new[hh]
        return p, alpha

    def stage_c(t, p, alpha):
        qi, kj = qi_tab[t], kj_tab[t]
        rows = rows_of(qi)
        vv = v_ref[rows_of(kj), :]
        pv = [_dot(p[hh], vv[:, sl[hh]]) for hh in hs]
        for hh in hs:
            acc_s[hh, rows, :] = alpha[hh].astype(F32) * acc_s[hh, rows, :] + pv[hh]

    def body(t, c):
        s_next, (p, alpha) = c
        s_after = stage_a(t + 2)
        stage_c(t, p, alpha)
        return s_after, stage_b(t + 1, s_next)

    m_s[...] = jnp.full(m_s.shape, -jnp.inf, m_s.dtype)
    acc_s[...] = jnp.zeros_like(acc_s)
    pb0 = stage_b(0, stage_a(0))
    s_last, pb = lax.fori_loop(0, n_tiles - 2, body, (stage_a(1), pb0), unroll=ATTN_UNROLL)
    pb_last = stage_b(n_tiles - 1, s_last)
    stage_c(n_tiles - 2, *pb)
    stage_c(n_tiles - 1, *pb_last)

    def normalize(i, _):
        rows = rows_of(i)
        acc0, acc1 = acc_s[0, rows, :], acc_s[1, rows, :]
        o0 = jnp.where(lane < MLA_V, acc0 * (1.0 / acc0[:, MLA_V:MLA_V + 1]), 0.0)
        o1 = jnp.where(lane >= MLA_V, acc1 * (1.0 / acc1[:, 0:1]), 0.0)
        o_ref[rows, :] = (o0 + o1).astype(BF16)
        return 0

    lax.fori_loop(0, n_q, normalize, 0)


def _attention(q, k, v, B, S, tq):
    T = q.shape[0]
    nq = S // tq
    pairs = MLA_HEADS // 2
    tiles = [(i, j) for i in range(nq) for j in range(i + 1)]
    assert len(tiles) >= 2
    qi_tab = jnp.asarray([t[0] for t in tiles], jnp.int32)
    kj_tab = jnp.asarray([t[1] for t in tiles], jnp.int32)
    seq = pl.BlockSpec((S, 2 * HEAD_PAD), lambda b, p, qt, kt: (b, p))
    grid_spec = pltpu.PrefetchScalarGridSpec(
        num_scalar_prefetch=2,
        grid=(B, pairs),
        in_specs=[seq, seq, seq],
        out_specs=pl.BlockSpec((S, 2 * MLA_V), lambda b, p, qt, kt: (b, p)),
        scratch_shapes=[pltpu.VMEM((2, S, HEAD_PAD), BF16), pltpu.VMEM((2, S, HEAD_PAD), F32)],
    )
    return pl.pallas_call(
        functools.partial(_attn_body, tq=tq, n_q=nq, n_tiles=len(tiles)),
        grid_spec=grid_spec,
        out_shape=jax.ShapeDtypeStruct((T, MLA_WIDTH), BF16),
        compiler_params=_cparams(("arbitrary", "arbitrary")),
        name="mla_attention",
    )(qi_tab, kj_tab, q, k, v)


def _pool_body(p_ref, w_ref, sc_ref, o_ref, prev_ref, *, ts):
    j = pl.program_id(1)
    hist = 16

    @pl.when(j == 0)
    def _():
        prev_ref[...] = jnp.zeros_like(prev_ref)

    cur = p_ref[...]
    x = jnp.concatenate([prev_ref[...], cur], axis=0)
    s1 = x + pltpu.roll(x, 1, axis=0)
    s2 = s1 + pltpu.roll(s1, 2, axis=0)
    s4 = s2 + pltpu.roll(s2, 4, axis=0)
    s8 = s4 + pltpu.roll(s4, 8, axis=0)
    pos = (j * ts + lax.broadcasted_iota(jnp.int32, (ts, POOL_WIDTH), 0) + 1).astype(F32)
    lane = lax.broadcasted_iota(jnp.int32, (ts, POOL_WIDTH), 1)
    sums = (s1, s2, s4, s8)
    mean = None
    for gi, win in enumerate(POOL_WINDOWS):
        m_g = sums[gi][hist:] / jnp.minimum(pos, float(win))
        mean = m_g if mean is None else jnp.where(lane >= gi * POOL_GROUP_DIM, m_g, mean)
    delta = mean - cur
    o_ref[...] = (_dot(delta.astype(BF16), w_ref[0]) * sc_ref[0]).astype(BF16)
    prev_ref[...] = cur[ts - hist:]


def _pool(proj, w_bd, scale, l, B, S, ts):
    T = proj.shape[0]
    nS = S // ts
    return pl.pallas_call(
        functools.partial(_pool_body, ts=ts),
        grid=(B, nS),
        in_specs=[pl.BlockSpec((ts, POOL_WIDTH), lambda b, j: (b * nS + j, _C_POOL // 256)),
                  pl.BlockSpec((1, POOL_WIDTH, POOL_WIDTH), lambda b, j: (l, 0, 0)),
                  pl.BlockSpec((1, 1, POOL_WIDTH), lambda b, j: (l, 0, 0))],
        out_specs=pl.BlockSpec((ts, POOL_WIDTH), lambda b, j: (b * nS + j, 0)),
        out_shape=jax.ShapeDtypeStruct((T, POOL_WIDTH), BF16),
        scratch_shapes=[pltpu.VMEM((16, POOL_WIDTH), F32)],
        compiler_params=_cparams(("arbitrary", "arbitrary")),
        name="pool_mixer",
    )(proj, w_bd, scale)


def _head_sum(x, seg_ref):
    hi, lo = _split2(x)
    return _dot(hi, seg_ref[...]) + _dot(lo, seg_ref[...])


def _gdn_pre_body(q_ref, k_ref, v_ref, a_ref, b_ref, conv_ref, alog_ref, dt_ref, seg_ref,
                  qo_ref, ko_ref, vo_ref, go_ref, bo_ref, prev_ref, *, ts):
    j = pl.program_id(1)
    hist = 8

    @pl.when(j == 0)
    def _():
        prev_ref[...] = jnp.zeros_like(prev_ref)

    cw = conv_ref[0]
    outs = []
    for idx, ref in enumerate((q_ref, k_ref, v_ref)):
        cur = ref[...]
        a, b = idx * GDN_WIDTH, (idx + 1) * GDN_WIDTH
        x = jnp.concatenate([prev_ref[:, a:b], cur], axis=0)
        w = cw[:, a:b]
        y = (w[3:4] * x + w[2:3] * pltpu.roll(x, 1, axis=0) + w[1:2] * pltpu.roll(x, 2, axis=0)
             + w[0:1] * pltpu.roll(x, 3, axis=0))[hist:]
        outs.append(y * _sigmoid(y))
        prev_ref[:, a:b] = cur[ts - hist:]
    qc, kc, vc = outs
    qo_ref[...] = qc * lax.rsqrt(_head_sum(qc * qc, seg_ref) + RMS_EPS) * (GDN_DIM ** -0.5)
    ko_ref[...] = kc * lax.rsqrt(_head_sum(kc * kc, seg_ref) + RMS_EPS)
    vo_ref[...] = vc
    z = a_ref[...] + dt_ref[0]
    softplus = jnp.maximum(z, 0.0) + jnp.log(1.0 + jnp.exp(-jnp.abs(z)))
    go_ref[...] = -jnp.exp(alog_ref[0]) * softplus
    bo_ref[...] = _sigmoid(b_ref[...])


def _gdn_pre(proj, conv_w, alog_e, dt_e, seg, l, B, S, ts):
    T = proj.shape[0]
    nS = S // ts
    W = GDN_WIDTH

    def col(c):
        return pl.BlockSpec((ts, W), lambda b, j: (b * nS + j, c // W))

    return pl.pallas_call(
        functools.partial(_gdn_pre_body, ts=ts),
        grid=(B, nS),
        in_specs=[col(_C_GQ), col(_C_GK), col(_C_GV), col(_C_A), col(_C_B),
                  pl.BlockSpec((1, GDN_CONV, 3 * W), lambda b, j: (l, 0, 0)),
                  pl.BlockSpec((1, 1, W), lambda b, j: (l, 0, 0)),
                  pl.BlockSpec((1, 1, W), lambda b, j: (l, 0, 0)),
                  pl.BlockSpec((W, W), lambda b, j: (0, 0))],
        out_specs=[pl.BlockSpec((ts, W), lambda b, j: (b * nS + j, 0))] * 5,
        out_shape=[jax.ShapeDtypeStruct((T, W), F32)] * 5,
        scratch_shapes=[pltpu.VMEM((8, 3 * W), F32)],
        compiler_params=_cparams(("arbitrary", "arbitrary")),
        name="gdn_pre",
    )(proj, proj, proj, proj, proj, conv_w, alog_e, dt_e, seg)


def _gdn_body(q_ref, k_ref, v_ref, g_ref, b_ref, z_ref, og_ref, seg_ref, tri_ref,
              o_ref, state_ref, u_s, w_s, qk_s, qd_s, kd_s, gl_s, *, ts):
    j = pl.program_id(1)
    C, W, H = CHUNK, GDN_WIDTH, GDN_HEADS
    n_chunks = ts // C
    groups = n_chunks // INTRA_UNROLL

    @pl.when(j == 0)
    def _():
        state_ref[...] = jnp.zeros_like(state_ref)

    lane = lax.broadcasted_iota(jnp.int32, (C, W), 1)
    row = lax.broadcasted_iota(jnp.int32, (C, W), 0)
    col_tok = lane % C
    incl = row >= col_tok
    strict = row > col_tok
    eye = (row == col_tok).astype(F32)
    head_masks = [(lane // C == h).astype(F32) for h in range(H)]
    bd_mask = (lax.broadcasted_iota(jnp.int32, (W, W), 0) // C
               == lax.broadcasted_iota(jnp.int32, (W, W), 1) // C)

    def expand(m):
        return jnp.concatenate([m * hm for hm in head_masks], axis=0)

    def bdot(lhs, rhs):
        return _dot(lhs.astype(BF16), rhs.astype(BF16))

    def intra(i, _):
        sq = i // groups
        cs = [(i % groups) * INTRA_UNROLL + uu for uu in range(INTRA_UNROLL)]
        r0s = [pl.multiple_of(c * C, C) for c in cs]
        n = range(INTRA_UNROLL)
        tri = tri_ref[...]
        gs = [_split3(g_ref[sq, pl.ds(r0, C), :]) for r0 in r0s]
        gc = [_dot(tri, g[0]) + _dot(tri, g[1]) + _dot(tri, g[2]) for g in gs]
        ks = [k_ref[sq, pl.ds(r0, C), :] for r0 in r0s]
        qs = [q_ref[sq, pl.ds(r0, C), :] for r0 in r0s]
        betas = [b_ref[sq, pl.ds(r0, C), :] for r0 in r0s]
        kb = [ks[x] * betas[x] for x in n]
        aq = [_dot_nt(jnp.concatenate([kb[x], qs[x]], axis=0).astype(BF16), expand(ks[x]).astype(BF16)) for x in n]
        g_row = [jnp.sum(gc[x] * eye, axis=0, keepdims=True) for x in n]
        g_last = [gc[x][C - 1:C, :] for x in n]
        decay = [jnp.where(incl, jnp.exp(jnp.where(incl, gc[x] - g_row[x], 0.0)), 0.0) for x in n]
        e_gc = [jnp.exp(gc[x]) for x in n]
        a_cat = [jnp.where(strict, aq[x][:C] * decay[x], 0.0) for x in n]
        for x in n:
            r0 = r0s[x]
            qk_s[sq, pl.ds(r0, C), :] = jnp.where(incl, aq[x][C:] * decay[x], 0.0).astype(BF16)
            qd_s[sq, pl.ds(r0, C), :] = (qs[x] * e_gc[x]).astype(BF16)
            kd_s[sq, pl.ds(r0, C), :] = (ks[x] * jnp.exp(g_last[x] - gc[x])).astype(BF16)
            gl_s[sq, pl.ds(pl.multiple_of(cs[x] * 8, 8), 8), :] = jnp.broadcast_to(jnp.exp(g_last[x]), (8, W))
        x_cat = [eye - a_cat[x] for x in n]
        p_cat = [bdot(a_cat[x], expand(a_cat[x])) for x in n]
        n_fac = int(math.log2(C)) - 1
        for r in range(n_fac):
            last = r == n_fac - 1
            xp = [bdot(x_cat[x] if last else jnp.concatenate([x_cat[x], p_cat[x]], axis=0), expand(p_cat[x]))
                  for x in n]
            x_cat = [x_cat[x] + xp[x][:C] for x in n]
            if not last:
                p_cat = [xp[x][C:] for x in n]
        for x in n:
            r0 = r0s[x]
            t_cat = x_cat[x].astype(BF16)
            v = v_ref[sq, pl.ds(r0, C), :]
            u_s[sq, pl.ds(r0, C), :] = _dot(t_cat, expand(v * betas[x]).astype(BF16))
            w_s[sq, pl.ds(r0, C), :] = _dot(t_cat, expand(kb[x] * e_gc[x]).astype(BF16)).astype(BF16)
        return 0

    lax.fori_loop(0, GDN_SEQS * groups, intra, 0)

    def scan(c, _):
        r0 = pl.multiple_of(c * C, C)
        sq = range(GDN_SEQS)
        state = [state_ref[s] for s in sq]
        wq = [_dot(jnp.concatenate([w_s[s, pl.ds(r0, C), :], qd_s[s, pl.ds(r0, C), :]], axis=0),
                   state[s].astype(BF16)) for s in sq]
        v_new = [u_s[s, pl.ds(r0, C), :] - wq[s][:C] for s in sq]
        upd = [_dot_tn(kd_s[s, pl.ds(r0, C), :], v_new[s].astype(BF16)) for s in sq]
        for s in sq:
            g_l = gl_s[s, pl.ds(pl.multiple_of(c * 8, 8), 1), :]
            state_ref[s] = state[s] * g_l + jnp.where(bd_mask, upd[s], 0.0)
        o = [wq[s][C:] + _dot(qk_s[s, pl.ds(r0, C), :], expand(v_new[s]).astype(BF16)) for s in sq]
        ms = [_head_sum(o[s] * o[s], seg_ref) * (1.0 / GDN_DIM) for s in sq]
        for s in sq:
            z = z_ref[s, pl.ds(r0, C), :]
            y = o[s] * lax.rsqrt(ms[s] + RMS_EPS) * og_ref[0] * (z * _sigmoid(z))
            o_ref[s, pl.ds(r0, C), :] = y.astype(BF16)
        return 0

    lax.fori_loop(0, n_chunks, scan, 0, unroll=2)


def _gdn(qh, kh, vh, g, beta, proj, og, seg, tri, l, B, S, ts):
    T = qh.shape[0]
    nS = S // ts
    W = GDN_WIDTH
    Q = GDN_SEQS
    seq3 = lambda a: a.reshape(B, S, a.shape[-1])
    row = pl.BlockSpec((Q, ts, W), lambda b, j: (b, j, 0))
    out = pl.pallas_call(
        functools.partial(_gdn_body, ts=ts),
        grid=(B // Q, nS),
        in_specs=[row, row, row, row, row,
                  pl.BlockSpec((Q, ts, W), lambda b, j: (b, j, _C_GZ // W)),
                  pl.BlockSpec((1, 1, W), lambda b, j: (l, 0, 0)),
                  pl.BlockSpec((W, W), lambda b, j: (0, 0)),
                  pl.BlockSpec((CHUNK, CHUNK), lambda b, j: (0, 0))],
        out_specs=pl.BlockSpec((Q, ts, W), lambda b, j: (b, j, 0)),
        out_shape=jax.ShapeDtypeStruct((B, S, W), BF16),
        scratch_shapes=[pltpu.VMEM((Q, W, W), F32), pltpu.VMEM((Q, ts, W), F32)]
        + [pltpu.VMEM((Q, ts, W), BF16)] * 4 + [pltpu.VMEM((Q, 8 * (ts // CHUNK), W), F32)],
        compiler_params=_cparams(("arbitrary", "arbitrary")),
        name="gdn_delta_rule",
    )(seq3(qh), seq3(kh), seq3(vh), seq3(g), seq3(beta), seq3(proj), og, seg, tri)
    return out.reshape(T, W)


def _layer_norm(r, g, b):
    mu = jnp.mean(r, axis=-1, keepdims=True)
    d = r - mu
    var = jnp.mean(d * d, axis=-1, keepdims=True)
    return d * lax.rsqrt(var + LN_EPS) * g + b


def _outproj_body(ym_ref, yg_ref, yp_ref, w_ref, x_ref, gt_ref, lg_ref, lb_ref, o_ref, *, alpha):
    w = w_ref[0]
    y = (_dot(ym_ref[...], w[:MLA_WIDTH]) + _dot(yg_ref[...], w[MLA_WIDTH:MLA_WIDTH + GDN_WIDTH])
         + _dot(yp_ref[...], w[MLA_WIDTH + GDN_WIDTH:]))
    r = alpha * x_ref[...] + (1.0 + gt_ref[0]) * y
    o_ref[...] = _layer_norm(r, lg_ref[0], lb_ref[0])


def _outproj(y_mla, y_gdn, y_pool, w_out, x2, mod_l, ln_g, ln_b, l, S, tm, alpha):
    T, D = x2.shape
    nS = S // tm
    vec = pl.BlockSpec((1, 1, D), lambda i: (l, 0, 0))
    return pl.pallas_call(
        functools.partial(_outproj_body, alpha=alpha),
        grid=(T // tm,),
        in_specs=[pl.BlockSpec((tm, MLA_WIDTH), lambda i: (i, 0)),
                  pl.BlockSpec((tm, GDN_WIDTH), lambda i: (i, 0)),
                  pl.BlockSpec((tm, POOL_WIDTH), lambda i: (i, 0)),
                  pl.BlockSpec((1, D, D), lambda i: (l, 0, 0)),
                  pl.BlockSpec((tm, D), lambda i: (i, 0)),
                  pl.BlockSpec((1, 1, D), lambda i: (i // nS, 0, 2)),
                  vec, vec],
        out_specs=pl.BlockSpec((tm, D), lambda i: (i, 0)),
        out_shape=jax.ShapeDtypeStruct((T, D), F32),
        compiler_params=_cparams(("arbitrary",)),
        name="outproj_ln",
    )(y_mla, y_gdn, y_pool, w_out, x2, mod_l, ln_g, ln_b)


def _lane_first(cond, lane_f):
    return jnp.min(jnp.where(cond, lane_f, float(LANE)), axis=-1, keepdims=True)


def _router_body(x_ref, sh_ref, sc_ref, whi_ref, wlo_ref, br_ref, tri_ref,
                 hm_out, cnt_out, carry_ref):
    i = pl.program_id(0)

    @pl.when(i == 0)
    def _():
        carry_ref[...] = jnp.zeros_like(carry_ref)

    h = x_ref[...] * (1.0 + sc_ref[0]) + sh_ref[0]
    hm_out[:, :D_MODEL] = h
    h_hi, h_lo = _split2(h)
    logits = _dot(h_hi, whi_ref[0]) + _dot(h_lo, whi_ref[0]) + _dot(h_hi, wlo_ref[0]) + br_ref[0]
    tm = logits.shape[0]
    lane = lax.broadcasted_iota(jnp.int32, (tm, LANE), 1)
    lane_f = lane.astype(F32)
    neg = -jnp.inf
    gl = jnp.where(lane < N_GROUPS, logits, neg)
    gmax = jnp.max(gl, axis=-1, keepdims=True)
    gsel = _lane_first(gl == gmax, lane_f)
    g_p = 1.0 / jnp.sum(jnp.exp(gl - gmax), axis=-1, keepdims=True)
    lo = N_GROUPS + EXPERTS_PER_GROUP * gsel
    el = jnp.where((lane_f >= lo) & (lane_f < lo + EXPERTS_PER_GROUP), logits, neg)
    m1 = jnp.max(el, axis=-1, keepdims=True)
    i1 = _lane_first(el == m1, lane_f)
    el2 = jnp.where(lane_f == i1, neg, el)
    m2 = jnp.max(el2, axis=-1, keepdims=True)
    i2 = _lane_first(el2 == m2, lane_f)
    t = jnp.exp(m2 - m1)
    w1 = g_p / (1.0 + t)
    w2 = g_p * t / (1.0 + t)
    loc1 = i1 - lo
    loc2 = i2 - lo
    a_loc = jnp.minimum(loc1, loc2)
    b_loc = jnp.maximum(loc1, loc2)
    pair = a_loc * (2 * EXPERTS_PER_GROUP - 1 - a_loc) * 0.5 + (b_loc - a_loc - 1.0)
    bucket = gsel * PAIRS_PER_GROUP + pair
    first_is_a = loc1 < loc2
    w_a = jnp.where(first_is_a, w1, w2)
    w_b = jnp.where(first_is_a, w2, w1)
    hit = lane_f == bucket
    onehot = hit.astype(BF16)
    before = _dot(tri_ref[...], onehot) + carry_ref[0:1, :]
    rank = jnp.sum(jnp.where(hit, before, 0.0), axis=-1, keepdims=True)
    total = carry_ref[0:1, :] + jnp.sum(onehot.astype(F32), axis=0, keepdims=True)
    carry_ref[...] = jnp.broadcast_to(total, carry_ref.shape)
    cnt_out[...] = jnp.broadcast_to(total, cnt_out.shape)
    meta = jnp.zeros((tm, LANE), F32)
    for idx, val in enumerate((bucket, rank, w_a, w_b)):
        meta = jnp.where(lane == idx, val, meta)
    hm_out[:, D_MODEL:] = meta


def _router(x2, mod_l, w_hi, w_lo, b_r, tri, l, S, tm):
    T, D = x2.shape
    nS = S // tm
    return pl.pallas_call(
        _router_body,
        grid=(T // tm,),
        in_specs=[pl.BlockSpec((tm, D), lambda i: (i, 0)),
                  pl.BlockSpec((1, 1, D), lambda i: (i // nS, 0, 3)),
                  pl.BlockSpec((1, 1, D), lambda i: (i // nS, 0, 4)),
                  pl.BlockSpec((1, D, LANE), lambda i: (l, 0, 0)),
                  pl.BlockSpec((1, D, LANE), lambda i: (l, 0, 0)),
                  pl.BlockSpec((1, 1, LANE), lambda i: (l, 0, 0)),
                  pl.BlockSpec((tm, tm), lambda i: (0, 0))],
        out_specs=[pl.BlockSpec((tm, D + LANE), lambda i: (i, 0)),
                   pl.BlockSpec((8, LANE), lambda i: (0, 0))],
        out_shape=[jax.ShapeDtypeStruct((T, D + LANE), F32),
                   jax.ShapeDtypeStruct((8, LANE), F32)],
        scratch_shapes=[pltpu.VMEM((8, LANE), F32)],
        compiler_params=_cparams(("arbitrary",)),
        name="router",
    )(x2, mod_l, mod_l, w_hi, w_lo, b_r, tri)


def _row_copy(src, s, dst, d, sem):
    return pltpu.make_async_copy(src.at[pl.ds(s, 1)], dst.at[pl.ds(d, 1)], sem)


DMA_WAIT_UNROLL = 32
INTRA_UNROLL = 4
GDN_SEQS = 2
ATTN_UNROLL = 4
FFN_BLOCKS = 2


def _issue_rows(n, make_copy):
    def body(r, _):
        make_copy(r).start()
        return 0

    lax.fori_loop(0, n, body, 0, unroll=8)


def _wait_rows(n, make_copy):
    def body(_, c):
        for _u in range(DMA_WAIT_UNROLL):
            make_copy(0).wait()
        return c

    lax.fori_loop(0, n // DMA_WAIT_UNROLL, body, 0)


def _dispatch_body(dest_ref, hm_ref, xs_in_ref, xs_ref, sem, *, tm):
    del xs_in_ref
    copy = lambda r: _row_copy(hm_ref, r, xs_ref, dest_ref[r], sem)
    _issue_rows(tm, copy)
    _wait_rows(tm, lambda r: _row_copy(hm_ref, 0, xs_ref, 0, sem))


def _dispatch(hm, dest, xs_buf, tm):
    T, W = hm.shape
    return pl.pallas_call(
        functools.partial(_dispatch_body, tm=tm),
        grid=(T // tm,),
        in_specs=[pl.BlockSpec((tm,), lambda i: (i,), memory_space=pltpu.SMEM),
                  pl.BlockSpec((tm, W), lambda i: (i, 0)),
                  pl.BlockSpec(memory_space=pl.ANY)],
        out_specs=pl.BlockSpec(memory_space=pl.ANY),
        out_shape=jax.ShapeDtypeStruct(xs_buf.shape, F32),
        input_output_aliases={2: 0},
        scratch_shapes=[pltpu.SemaphoreType.DMA],
        compiler_params=_cparams(("arbitrary",)),
        name="moe_dispatch",
    )(dest, hm, xs_buf)


def _ffn_body(grp_ref, ea_ref, eb_ref, used_ref, xs_ref, wg_ref, wu_ref, wd_ref, o_ref):
    i = pl.program_id(0)
    subs = range(FFN_BLOCKS)
    first = FFN_BLOCKS * i

    @pl.when(used_ref[first] != 0)
    def _():
        a = [ea_ref[first + s] for s in subs]
        b = [eb_ref[first + s] for s in subs]
        xm = [xs_ref[s * MOE_ROWS:(s + 1) * MOE_ROWS, :] for s in subs]
        x = [xm[s][:, :D_MODEL].astype(BF16) for s in subs]
        g_a = [_dot(x[s], wg_ref[0, 0, a[s]]) for s in subs]
        u_a = [_dot(x[s], wu_ref[0, 0, a[s]]) for s in subs]
        g_b = [_dot(x[s], wg_ref[0, 0, b[s]]) for s in subs]
        u_b = [_dot(x[s], wu_ref[0, 0, b[s]]) for s in subs]
        w_a = [xm[s][:, D_MODEL + 2:D_MODEL + 3] for s in subs]
        w_b = [xm[s][:, D_MODEL + 3:D_MODEL + 4] for s in subs]
        act_a = [(g_a[s] * _sigmoid(g_a[s]) * u_a[s] * w_a[s]).astype(BF16) for s in subs]
        act_b = [(g_b[s] * _sigmoid(g_b[s]) * u_b[s] * w_b[s]).astype(BF16) for s in subs]
        for s in subs:
            o_ref[s * MOE_ROWS:(s + 1) * MOE_ROWS, :] = (_dot(act_a[s], wd_ref[0, 0, a[s]])
                                                         + _dot(act_b[s], wd_ref[0, 0, b[s]]))

    @pl.when(used_ref[first] == 0)
    def _():
        o_ref[...] = jnp.zeros_like(o_ref)


def _ffn(xs, grp, ea, eb, used, w_gate, w_up, w_down, l):
    P, W = xs.shape
    D = D_MODEL
    rows = MOE_ROWS * FFN_BLOCKS
    nb = P // rows
    FF = EXPERT_FF
    E = EXPERTS_PER_GROUP
    w_up_spec = pl.BlockSpec((1, 1, E, D, FF), lambda i, g, ea, eb, u: (l, g[FFN_BLOCKS * i], 0, 0, 0))
    w_dn_spec = pl.BlockSpec((1, 1, E, FF, D), lambda i, g, ea, eb, u: (l, g[FFN_BLOCKS * i], 0, 0, 0))
    grid_spec = pltpu.PrefetchScalarGridSpec(
        num_scalar_prefetch=4,
        grid=(nb,),
        in_specs=[pl.BlockSpec((rows, W), lambda i, g, ea, eb, u: (i, 0)),
                  w_up_spec, w_up_spec, w_dn_spec],
        out_specs=pl.BlockSpec((rows, D), lambda i, g, ea, eb, u: (i, 0)),
    )
    return pl.pallas_call(
        _ffn_body,
        grid_spec=grid_spec,
        out_shape=jax.ShapeDtypeStruct((P, D), F32),
        compiler_params=_cparams(("arbitrary",)),
        name="moe_ffn",
    )(grp, ea, eb, used, xs, w_gate, w_up, w_down)


def _combine_body(dcur_ref, dnext_ref, x_ref, gt_ref, lg_ref, lb_ref, ys_ref, o_ref,
                  y0_ref, y1_ref, sems, *, tm, n_steps, alpha):
    i = pl.program_id(0)
    bufs = (y0_ref, y1_ref)

    def issue(dref, slot):
        _issue_rows(tm, lambda r: _row_copy(ys_ref, dref[r], bufs[slot], r, sems.at[slot]))

    def finish(slot):
        _wait_rows(tm, lambda r: _row_copy(ys_ref, 0, bufs[slot], 0, sems.at[slot]))
        r = alpha * x_ref[...] + (1.0 + gt_ref[0]) * bufs[slot][...]
        o_ref[...] = _layer_norm(r, lg_ref[0], lb_ref[0])

    @pl.when(i == 0)
    def _():
        issue(dcur_ref, 0)

    has_next = i + 1 < n_steps
    even = (i % 2) == 0

    @pl.when(has_next & even)
    def _():
        issue(dnext_ref, 1)

    @pl.when(has_next & jnp.logical_not(even))
    def _():
        issue(dnext_ref, 0)

    @pl.when(even)
    def _():
        finish(0)

    @pl.when(jnp.logical_not(even))
    def _():
        finish(1)


def _combine(ys, dest, x2, mod_l, ln_g, ln_b, l, S, tm, alpha):
    T, D = x2.shape
    nS = S // tm
    n_steps = T // tm
    vec = pl.BlockSpec((1, 1, D), lambda i: (l, 0, 0))
    return pl.pallas_call(
        functools.partial(_combine_body, tm=tm, n_steps=n_steps, alpha=alpha),
        grid=(n_steps,),
        in_specs=[pl.BlockSpec((tm,), lambda i: (i,), memory_space=pltpu.SMEM),
                  pl.BlockSpec((tm,), lambda i: (jnp.minimum(i + 1, n_steps - 1),), memory_space=pltpu.SMEM),
                  pl.BlockSpec((tm, D), lambda i: (i, 0)),
                  pl.BlockSpec((1, 1, D), lambda i: (i // nS, 0, 5)),
                  vec, vec,
                  pl.BlockSpec(memory_space=pl.ANY)],
        out_specs=pl.BlockSpec((tm, D), lambda i: (i, 0)),
        out_shape=jax.ShapeDtypeStruct((T, D), F32),
        scratch_shapes=[pltpu.VMEM((tm, D), F32), pltpu.VMEM((tm, D), F32), pltpu.SemaphoreType.DMA((2,))],
        compiler_params=_cparams(("arbitrary",)),
        name="moe_combine_ln",
    )(dest, dest, x2, mod_l, ln_g, ln_b, ys)


def _prep_w_in(w_in):
    L, D, _ = w_in.shape
    o = 0
    cq = w_in[..., o:o + MLA_Q_RANK]; o += MLA_Q_RANK
    ckv = w_in[..., o:o + MLA_KV_RANK]; o += MLA_KV_RANK
    kr = w_in[..., o:o + MLA_ROPE]; o += MLA_ROPE
    gq = w_in[..., o:o + GDN_WIDTH]; o += GDN_WIDTH
    gk = w_in[..., o:o + GDN_WIDTH]; o += GDN_WIDTH
    gv = w_in[..., o:o + GDN_WIDTH]; o += GDN_WIDTH
    gz = w_in[..., o:o + GDN_WIDTH]; o += GDN_WIDTH
    ga = w_in[..., o:o + GDN_HEADS]; o += GDN_HEADS
    gb = w_in[..., o:o + GDN_HEADS]; o += GDN_HEADS
    pw = w_in[..., o:o + POOL_WIDTH]
    half = MLA_ROPE // 2
    z = lambda n: jnp.zeros((L, D, n), w_in.dtype)
    kra = jnp.concatenate([z(MLA_NOPE), kr, z(HEAD_PAD - MLA_NOPE - MLA_ROPE)], axis=-1)
    krb = jnp.concatenate([z(MLA_NOPE), kr[..., half:], kr[..., :half], z(HEAD_PAD - MLA_NOPE - MLA_ROPE)], axis=-1)
    a_e = jnp.repeat(ga, GDN_DIM, axis=-1)
    b_e = jnp.repeat(gb, GDN_DIM, axis=-1)
    out = jnp.concatenate([gq, gk, gv, gz, a_e, b_e, cq, pw, ckv, kra, krb], axis=-1)
    assert out.shape[-1] == IN_COLS
    return out.astype(BF16)


def _prep_mla(w_uq, w_ukv):
    L = w_uq.shape[0]
    H, half = MLA_HEADS, MLA_ROPE // 2
    pad = HEAD_PAD - MLA_NOPE - MLA_ROPE
    q = w_uq.reshape(L, MLA_Q_RANK, H, MLA_NOPE + MLA_ROPE)
    nope, r1, r2 = q[..., :MLA_NOPE], q[..., MLA_NOPE:MLA_NOPE + half], q[..., MLA_NOPE + half:]
    zq = lambda n: jnp.zeros((L, MLA_Q_RANK, H, n), w_uq.dtype)
    plain = jnp.concatenate([nope, r1, r2, zq(pad)], axis=-1).reshape(L, MLA_Q_RANK, H * HEAD_PAD)
    partner = jnp.concatenate([zq(MLA_NOPE), r2, r1, zq(pad)], axis=-1).reshape(L, MLA_Q_RANK, H * HEAD_PAD)
    wq2 = jnp.concatenate([plain, partner], axis=-1).astype(BF16)
    kv = w_ukv.reshape(L, MLA_KV_RANK, H, MLA_NOPE + MLA_V)
    k_nope, v = kv[..., :MLA_NOPE], kv[..., MLA_NOPE:]
    zk = lambda n: jnp.zeros((L, MLA_KV_RANK, H, n), w_ukv.dtype)
    k_main = jnp.concatenate([k_nope, zk(HEAD_PAD - MLA_NOPE)], axis=-1).reshape(L, MLA_KV_RANK, H * HEAD_PAD)
    even = (jnp.arange(H) % 2 == 0)[None, None, :, None]
    v_pair = jnp.where(even, jnp.concatenate([v, zk(MLA_V)], axis=-1), jnp.concatenate([zk(MLA_V), v], axis=-1))
    wkv2 = jnp.concatenate([k_main, v_pair.reshape(L, MLA_KV_RANK, H * HEAD_PAD)], axis=-1).astype(BF16)
    return wq2, wkv2


def _rope_tables(S):
    half = MLA_ROPE // 2
    inv_freq = jnp.power(ROPE_THETA, -jnp.arange(0, MLA_ROPE, 2, dtype=F32) / MLA_ROPE)
    ang = jnp.arange(S, dtype=F32)[:, None] * inv_freq[None, :]
    cos, sin = jnp.cos(ang), jnp.sin(ang)
    pad = jnp.zeros((S, HEAD_PAD - MLA_NOPE - MLA_ROPE), F32)
    cos_t = jnp.concatenate([jnp.ones((S, MLA_NOPE), F32), cos, cos, pad], axis=-1)
    sin_t = jnp.concatenate([jnp.zeros((S, MLA_NOPE), F32), -sin, sin, pad], axis=-1)
    scale = (MLA_NOPE + MLA_ROPE) ** -0.5
    return cos_t * scale, sin_t * scale, cos_t, sin_t


def _block_diag(blocks):
    L, G, n, _ = blocks.shape
    eye = jnp.eye(G, dtype=blocks.dtype)
    return jnp.einsum('lgij,gh->lgihj', blocks, eye).reshape(L, G * n, G * n)


def kernel(x, c, w_in, mla_q_norm, mla_kv_norm, mla_w_uq, mla_w_ukv, gdn_conv, gdn_a_log, gdn_dt_bias, gdn_out_norm, pool_w, pool_scale, w_out, w_mod, b_mod, ln1_g, ln1_b, ln2_g, ln2_b, router_w_group, router_b_group, router_w_expert, router_b_expert, moe_w_gate, moe_w_up, moe_w_down):
    B, S, D = x.shape
    L = w_in.shape[0]
    T = B * S
    alpha = (2 * L) ** 0.25
    ts = min(512, S)
    tq = min(256, S)
    t_moe = min(512, S)
    t_disp = min(1024, S)
    assert D == D_MODEL and S % ts == 0 and S % tq == 0 and ts % CHUNK == 0
    assert S % t_disp == 0 and t_moe % DMA_WAIT_UNROLL == 0

    w_in2 = _prep_w_in(w_in)
    wq2, wkv2 = _prep_mla(mla_w_uq, mla_w_ukv)
    tabs = _rope_tables(S)
    gq = mla_q_norm.reshape(L, 1, MLA_Q_RANK)
    gkv = mla_kv_norm.reshape(L, 1, MLA_KV_RANK)
    alog_e = jnp.repeat(gdn_a_log, GDN_DIM, axis=-1).reshape(L, 1, GDN_WIDTH)
    dt_e = jnp.repeat(gdn_dt_bias, GDN_DIM, axis=-1).reshape(L, 1, GDN_WIDTH)
    og_e = jnp.tile(gdn_out_norm, (1, GDN_HEADS)).reshape(L, 1, GDN_WIDTH)
    lane_head = jnp.arange(GDN_WIDTH) // GDN_DIM
    seg = (lane_head[:, None] == lane_head[None, :]).astype(BF16)
    tri_c = (jnp.arange(CHUNK)[:, None] >= jnp.arange(CHUNK)[None, :]).astype(BF16)
    pool_bd = _block_diag(pool_w).astype(BF16)
    pool_sc = pool_scale.reshape(L, 1, POOL_WIDTH)
    w_out_b = w_out.astype(BF16)
    w_r = jnp.concatenate([router_w_group, router_w_expert,
                           jnp.zeros((L, D, LANE - N_GROUPS - N_EXPERTS), F32)], axis=-1)
    w_r_hi = w_r.astype(BF16)
    w_r_lo = (w_r - w_r_hi.astype(F32)).astype(BF16)
    b_r = jnp.concatenate([router_b_group, router_b_expert,
                           jnp.zeros((L, LANE - N_GROUPS - N_EXPERTS), F32)], axis=-1).reshape(L, 1, LANE)
    tri_r = (jnp.arange(ts)[:, None] > jnp.arange(ts)[None, :]).astype(BF16)
    ln1g, ln1b = ln1_g.reshape(L, 1, D), ln1_b.reshape(L, 1, D)
    ln2g, ln2b = ln2_g.reshape(L, 1, D), ln2_b.reshape(L, 1, D)

    assert N_BUCKETS <= LANE
    step_rows = MOE_ROWS * FFN_BLOCKS
    max_rows = T + N_BUCKETS * (MOE_ROWS - 1) + N_GROUPS * (step_rows - MOE_ROWS)
    P = (max_rows + step_rows - 1) // step_rows * step_rows
    nb = P // MOE_ROWS
    pg, pa, pb = [], [], []
    for g_ in range(N_GROUPS):
        for a_ in range(EXPERTS_PER_GROUP):
            for b_ in range(a_ + 1, EXPERTS_PER_GROUP):
                pg.append(g_)
                pa.append(a_)
                pb.append(b_)
    bucket_g = jnp.asarray(pg, jnp.int32)
    bucket_a = jnp.asarray(pa, jnp.int32)
    bucket_b = jnp.asarray(pb, jnp.int32)
    grouped = lambda w: w.astype(BF16).reshape((L, N_GROUPS, EXPERTS_PER_GROUP) + w.shape[2:])
    wg_b, wu_b, wd_b = grouped(moe_w_gate), grouped(moe_w_up), grouped(moe_w_down)
    bucket_ids = jnp.arange(N_BUCKETS, dtype=jnp.int32)
    blk0 = jnp.arange(nb, dtype=jnp.int32) * MOE_ROWS
    xs = jnp.zeros((P, D + LANE), F32)

    mod = _modulation(c, w_mod, b_mod)
    x2 = x.reshape(T, D)
    for l in range(L):
        mod_l = mod[l].reshape(B, 1, 6 * D)
        proj = _inproj(x2, mod_l, w_in2, l, S, ts)
        q, k, v = _mla_proj(proj, tabs, gq, gkv, wq2, wkv2, l, S, ts)
        y_mla = _attention(q, k, v, B, S, tq)
        qh, kh, vh, g, beta = _gdn_pre(proj, gdn_conv, alog_e, dt_e, seg, l, B, S, ts)
        y_gdn = _gdn(qh, kh, vh, g, beta, proj, og_e, seg, tri_c, l, B, S, ts)
        y_pool = _pool(proj, pool_bd, pool_sc, l, B, S, ts)
        x2 = _outproj(y_mla, y_gdn, y_pool, w_out_b, x2, mod_l, ln1g, ln1b, l, S, ts, alpha)

        hm, cnt = _router(x2, mod_l, w_r_hi, w_r_lo, b_r, tri_r, l, S, ts)
        counts = cnt[0, :N_BUCKETS].astype(jnp.int32)
        padded = ((counts + MOE_ROWS - 1) // MOE_ROWS * MOE_ROWS).reshape(N_GROUPS, PAIRS_PER_GROUP)
        group_rows = jnp.sum(padded, axis=1)
        group_pad = (step_rows - group_rows % step_rows) % step_rows
        padded = padded.at[:, PAIRS_PER_GROUP - 1].add(group_pad).reshape(N_BUCKETS)
        pad_end = jnp.cumsum(padded)
        pad_start = pad_end - padded
        bucket = hm[:, D].astype(jnp.int32)
        rank = hm[:, D + 1].astype(jnp.int32)
        dest = jnp.sum(jnp.where(bucket[:, None] == bucket_ids[None, :], pad_start[None, :], 0), axis=1) + rank
        block_bucket = jnp.minimum(jnp.sum((pad_end[None, :] <= blk0[:, None]).astype(jnp.int32), axis=1),
                                   N_BUCKETS - 1)
        onehot_b = (block_bucket[:, None] == bucket_ids[None, :]).astype(jnp.int32)
        grp = jnp.sum(onehot_b * bucket_g[None, :], axis=1)
        ea = jnp.sum(onehot_b * bucket_a[None, :], axis=1)
        eb = jnp.sum(onehot_b * bucket_b[None, :], axis=1)
        used = (blk0 < pad_end[-1]).astype(jnp.int32)

        xs = _dispatch(hm, dest, xs, t_disp)
        ys = _ffn(xs, grp, ea, eb, used, wg_b, wu_b, wd_b, l)
        x2 = _combine(ys, dest, x2, mod_l, ln2g, ln2b, l, S, t_moe, alpha)
    return x2.reshape(B, S, D)
```

```python
import functools
import math

import jax
import jax.numpy as jnp
from jax import lax
from jax.experimental import pallas as pl
from jax.experimental.pallas import tpu as pltpu

F32 = jnp.float32
BF16 = jnp.bfloat16

D_MODEL = 1024
CHUNK = 64
MLA_HEADS = 8
MLA_NOPE = 64
MLA_ROPE = 32
MLA_V = 64
MLA_Q_RANK = 256
MLA_KV_RANK = 128
MLA_WIDTH = MLA_HEADS * MLA_V
ROPE_THETA = 10000.0
GDN_HEADS = 4
GDN_DIM = 64
GDN_WIDTH = GDN_HEADS * GDN_DIM
GDN_CONV = 4
POOL_WIDTH = 256
POOL_WINDOWS = (2, 4, 8, 16)
POOL_GROUP_DIM = 64
N_GROUPS = 4
EXPERTS_PER_GROUP = 8
N_EXPERTS = N_GROUPS * EXPERTS_PER_GROUP
TOP_K = 2
EXPERT_FF = 256
LN_EPS = 1e-5
RMS_EPS = 1e-6

LANE = 128
SUBLANES = 8
HEAD_PAD = 128
PAIRS_PER_GROUP = EXPERTS_PER_GROUP * (EXPERTS_PER_GROUP - 1) // 2
N_BUCKETS = N_GROUPS * PAIRS_PER_GROUP
MOE_ROWS = 128
VMEM_LIMIT = 48 * 1024 * 1024

_C_GQ, _C_GK, _C_GV, _C_GZ = 0, 256, 512, 768
_C_A, _C_B = 1024, 1280
_C_CQ, _C_POOL = 1536, 1792
_C_CKV, _C_KRA, _C_KRB = 2048, 2176, 2304
IN_COLS = 2432


def _cparams(sem):
    return pltpu.CompilerParams(dimension_semantics=sem, vmem_limit_bytes=VMEM_LIMIT)


def _sigmoid(x):
    return 1.0 / (1.0 + jnp.exp(-x))


def _split2(x):
    hi = x.astype(BF16)
    lo = (x - hi.astype(F32)).astype(BF16)
    return hi, lo


def _split3(x):
    hi = x.astype(BF16)
    r = x - hi.astype(F32)
    mid = r.astype(BF16)
    lo = (r - mid.astype(F32)).astype(BF16)
    return hi, mid, lo


def _dot(a, b):
    return jnp.dot(a, b, preferred_element_type=F32)


def _dot_nt(a, b):
    return lax.dot_general(a, b, (((1,), (1,)), ((), ())), preferred_element_type=F32)


def _dot_tn(a, b):
    return lax.dot_general(a, b, (((0,), (0,)), ((), ())), preferred_element_type=F32)


def _mod_body(c_ref, w_ref, b_ref, o_ref):
    c = c_ref[...]
    ca = c * _sigmoid(c)
    o_ref[0] = _dot(ca.astype(BF16), w_ref[0].astype(BF16)) + b_ref[0]


def _modulation(c, w_mod, b_mod):
    L, D, N = w_mod.shape
    B = c.shape[0]
    tn = 1024
    return pl.pallas_call(
        _mod_body,
        grid=(L, N // tn),
        in_specs=[pl.BlockSpec((B, D), lambda l, j: (0, 0)),
                  pl.BlockSpec((1, D, tn), lambda l, j: (l, 0, j)),
                  pl.BlockSpec((1, 1, tn), lambda l, j: (l, 0, j))],
        out_specs=pl.BlockSpec((1, B, tn), lambda l, j: (l, 0, j)),
        out_shape=jax.ShapeDtypeStruct((L, B, N), F32),
        compiler_params=_cparams(("arbitrary", "arbitrary")),
        name="modulation",
    )(c, w_mod, b_mod.reshape(L, 1, N))


def _inproj_body(x_ref, sh_ref, sc_ref, w_ref, o_ref):
    h = x_ref[...] * (1.0 + sc_ref[0]) + sh_ref[0]
    o_ref[...] = _dot(h.astype(BF16), w_ref[0])


def _inproj(x2, mod_l, w_in2, l, S, tm):
    T, D = x2.shape
    nS = S // tm
    return pl.pallas_call(
        _inproj_body,
        grid=(T // tm,),
        in_specs=[pl.BlockSpec((tm, D), lambda i: (i, 0)),
                  pl.BlockSpec((1, 1, D), lambda i: (i // nS, 0, 0)),
                  pl.BlockSpec((1, 1, D), lambda i: (i // nS, 0, 1)),
                  pl.BlockSpec((1, D, IN_COLS), lambda i: (l, 0, 0))],
        out_specs=pl.BlockSpec((tm, IN_COLS), lambda i: (i, 0)),
        out_shape=jax.ShapeDtypeStruct((T, IN_COLS), F32),
        compiler_params=_cparams(("arbitrary",)),
        name="inproj",
    )(x2, mod_l, mod_l, w_in2)


def _mla_proj_body(cq_ref, ckv_ref, kra_ref, krb_ref, cosq_ref, sinq_ref, cosk_ref, sink_ref,
                   gq_ref, gkv_ref, wq_ref, wkv_ref, q_out, k_out, v_out):
    cq = cq_ref[...]
    qn = cq * lax.rsqrt(jnp.mean(cq * cq, axis=-1, keepdims=True) + RMS_EPS) * gq_ref[0]
    q2 = _dot(qn.astype(BF16), wq_ref[0])
    ckv = ckv_ref[...]
    kvn = ckv * lax.rsqrt(jnp.mean(ckv * ckv, axis=-1, keepdims=True) + RMS_EPS) * gkv_ref[0]
    kv2 = _dot(kvn.astype(BF16), wkv_ref[0])
    cq_t, sq_t = cosq_ref[...], sinq_ref[...]
    krope = kra_ref[...] * cosk_ref[...] + krb_ref[...] * sink_ref[...]
    hw = MLA_HEADS * HEAD_PAD
    for h in range(MLA_HEADS):
        a, b = h * HEAD_PAD, (h + 1) * HEAD_PAD
        q_out[:, a:b] = (q2[:, a:b] * cq_t + q2[:, hw + a:hw + b] * sq_t).astype(BF16)
        k_out[:, a:b] = (kv2[:, a:b] + krope).astype(BF16)
    vl = lax.broadcasted_iota(jnp.int32, (1, hw), 1)
    ones_lane = jnp.where((vl // HEAD_PAD) % 2 == 0, MLA_V, 0)
    v_out[...] = (kv2[:, hw:] + (vl % HEAD_PAD == ones_lane).astype(F32)).astype(BF16)


def _mla_proj(proj, tabs, gq, gkv, wq2, wkv2, l, S, ts):
    T = proj.shape[0]
    nS = S // ts
    hw = MLA_HEADS * HEAD_PAD
    tab_spec = pl.BlockSpec((ts, LANE), lambda i: (i % nS, 0))
    return pl.pallas_call(
        _mla_proj_body,
        grid=(T // ts,),
        in_specs=[pl.BlockSpec((ts, 256), lambda i: (i, _C_CQ // 256)),
                  pl.BlockSpec((ts, 128), lambda i: (i, _C_CKV // 128)),
                  pl.BlockSpec((ts, 128), lambda i: (i, _C_KRA // 128)),
                  pl.BlockSpec((ts, 128), lambda i: (i, _C_KRB // 128)),
                  tab_spec, tab_spec, tab_spec, tab_spec,
                  pl.BlockSpec((1, 1, MLA_Q_RANK), lambda i: (l, 0, 0)),
                  pl.BlockSpec((1, 1, MLA_KV_RANK), lambda i: (l, 0, 0)),
                  pl.BlockSpec((1, MLA_Q_RANK, 2 * hw), lambda i: (l, 0, 0)),
                  pl.BlockSpec((1, MLA_KV_RANK, 2 * hw), lambda i: (l, 0, 0))],
        out_specs=[pl.BlockSpec((ts, hw), lambda i: (i, 0))] * 3,
        out_shape=[jax.ShapeDtypeStruct((T, hw), BF16)] * 3,
        compiler_params=_cparams(("arbitrary",)),
        name="mla_proj",
    )(proj, proj, proj, proj, *tabs, gq, gkv, wq2, wkv2)


def _attn_body(qi_tab, kj_tab, q_ref, k_ref, v_ref, o_ref, m_s, acc_s, *, tq, n_q, n_tiles):
    row_c = lax.broadcasted_iota(jnp.int32, (tq, tq), 0) // CHUNK
    col_c = lax.broadcasted_iota(jnp.int32, (tq, tq), 1) // CHUNK
    allowed = col_c <= row_c
    lane = lax.broadcasted_iota(jnp.int32, (tq, HEAD_PAD), 1)
    sl = [slice(hh * HEAD_PAD, (hh + 1) * HEAD_PAD) for hh in range(2)]
    hs = range(2)

    def rows_of(tile):
        return pl.ds(pl.multiple_of(tile * tq, tq), tq)

    def stage_a(t):
        q = q_ref[rows_of(qi_tab[t]), :]
        kk = k_ref[rows_of(kj_tab[t]), :]
        return tuple(_dot_nt(q[:, sl[hh]], kk[:, sl[hh]]).astype(BF16) for hh in hs)

    def stage_b(t, s_pair):
        qi, kj = qi_tab[t], kj_tab[t]
        rows = rows_of(qi)
        keep = jnp.logical_or(allowed, kj < qi)
        neg = jnp.asarray(-jnp.inf, BF16)
        s = [jnp.where(keep, s_pair[hh], neg) for hh in hs]
        m_prev = [m_s[hh, rows, :] for hh in hs]
        m_new = [jnp.maximum(m_prev[hh], jnp.max(s[hh], axis=-1, keepdims=True)) for hh in hs]
        p = tuple(jnp.exp(s[hh] - jnp.concatenate([m_new[hh]] * (tq // HEAD_PAD), axis=1)) for hh in hs)
        alpha = tuple(jnp.exp(m_prev[hh] - m_new[hh]) for hh in hs)
        for hh in hs:
            m_s[hh, rows, :] = m_new[hh]
        return p, alpha

    def stage_c(t, p, alpha):
        qi, kj = qi_tab[t], kj_tab[t]
        rows = rows_of(qi)
        vv = v_ref[rows_of(kj), :]
        pv = [_dot(p[hh], vv[:, sl[hh]]) for hh in hs]
        for hh in hs:
            acc_s[hh, rows, :] = alpha[hh].astype(F32) * acc_s[hh, rows, :] + pv[hh]

    def body(t, c):
        s_next, (p, alpha) = c
        s_after = stage_a(t + 2)
        stage_c(t, p, alpha)
        return s_after, stage_b(t + 1, s_next)

    m_s[...] = jnp.full(m_s.shape, -jnp.inf, m_s.dtype)
    acc_s[...] = jnp.zeros_like(acc_s)
    pb0 = stage_b(0, stage_a(0))
    s_last, pb = lax.fori_loop(0, n_tiles - 2, body, (stage_a(1), pb0), unroll=ATTN_UNROLL)
    pb_last = stage_b(n_tiles - 1, s_last)
    stage_c(n_tiles - 2, *pb)
    stage_c(n_tiles - 1, *pb_last)

    def normalize(i, _):
        rows = rows_of(i)
        acc0, acc1 = acc_s[0, rows, :], acc_s[1, rows, :]
        o0 = jnp.where(lane < MLA_V, acc0 * (1.0 / acc0[:, MLA_V:MLA_V + 1]), 0.0)
        o1 = jnp.where(lane >= MLA_V, acc1 * (1.0 / acc1[:, 0:1]), 0.0)
        o_ref[rows, :] = (o0 + o1).astype(BF16)
        return 0

    lax.fori_loop(0, n_q, normalize, 0)


def _attention(q, k, v, B, S, tq):
    T = q.shape[0]
    nq = S // tq
    pairs = MLA_HEADS // 2
    tiles = [(i, j) for i in range(nq) for j in range(i + 1)]
    assert len(tiles) >= 2
    qi_tab = jnp.asarray([t[0] for t in tiles], jnp.int32)
    kj_tab = jnp.asarray([t[1] for t in tiles], jnp.int32)
    seq = pl.BlockSpec((S, 2 * HEAD_PAD), lambda b, p, qt, kt: (b, p))
    grid_spec = pltpu.PrefetchScalarGridSpec(
        num_scalar_prefetch=2,
        grid=(B, pairs),
        in_specs=[seq, seq, seq],
        out_specs=pl.BlockSpec((S, 2 * MLA_V), lambda b, p, qt, kt: (b, p)),
        scratch_shapes=[pltpu.VMEM((2, S, HEAD_PAD), BF16), pltpu.VMEM((2, S, HEAD_PAD), F32)],
    )
    return pl.pallas_call(
        functools.partial(_attn_body, tq=tq, n_q=nq, n_tiles=len(tiles)),
        grid_spec=grid_spec,
        out_shape=jax.ShapeDtypeStruct((T, MLA_WIDTH), BF16),
        compiler_params=_cparams(("arbitrary", "arbitrary")),
        name="mla_attention",
    )(qi_tab, kj_tab, q, k, v)


def _pool_body(p_ref, w_ref, sc_ref, o_ref, prev_ref, *, ts):
    j = pl.program_id(1)
    hist = 16

    @pl.when(j == 0)
    def _():
        prev_ref[...] = jnp.zeros_like(prev_ref)

    cur = p_ref[...]
    x = jnp.concatenate([prev_ref[...], cur], axis=0)
    s1 = x + pltpu.roll(x, 1, axis=0)
    s2 = s1 + pltpu.roll(s1, 2, axis=0)
    s4 = s2 + pltpu.roll(s2, 4, axis=0)
    s8 = s4 + pltpu.roll(s4, 8, axis=0)
    pos = (j * ts + lax.broadcasted_iota(jnp.int32, (ts, POOL_WIDTH), 0) + 1).astype(F32)
    lane = lax.broadcasted_iota(jnp.int32, (ts, POOL_WIDTH), 1)
    sums = (s1, s2, s4, s8)
    mean = None
    for gi, win in enumerate(POOL_WINDOWS):
        m_g = sums[gi][hist:] / jnp.minimum(pos, float(win))
        mean = m_g if mean is None else jnp.where(lane >= gi * POOL_GROUP_DIM, m_g, mean)
    delta = mean - cur
    o_ref[...] = (_dot(delta.astype(BF16), w_ref[0]) * sc_ref[0]).astype(BF16)
    prev_ref[...] = cur[ts - hist:]


def _pool(proj, w_bd, scale, l, B, S, ts):
    T = proj.shape[0]
    nS = S // ts
    return pl.pallas_call(
        functools.partial(_pool_body, ts=ts),
        grid=(B, nS),
        in_specs=[pl.BlockSpec((ts, POOL_WIDTH), lambda b, j: (b * nS + j, _C_POOL // 256)),
                  pl.BlockSpec((1, POOL_WIDTH, POOL_WIDTH), lambda b, j: (l, 0, 0)),
                  pl.BlockSpec((1, 1, POOL_WIDTH), lambda b, j: (l, 0, 0))],
        out_specs=pl.BlockSpec((ts, POOL_WIDTH), lambda b, j: (b * nS + j, 0)),
        out_shape=jax.ShapeDtypeStruct((T, POOL_WIDTH), BF16),
        scratch_shapes=[pltpu.VMEM((16, POOL_WIDTH), F32)],
        compiler_params=_cparams(("arbitrary", "arbitrary")),
        name="pool_mixer",
    )(proj, w_bd, scale)


def _head_sum(x, seg_ref):
    hi, lo = _split2(x)
    return _dot(hi, seg_ref[...]) + _dot(lo, seg_ref[...])


def _gdn_inputs(s, raw_q, raw_k, raw_v, raw_a, raw_b, conv_ref, alog_ref, dt_ref, seg_ref,
                q_ref, k_ref, v_ref, g_ref, b_ref, prev_ref, ts):
    hist = 8
    cw = conv_ref[0]
    outs = []
    for idx, ref in enumerate((raw_q, raw_k, raw_v)):
        cur = ref[s]
        a, b = idx * GDN_WIDTH, (idx + 1) * GDN_WIDTH
        x = jnp.concatenate([prev_ref[s, :, a:b], cur], axis=0)
        w = cw[:, a:b]
        y = (w[3:4] * x + w[2:3] * pltpu.roll(x, 1, axis=0) + w[1:2] * pltpu.roll(x, 2, axis=0)
             + w[0:1] * pltpu.roll(x, 3, axis=0))[hist:]
        outs.append(y * _sigmoid(y))
        prev_ref[s, :, a:b] = cur[ts - hist:]
    qc, kc, vc = outs
    q_ref[s] = qc * lax.rsqrt(_head_sum(qc * qc, seg_ref) + RMS_EPS) * (GDN_DIM ** -0.5)
    k_ref[s] = kc * lax.rsqrt(_head_sum(kc * kc, seg_ref) + RMS_EPS)
    v_ref[s] = vc
    z = raw_a[s] + dt_ref[0]
    softplus = jnp.maximum(z, 0.0) + jnp.log(1.0 + jnp.exp(-jnp.abs(z)))
    g_ref[s] = -jnp.exp(alog_ref[0]) * softplus
    b_ref[s] = _sigmoid(raw_b[s])


def _gdn_body(raw_q, raw_k, raw_v, raw_a, raw_b, z_ref, conv_ref, alog_ref, dt_ref, og_ref, seg_ref, tri_ref,
              o_ref, state_ref, prev_ref, q_ref, k_ref, v_ref, g_ref, b_ref,
              u_s, w_s, qk_s, qd_s, kd_s, gl_s, *, ts):
    j = pl.program_id(1)
    C, W, H = CHUNK, GDN_WIDTH, GDN_HEADS
    n_chunks = ts // C
    groups = n_chunks // INTRA_UNROLL

    @pl.when(j == 0)
    def _():
        state_ref[...] = jnp.zeros_like(state_ref)
        prev_ref[...] = jnp.zeros_like(prev_ref)

    for s in range(GDN_SEQS):
        _gdn_inputs(s, raw_q, raw_k, raw_v, raw_a, raw_b, conv_ref, alog_ref, dt_ref, seg_ref,
                    q_ref, k_ref, v_ref, g_ref, b_ref, prev_ref, ts)

    lane = lax.broadcasted_iota(jnp.int32, (C, W), 1)
    row = lax.broadcasted_iota(jnp.int32, (C, W), 0)
    col_tok = lane % C
    incl = row >= col_tok
    strict = row > col_tok
    eye = (row == col_tok).astype(F32)
    head_masks = [(lane // C == h).astype(F32) for h in range(H)]
    bd_mask = (lax.broadcasted_iota(jnp.int32, (W, W), 0) // C
               == lax.broadcasted_iota(jnp.int32, (W, W), 1) // C)

    def expand(m):
        return jnp.concatenate([m * hm for hm in head_masks], axis=0)

    def bdot(lhs, rhs):
        return _dot(lhs.astype(BF16), rhs.astype(BF16))

    def intra(i, _):
        sq = i // groups
        cs = [(i % groups) * INTRA_UNROLL + uu for uu in range(INTRA_UNROLL)]
        r0s = [pl.multiple_of(c * C, C) for c in cs]
        n = range(INTRA_UNROLL)
        tri = tri_ref[...]
        gs = [_split3(g_ref[sq, pl.ds(r0, C), :]) for r0 in r0s]
        gc = [_dot(tri, g[0]) + _dot(tri, g[1]) + _dot(tri, g[2]) for g in gs]
        ks = [k_ref[sq, pl.ds(r0, C), :] for r0 in r0s]
        qs = [q_ref[sq, pl.ds(r0, C), :] for r0 in r0s]
        betas = [b_ref[sq, pl.ds(r0, C), :] for r0 in r0s]
        kb = [ks[x] * betas[x] for x in n]
        aq = [_dot_nt(jnp.concatenate([kb[x], qs[x]], axis=0).astype(BF16), expand(ks[x]).astype(BF16)) for x in n]
        g_row = [jnp.sum(gc[x] * eye, axis=0, keepdims=True) for x in n]
        g_last = [gc[x][C - 1:C, :] for x in n]
        decay = [jnp.where(incl, jnp.exp(jnp.where(incl, gc[x] - g_row[x], 0.0)), 0.0) for x in n]
        e_gc = [jnp.exp(gc[x]) for x in n]
        a_cat = [jnp.where(strict, aq[x][:C] * decay[x], 0.0) for x in n]
        for x in n:
            r0 = r0s[x]
            qk_s[sq, pl.ds(r0, C), :] = jnp.where(incl, aq[x][C:] * decay[x], 0.0).astype(BF16)
            qd_s[sq, pl.ds(r0, C), :] = (qs[x] * e_gc[x]).astype(BF16)
            kd_s[sq, pl.ds(r0, C), :] = (ks[x] * jnp.exp(g_last[x] - gc[x])).astype(BF16)
            gl_s[sq, pl.ds(pl.multiple_of(cs[x] * 8, 8), 8), :] = jnp.broadcast_to(jnp.exp(g_last[x]), (8, W))
        x_cat = [eye - a_cat[x] for x in n]
        p_cat = [bdot(a_cat[x], expand(a_cat[x])) for x in n]
        n_fac = int(math.log2(C)) - 1
        for r in range(n_fac):
            last = r == n_fac - 1
            xp = [bdot(x_cat[x] if last else jnp.concatenate([x_cat[x], p_cat[x]], axis=0), expand(p_cat[x]))
                  for x in n]
            x_cat = [x_cat[x] + xp[x][:C] for x in n]
            if not last:
                p_cat = [xp[x][C:] for x in n]
        for x in n:
            r0 = r0s[x]
            t_cat = x_cat[x].astype(BF16)
            v = v_ref[sq, pl.ds(r0, C), :]
            u_s[sq, pl.ds(r0, C), :] = _dot(t_cat, expand(v * betas[x]).astype(BF16))
            w_s[sq, pl.ds(r0, C), :] = _dot(t_cat, expand(kb[x] * e_gc[x]).astype(BF16)).astype(BF16)
        return 0

    lax.fori_loop(0, GDN_SEQS * groups, intra, 0)

    def scan(c, _):
        r0 = pl.multiple_of(c * C, C)
        sq = range(GDN_SEQS)
        state = [state_ref[s] for s in sq]
        wq = [_dot(jnp.concatenate([w_s[s, pl.ds(r0, C), :], qd_s[s, pl.ds(r0, C), :]], axis=0),
                   state[s].astype(BF16)) for s in sq]
        v_new = [u_s[s, pl.ds(r0, C), :] - wq[s][:C] for s in sq]
        upd = [_dot_tn(kd_s[s, pl.ds(r0, C), :], v_new[s].astype(BF16)) for s in sq]
        for s in sq:
            g_l = gl_s[s, pl.ds(pl.multiple_of(c * 8, 8), 1), :]
            state_ref[s] = state[s] * g_l + jnp.where(bd_mask, upd[s], 0.0)
        o = [wq[s][C:] + _dot(qk_s[s, pl.ds(r0, C), :], expand(v_new[s]).astype(BF16)) for s in sq]
        ms = [_head_sum(o[s] * o[s], seg_ref) * (1.0 / GDN_DIM) for s in sq]
        for s in sq:
            z = z_ref[s, pl.ds(r0, C), :]
            y = o[s] * lax.rsqrt(ms[s] + RMS_EPS) * og_ref[0] * (z * _sigmoid(z))
            o_ref[s, pl.ds(r0, C), :] = y.astype(BF16)
        return 0

    lax.fori_loop(0, n_chunks, scan, 0, unroll=2)


def _gdn(proj, conv_w, alog_e, dt_e, og, seg, tri, l, B, S, ts):
    T = proj.shape[0]
    nS = S // ts
    W = GDN_WIDTH
    Q = GDN_SEQS
    proj3 = proj.reshape(B, S, proj.shape[-1])

    def col(c):
        return pl.BlockSpec((Q, ts, W), lambda b, j: (b, j, c // W))

    vec = pl.BlockSpec((1, 1, W), lambda b, j: (l, 0, 0))
    seq_f32 = pltpu.VMEM((Q, ts, W), F32)
    out = pl.pallas_call(
        functools.partial(_gdn_body, ts=ts),
        grid=(B // Q, nS),
        in_specs=[col(_C_GQ), col(_C_GK), col(_C_GV), col(_C_A), col(_C_B), col(_C_GZ),
                  pl.BlockSpec((1, GDN_CONV, 3 * W), lambda b, j: (l, 0, 0)),
                  vec, vec, vec,
                  pl.BlockSpec((W, W), lambda b, j: (0, 0)),
                  pl.BlockSpec((CHUNK, CHUNK), lambda b, j: (0, 0))],
        out_specs=pl.BlockSpec((Q, ts, W), lambda b, j: (b, j, 0)),
        out_shape=jax.ShapeDtypeStruct((B, S, W), BF16),
        scratch_shapes=[pltpu.VMEM((Q, W, W), F32), pltpu.VMEM((Q, 8, 3 * W), F32)]
        + [seq_f32] * 5 + [seq_f32] + [pltpu.VMEM((Q, ts, W), BF16)] * 4
        + [pltpu.VMEM((Q, 8 * (ts // CHUNK), W), F32)],
        compiler_params=_cparams(("arbitrary", "arbitrary")),
        name="gdn_delta_rule",
    )(proj3, proj3, proj3, proj3, proj3, proj3, conv_w, alog_e, dt_e, og, seg, tri)
    return out.reshape(T, W)


def _layer_norm(r, g, b):
    mu = jnp.mean(r, axis=-1, keepdims=True)
    d = r - mu
    var = jnp.mean(d * d, axis=-1, keepdims=True)
    return d * lax.rsqrt(var + LN_EPS) * g + b


def _outproj_body(ym_ref, yg_ref, yp_ref, w_ref, x_ref, gt_ref, lg_ref, lb_ref, o_ref, *, alpha):
    w = w_ref[0]
    y = (_dot(ym_ref[...], w[:MLA_WIDTH]) + _dot(yg_ref[...], w[MLA_WIDTH:MLA_WIDTH + GDN_WIDTH])
         + _dot(yp_ref[...], w[MLA_WIDTH + GDN_WIDTH:]))
    r = alpha * x_ref[...] + (1.0 + gt_ref[0]) * y
    o_ref[...] = _layer_norm(r, lg_ref[0], lb_ref[0])


def _outproj(y_mla, y_gdn, y_pool, w_out, x2, mod_l, ln_g, ln_b, l, S, tm, alpha):
    T, D = x2.shape
    nS = S // tm
    vec = pl.BlockSpec((1, 1, D), lambda i: (l, 0, 0))
    return pl.pallas_call(
        functools.partial(_outproj_body, alpha=alpha),
        grid=(T // tm,),
        in_specs=[pl.BlockSpec((tm, MLA_WIDTH), lambda i: (i, 0)),
                  pl.BlockSpec((tm, GDN_WIDTH), lambda i: (i, 0)),
                  pl.BlockSpec((tm, POOL_WIDTH), lambda i: (i, 0)),
                  pl.BlockSpec((1, D, D), lambda i: (l, 0, 0)),
                  pl.BlockSpec((tm, D), lambda i: (i, 0)),
                  pl.BlockSpec((1, 1, D), lambda i: (i // nS, 0, 2)),
                  vec, vec],
        out_specs=pl.BlockSpec((tm, D), lambda i: (i, 0)),
        out_shape=jax.ShapeDtypeStruct((T, D), F32),
        compiler_params=_cparams(("arbitrary",)),
        name="outproj_ln",
    )(y_mla, y_gdn, y_pool, w_out, x2, mod_l, ln_g, ln_b)


def _lane_first(cond, lane_f):
    return jnp.min(jnp.where(cond, lane_f, float(LANE)), axis=-1, keepdims=True)


def _router_body(x_ref, sh_ref, sc_ref, whi_ref, wlo_ref, br_ref, tri_ref,
                 hm_out, cnt_out, carry_ref):
    i = pl.program_id(0)

    @pl.when(i == 0)
    def _():
        carry_ref[...] = jnp.zeros_like(carry_ref)

    h = x_ref[...] * (1.0 + sc_ref[0]) + sh_ref[0]
    hm_out[:, :D_MODEL] = h
    h_hi, h_lo = _split2(h)
    logits = _dot(h_hi, whi_ref[0]) + _dot(h_lo, whi_ref[0]) + _dot(h_hi, wlo_ref[0]) + br_ref[0]
    tm = logits.shape[0]
    lane = lax.broadcasted_iota(jnp.int32, (tm, LANE), 1)
    lane_f = lane.astype(F32)
    neg = -jnp.inf
    gl = jnp.where(lane < N_GROUPS, logits, neg)
    gmax = jnp.max(gl, axis=-1, keepdims=True)
    gsel = _lane_first(gl == gmax, lane_f)
    g_p = 1.0 / jnp.sum(jnp.exp(gl - gmax), axis=-1, keepdims=True)
    lo = N_GROUPS + EXPERTS_PER_GROUP * gsel
    el = jnp.where((lane_f >= lo) & (lane_f < lo + EXPERTS_PER_GROUP), logits, neg)
    m1 = jnp.max(el, axis=-1, keepdims=True)
    i1 = _lane_first(el == m1, lane_f)
    el2 = jnp.where(lane_f == i1, neg, el)
    m2 = jnp.max(el2, axis=-1, keepdims=True)
    i2 = _lane_first(el2 == m2, lane_f)
    t = jnp.exp(m2 - m1)
    w1 = g_p / (1.0 + t)
    w2 = g_p * t / (1.0 + t)
    loc1 = i1 - lo
    loc2 = i2 - lo
    a_loc = jnp.minimum(loc1, loc2)
    b_loc = jnp.maximum(loc1, loc2)
    pair = a_loc * (2 * EXPERTS_PER_GROUP - 1 - a_loc) * 0.5 + (b_loc - a_loc - 1.0)
    bucket = gsel * PAIRS_PER_GROUP + pair
    first_is_a = loc1 < loc2
    w_a = jnp.where(first_is_a, w1, w2)
    w_b = jnp.where(first_is_a, w2, w1)
    hit = lane_f == bucket
    onehot = hit.astype(BF16)
    before = _dot(tri_ref[...], onehot) + carry_ref[0:1, :]
    rank = jnp.sum(jnp.where(hit, before, 0.0), axis=-1, keepdims=True)
    total = carry_ref[0:1, :] + jnp.sum(onehot.astype(F32), axis=0, keepdims=True)
    carry_ref[...] = jnp.broadcast_to(total, carry_ref.shape)
    cnt_out[...] = jnp.broadcast_to(total, cnt_out.shape)
    meta = jnp.zeros((tm, LANE), F32)
    for idx, val in enumerate((bucket, rank, w_a, w_b)):
        meta = jnp.where(lane == idx, val, meta)
    hm_out[:, D_MODEL:] = meta


def _router(x2, mod_l, w_hi, w_lo, b_r, tri, l, S, tm):
    T, D = x2.shape
    nS = S // tm
    return pl.pallas_call(
        _router_body,
        grid=(T // tm,),
        in_specs=[pl.BlockSpec((tm, D), lambda i: (i, 0)),
                  pl.BlockSpec((1, 1, D), lambda i: (i // nS, 0, 3)),
                  pl.BlockSpec((1, 1, D), lambda i: (i // nS, 0, 4)),
                  pl.BlockSpec((1, D, LANE), lambda i: (l, 0, 0)),
                  pl.BlockSpec((1, D, LANE), lambda i: (l, 0, 0)),
                  pl.BlockSpec((1, 1, LANE), lambda i: (l, 0, 0)),
                  pl.BlockSpec((tm, tm), lambda i: (0, 0))],
        out_specs=[pl.BlockSpec((tm, D + LANE), lambda i: (i, 0)),
                   pl.BlockSpec((8, LANE), lambda i: (0, 0))],
        out_shape=[jax.ShapeDtypeStruct((T, D + LANE), F32),
                   jax.ShapeDtypeStruct((8, LANE), F32)],
        scratch_shapes=[pltpu.VMEM((8, LANE), F32)],
        compiler_params=_cparams(("arbitrary",)),
        name="router",
    )(x2, mod_l, mod_l, w_hi, w_lo, b_r, tri)


def _row_copy(src, s, dst, d, sem):
    return pltpu.make_async_copy(src.at[pl.ds(s, 1)], dst.at[pl.ds(d, 1)], sem)


DMA_WAIT_UNROLL = 32
INTRA_UNROLL = 4
GDN_SEQS = 2
ATTN_UNROLL = 8
COMBINE_ROWS = 128
FFN_BLOCKS = 2


def _issue_rows(n, make_copy):
    def body(g, _):
        base = pl.multiple_of(g * SUBLANES, SUBLANES)
        for u in range(SUBLANES):
            make_copy(base + u).start()
        return 0

    lax.fori_loop(0, n // SUBLANES, body, 0)


def _wait_rows(n, make_copy):
    def body(_, c):
        for _u in range(DMA_WAIT_UNROLL):
            make_copy(0).wait()
        return c

    lax.fori_loop(0, n // DMA_WAIT_UNROLL, body, 0)


def _dispatch_body(dest_ref, hm_ref, xs_in_ref, xs_ref, sem, *, tm):
    del xs_in_ref
    copy = lambda r: _row_copy(hm_ref, r, xs_ref, dest_ref[r], sem)
    _issue_rows(tm, copy)
    _wait_rows(tm, lambda r: _row_copy(hm_ref, 0, xs_ref, 0, sem))


def _dispatch(hm, dest, xs_buf, tm):
    T, W = hm.shape
    return pl.pallas_call(
        functools.partial(_dispatch_body, tm=tm),
        grid=(T // tm,),
        in_specs=[pl.BlockSpec((tm,), lambda i: (i,), memory_space=pltpu.SMEM),
                  pl.BlockSpec((tm, W), lambda i: (i, 0)),
                  pl.BlockSpec(memory_space=pl.ANY)],
        out_specs=pl.BlockSpec(memory_space=pl.ANY),
        out_shape=jax.ShapeDtypeStruct(xs_buf.shape, F32),
        input_output_aliases={2: 0},
        scratch_shapes=[pltpu.SemaphoreType.DMA],
        compiler_params=_cparams(("arbitrary",)),
        name="moe_dispatch",
    )(dest, hm, xs_buf)


def _ffn_body(grp_ref, ea_ref, eb_ref, used_ref, xs_ref, wg_ref, wu_ref, wd_ref, o_ref):
    i = pl.program_id(0)
    subs = range(FFN_BLOCKS)
    first = FFN_BLOCKS * i

    @pl.when(used_ref[first] != 0)
    def _():
        a = [ea_ref[first + s] for s in subs]
        b = [eb_ref[first + s] for s in subs]
        xm = [xs_ref[s * MOE_ROWS:(s + 1) * MOE_ROWS, :] for s in subs]
        x = [xm[s][:, :D_MODEL].astype(BF16) for s in subs]
        g_a = [_dot(x[s], wg_ref[0, 0, a[s]]) for s in subs]
        u_a = [_dot(x[s], wu_ref[0, 0, a[s]]) for s in subs]
        g_b = [_dot(x[s], wg_ref[0, 0, b[s]]) for s in subs]
        u_b = [_dot(x[s], wu_ref[0, 0, b[s]]) for s in subs]
        w_a = [xm[s][:, D_MODEL + 2:D_MODEL + 3] for s in subs]
        w_b = [xm[s][:, D_MODEL + 3:D_MODEL + 4] for s in subs]
        act_a = [(g_a[s] * _sigmoid(g_a[s]) * u_a[s] * w_a[s]).astype(BF16) for s in subs]
        act_b = [(g_b[s] * _sigmoid(g_b[s]) * u_b[s] * w_b[s]).astype(BF16) for s in subs]
        for s in subs:
            o_ref[s * MOE_ROWS:(s + 1) * MOE_ROWS, :] = (_dot(act_a[s], wd_ref[0, 0, a[s]])
                                                         + _dot(act_b[s], wd_ref[0, 0, b[s]]))

    @pl.when(used_ref[first] == 0)
    def _():
        o_ref[...] = jnp.zeros_like(o_ref)


def _ffn(xs, grp, ea, eb, used, w_gate, w_up, w_down, l):
    P, W = xs.shape
    D = D_MODEL
    rows = MOE_ROWS * FFN_BLOCKS
    nb = P // rows
    FF = EXPERT_FF
    E = EXPERTS_PER_GROUP
    w_up_spec = pl.BlockSpec((1, 1, E, D, FF), lambda i, g, ea, eb, u: (l, g[FFN_BLOCKS * i], 0, 0, 0))
    w_dn_spec = pl.BlockSpec((1, 1, E, FF, D), lambda i, g, ea, eb, u: (l, g[FFN_BLOCKS * i], 0, 0, 0))
    grid_spec = pltpu.PrefetchScalarGridSpec(
        num_scalar_prefetch=4,
        grid=(nb,),
        in_specs=[pl.BlockSpec((rows, W), lambda i, g, ea, eb, u: (i, 0)),
                  w_up_spec, w_up_spec, w_dn_spec],
        out_specs=pl.BlockSpec((rows, D), lambda i, g, ea, eb, u: (i, 0)),
    )
    return pl.pallas_call(
        _ffn_body,
        grid_spec=grid_spec,
        out_shape=jax.ShapeDtypeStruct((P, D), F32),
        compiler_params=_cparams(("arbitrary",)),
        name="moe_ffn",
    )(grp, ea, eb, used, xs, w_gate, w_up, w_down)


def _combine_body(dcur_ref, dnext_ref, x_ref, gt_ref, lg_ref, lb_ref, ys_ref, o_ref,
                  y0_ref, y1_ref, sems, *, tm, n_steps, alpha):
    i = pl.program_id(0)
    bufs = (y0_ref, y1_ref)

    n_chunks = tm // COMBINE_ROWS

    def finish(slot, prefetch):
        _wait_rows(tm, lambda r: _row_copy(ys_ref, 0, bufs[slot], 0, sems.at[slot]))
        gate = 1.0 + gt_ref[0]

        def chunk(c, _):
            r0 = pl.multiple_of(c * COMBINE_ROWS, COMBINE_ROWS)
            if prefetch:
                for u in range(COMBINE_ROWS):
                    _row_copy(ys_ref, dnext_ref[r0 + u], bufs[1 - slot], r0 + u, sems.at[1 - slot]).start()
            rows = pl.ds(r0, COMBINE_ROWS)
            r = alpha * x_ref[rows, :] + gate * bufs[slot][rows, :]
            o_ref[rows, :] = _layer_norm(r, lg_ref[0], lb_ref[0])
            return 0

        lax.fori_loop(0, n_chunks, chunk, 0)

    @pl.when(i == 0)
    def _():
        _issue_rows(tm, lambda r: _row_copy(ys_ref, dcur_ref[r], bufs[0], r, sems.at[0]))

    has_next = i + 1 < n_steps
    for slot in range(2):
        mine = (i % 2) == slot

        @pl.when(mine & has_next)
        def _():
            finish(slot, True)

        @pl.when(mine & jnp.logical_not(has_next))
        def _():
            finish(slot, False)


def _combine(ys, dest, x2, mod_l, ln_g, ln_b, l, S, tm, alpha):
    T, D = x2.shape
    nS = S // tm
    n_steps = T // tm
    vec = pl.BlockSpec((1, 1, D), lambda i: (l, 0, 0))
    return pl.pallas_call(
        functools.partial(_combine_body, tm=tm, n_steps=n_steps, alpha=alpha),
        grid=(n_steps,),
        in_specs=[pl.BlockSpec((tm,), lambda i: (i,), memory_space=pltpu.SMEM),
                  pl.BlockSpec((tm,), lambda i: (jnp.minimum(i + 1, n_steps - 1),), memory_space=pltpu.SMEM),
                  pl.BlockSpec((tm, D), lambda i: (i, 0)),
                  pl.BlockSpec((1, 1, D), lambda i: (i // nS, 0, 5)),
                  vec, vec,
                  pl.BlockSpec(memory_space=pl.ANY)],
        out_specs=pl.BlockSpec((tm, D), lambda i: (i, 0)),
        out_shape=jax.ShapeDtypeStruct((T, D), F32),
        scratch_shapes=[pltpu.VMEM((tm, D), F32), pltpu.VMEM((tm, D), F32), pltpu.SemaphoreType.DMA((2,))],
        compiler_params=_cparams(("arbitrary",)),
        name="moe_combine_ln",
    )(dest, dest, x2, mod_l, ln_g, ln_b, ys)


def _prep_w_in(w_in):
    L, D, _ = w_in.shape
    o = 0
    cq = w_in[..., o:o + MLA_Q_RANK]; o += MLA_Q_RANK
    ckv = w_in[..., o:o + MLA_KV_RANK]; o += MLA_KV_RANK
    kr = w_in[..., o:o + MLA_ROPE]; o += MLA_ROPE
    gq = w_in[..., o:o + GDN_WIDTH]; o += GDN_WIDTH
    gk = w_in[..., o:o + GDN_WIDTH]; o += GDN_WIDTH
    gv = w_in[..., o:o + GDN_WIDTH]; o += GDN_WIDTH
    gz = w_in[..., o:o + GDN_WIDTH]; o += GDN_WIDTH
    ga = w_in[..., o:o + GDN_HEADS]; o += GDN_HEADS
    gb = w_in[..., o:o + GDN_HEADS]; o += GDN_HEADS
    pw = w_in[..., o:o + POOL_WIDTH]
    half = MLA_ROPE // 2
    z = lambda n: jnp.zeros((L, D, n), w_in.dtype)
    kra = jnp.concatenate([z(MLA_NOPE), kr, z(HEAD_PAD - MLA_NOPE - MLA_ROPE)], axis=-1)
    krb = jnp.concatenate([z(MLA_NOPE), kr[..., half:], kr[..., :half], z(HEAD_PAD - MLA_NOPE - MLA_ROPE)], axis=-1)
    a_e = jnp.repeat(ga, GDN_DIM, axis=-1)
    b_e = jnp.repeat(gb, GDN_DIM, axis=-1)
    out = jnp.concatenate([gq, gk, gv, gz, a_e, b_e, cq, pw, ckv, kra, krb], axis=-1)
    assert out.shape[-1] == IN_COLS
    return out.astype(BF16)


def _prep_mla(w_uq, w_ukv):
    L = w_uq.shape[0]
    H, half = MLA_HEADS, MLA_ROPE // 2
    pad = HEAD_PAD - MLA_NOPE - MLA_ROPE
    q = w_uq.reshape(L, MLA_Q_RANK, H, MLA_NOPE + MLA_ROPE)
    nope, r1, r2 = q[..., :MLA_NOPE], q[..., MLA_NOPE:MLA_NOPE + half], q[..., MLA_NOPE + half:]
    zq = lambda n: jnp.zeros((L, MLA_Q_RANK, H, n), w_uq.dtype)
    plain = jnp.concatenate([nope, r1, r2, zq(pad)], axis=-1).reshape(L, MLA_Q_RANK, H * HEAD_PAD)
    partner = jnp.concatenate([zq(MLA_NOPE), r2, r1, zq(pad)], axis=-1).reshape(L, MLA_Q_RANK, H * HEAD_PAD)
    wq2 = jnp.concatenate([plain, partner], axis=-1).astype(BF16)
    kv = w_ukv.reshape(L, MLA_KV_RANK, H, MLA_NOPE + MLA_V)
    k_nope, v = kv[..., :MLA_NOPE], kv[..., MLA_NOPE:]
    zk = lambda n: jnp.zeros((L, MLA_KV_RANK, H, n), w_ukv.dtype)
    k_main = jnp.concatenate([k_nope, zk(HEAD_PAD - MLA_NOPE)], axis=-1).reshape(L, MLA_KV_RANK, H * HEAD_PAD)
    even = (jnp.arange(H) % 2 == 0)[None, None, :, None]
    v_pair = jnp.where(even, jnp.concatenate([v, zk(MLA_V)], axis=-1), jnp.concatenate([zk(MLA_V), v], axis=-1))
    wkv2 = jnp.concatenate([k_main, v_pair.reshape(L, MLA_KV_RANK, H * HEAD_PAD)], axis=-1).astype(BF16)
    return wq2, wkv2


def _rope_tables(S):
    half = MLA_ROPE // 2
    inv_freq = jnp.power(ROPE_THETA, -jnp.arange(0, MLA_ROPE, 2, dtype=F32) / MLA_ROPE)
    ang = jnp.arange(S, dtype=F32)[:, None] * inv_freq[None, :]
    cos, sin = jnp.cos(ang), jnp.sin(ang)
    pad = jnp.zeros((S, HEAD_PAD - MLA_NOPE - MLA_ROPE), F32)
    cos_t = jnp.concatenate([jnp.ones((S, MLA_NOPE), F32), cos, cos, pad], axis=-1)
    sin_t = jnp.concatenate([jnp.zeros((S, MLA_NOPE), F32), -sin, sin, pad], axis=-1)
    scale = (MLA_NOPE + MLA_ROPE) ** -0.5
    return cos_t * scale, sin_t * scale, cos_t, sin_t


def _block_diag(blocks):
    L, G, n, _ = blocks.shape
    eye = jnp.eye(G, dtype=blocks.dtype)
    return jnp.einsum('lgij,gh->lgihj', blocks, eye).reshape(L, G * n, G * n)


def kernel(x, c, w_in, mla_q_norm, mla_kv_norm, mla_w_uq, mla_w_ukv, gdn_conv, gdn_a_log, gdn_dt_bias, gdn_out_norm, pool_w, pool_scale, w_out, w_mod, b_mod, ln1_g, ln1_b, ln2_g, ln2_b, router_w_group, router_b_group, router_w_expert, router_b_expert, moe_w_gate, moe_w_up, moe_w_down):
    B, S, D = x.shape
    L = w_in.shape[0]
    T = B * S
    alpha = (2 * L) ** 0.25
    ts = min(512, S)
    tq = min(256, S)
    t_moe = min(512, S)
    t_disp = min(1024, S)
    assert D == D_MODEL and S % ts == 0 and S % tq == 0 and ts % CHUNK == 0
    assert S % t_disp == 0 and t_moe % DMA_WAIT_UNROLL == 0

    w_in2 = _prep_w_in(w_in)
    wq2, wkv2 = _prep_mla(mla_w_uq, mla_w_ukv)
    tabs = _rope_tables(S)
    gq = mla_q_norm.reshape(L, 1, MLA_Q_RANK)
    gkv = mla_kv_norm.reshape(L, 1, MLA_KV_RANK)
    alog_e = jnp.repeat(gdn_a_log, GDN_DIM, axis=-1).reshape(L, 1, GDN_WIDTH)
    dt_e = jnp.repeat(gdn_dt_bias, GDN_DIM, axis=-1).reshape(L, 1, GDN_WIDTH)
    og_e = jnp.tile(gdn_out_norm, (1, GDN_HEADS)).reshape(L, 1, GDN_WIDTH)
    lane_head = jnp.arange(GDN_WIDTH) // GDN_DIM
    seg = (lane_head[:, None] == lane_head[None, :]).astype(BF16)
    tri_c = (jnp.arange(CHUNK)[:, None] >= jnp.arange(CHUNK)[None, :]).astype(BF16)
    pool_bd = _block_diag(pool_w).astype(BF16)
    pool_sc = pool_scale.reshape(L, 1, POOL_WIDTH)
    w_out_b = w_out.astype(BF16)
    w_r = jnp.concatenate([router_w_group, router_w_expert,
                           jnp.zeros((L, D, LANE - N_GROUPS - N_EXPERTS), F32)], axis=-1)
    w_r_hi = w_r.astype(BF16)
    w_r_lo = (w_r - w_r_hi.astype(F32)).astype(BF16)
    b_r = jnp.concatenate([router_b_group, router_b_expert,
                           jnp.zeros((L, LANE - N_GROUPS - N_EXPERTS), F32)], axis=-1).reshape(L, 1, LANE)
    tri_r = (jnp.arange(ts)[:, None] > jnp.arange(ts)[None, :]).astype(BF16)
    ln1g, ln1b = ln1_g.reshape(L, 1, D), ln1_b.reshape(L, 1, D)
    ln2g, ln2b = ln2_g.reshape(L, 1, D), ln2_b.reshape(L, 1, D)

    assert N_BUCKETS <= LANE
    step_rows = MOE_ROWS * FFN_BLOCKS
    max_rows = T + N_BUCKETS * (MOE_ROWS - 1) + N_GROUPS * (step_rows - MOE_ROWS)
    P = (max_rows + step_rows - 1) // step_rows * step_rows
    nb = P // MOE_ROWS
    pg, pa, pb = [], [], []
    for g_ in range(N_GROUPS):
        for a_ in range(EXPERTS_PER_GROUP):
            for b_ in range(a_ + 1, EXPERTS_PER_GROUP):
                pg.append(g_)
                pa.append(a_)
                pb.append(b_)
    bucket_g = jnp.asarray(pg, jnp.int32)
    bucket_a = jnp.asarray(pa, jnp.int32)
    bucket_b = jnp.asarray(pb, jnp.int32)
    grouped = lambda w: w.astype(BF16).reshape((L, N_GROUPS, EXPERTS_PER_GROUP) + w.shape[2:])
    wg_b, wu_b, wd_b = grouped(moe_w_gate), grouped(moe_w_up), grouped(moe_w_down)
    bucket_ids = jnp.arange(N_BUCKETS, dtype=jnp.int32)
    blk0 = jnp.arange(nb, dtype=jnp.int32) * MOE_ROWS
    xs = jnp.zeros((P, D + LANE), F32)

    mod = _modulation(c, w_mod, b_mod)
    x2 = x.reshape(T, D)
    for l in range(L):
        mod_l = mod[l].reshape(B, 1, 6 * D)
        proj = _inproj(x2, mod_l, w_in2, l, S, ts)
        q, k, v = _mla_proj(proj, tabs, gq, gkv, wq2, wkv2, l, S, ts)
        y_mla = _attention(q, k, v, B, S, tq)
        y_gdn = _gdn(proj, gdn_conv, alog_e, dt_e, og_e, seg, tri_c, l, B, S, ts)
        y_pool = _pool(proj, pool_bd, pool_sc, l, B, S, ts)
        x2 = _outproj(y_mla, y_gdn, y_pool, w_out_b, x2, mod_l, ln1g, ln1b, l, S, ts, alpha)

        hm, cnt = _router(x2, mod_l, w_r_hi, w_r_lo, b_r, tri_r, l, S, ts)
        counts = cnt[0, :N_BUCKETS].astype(jnp.int32)
        padded = ((counts + MOE_ROWS - 1) // MOE_ROWS * MOE_ROWS).reshape(N_GROUPS, PAIRS_PER_GROUP)
        group_rows = jnp.sum(padded, axis=1)
        group_pad = (step_rows - group_rows % step_rows) % step_rows
        padded = padded.at[:, PAIRS_PER_GROUP - 1].add(group_pad).reshape(N_BUCKETS)
        pad_end = jnp.cumsum(padded)
        pad_start = pad_end - padded
        bucket = hm[:, D].astype(jnp.int32)
        rank = hm[:, D + 1].astype(jnp.int32)
        dest = jnp.sum(jnp.where(bucket[:, None] == bucket_ids[None, :], pad_start[None, :], 0), axis=1) + rank
        block_bucket = jnp.minimum(jnp.sum((pad_end[None, :] <= blk0[:, None]).astype(jnp.int32), axis=1),
                                   N_BUCKETS - 1)
        onehot_b = (block_bucket[:, None] == bucket_ids[None, :]).astype(jnp.int32)
        grp = jnp.sum(onehot_b * bucket_g[None, :], axis=1)
        ea = jnp.sum(onehot_b * bucket_a[None, :], axis=1)
        eb = jnp.sum(onehot_b * bucket_b[None, :], axis=1)
        used = (blk0 < pad_end[-1]).astype(jnp.int32)

        xs = _dispatch(hm, dest, xs, t_disp)
        ys = _ffn(xs, grp, ea, eb, used, wg_b, wu_b, wd_b, l)
        x2 = _combine(ys, dest, x2, mod_l, ln2g, ln2b, l, S, t_moe, alpha)
    return x2.reshape(B, S, D)
```

```python
import functools
import math

import jax
import jax.numpy as jnp
from jax import lax
from jax.experimental import pallas as pl
from jax.experimental.pallas import tpu as pltpu

F32 = jnp.float32
BF16 = jnp.bfloat16

D_MODEL = 1024
HALF_D = D_MODEL // 2
CHUNK = 64
MLA_HEADS = 8
MLA_NOPE = 64
MLA_ROPE = 32
MLA_V = 64
MLA_Q_RANK = 256
MLA_KV_RANK = 128
MLA_WIDTH = MLA_HEADS * MLA_V
ROPE_THETA = 10000.0
GDN_HEADS = 4
GDN_DIM = 64
GDN_WIDTH = GDN_HEADS * GDN_DIM
GDN_CONV = 4
POOL_WIDTH = 256
POOL_WINDOWS = (2, 4, 8, 16)
POOL_GROUP_DIM = 64
N_GROUPS = 4
EXPERTS_PER_GROUP = 8
N_EXPERTS = N_GROUPS * EXPERTS_PER_GROUP
TOP_K = 2
EXPERT_FF = 256
LN_EPS = 1e-5
RMS_EPS = 1e-6

LANE = 128
SUBLANES = 8
HEAD_PAD = 128
PAIRS_PER_GROUP = EXPERTS_PER_GROUP * (EXPERTS_PER_GROUP - 1) // 2
N_BUCKETS = N_GROUPS * PAIRS_PER_GROUP
MOE_ROWS = 128
VMEM_LIMIT = 48 * 1024 * 1024

_C_GQ, _C_GK, _C_GV, _C_GZ = 0, 256, 512, 768
_C_A, _C_B = 1024, 1280
_C_CQ, _C_POOL = 1536, 1792
_C_CKV, _C_KRA, _C_KRB = 2048, 2176, 2304
IN_COLS = 2432


def _cparams(sem):
    return pltpu.CompilerParams(dimension_semantics=sem, vmem_limit_bytes=VMEM_LIMIT)


def _sigmoid(x):
    return 1.0 / (1.0 + jnp.exp(-x))


def _split2(x):
    hi = x.astype(BF16)
    lo = (x - hi.astype(F32)).astype(BF16)
    return hi, lo


def _split3(x):
    hi = x.astype(BF16)
    r = x - hi.astype(F32)
    mid = r.astype(BF16)
    lo = (r - mid.astype(F32)).astype(BF16)
    return hi, mid, lo


def _pack_bf16_pairs(x):
    n = x.shape[1] // 2
    lo = lax.bitcast_convert_type(x[:, :n].astype(BF16).astype(F32), jnp.uint32)
    hi = lax.bitcast_convert_type(x[:, n:].astype(BF16).astype(F32), jnp.uint32)
    return (hi & jnp.uint32(0xFFFF0000)) | (lo >> 16)


def _unpack_bf16_pairs(p):
    lo = lax.bitcast_convert_type(p << 16, F32)
    hi = lax.bitcast_convert_type(p & jnp.uint32(0xFFFF0000), F32)
    return jnp.concatenate([lo, hi], axis=1)


def _dot(a, b):
    return jnp.dot(a, b, preferred_element_type=F32)


def _dot_nt(a, b):
    return lax.dot_general(a, b, (((1,), (1,)), ((), ())), preferred_element_type=F32)


def _dot_tn(a, b):
    return lax.dot_general(a, b, (((0,), (0,)), ((), ())), preferred_element_type=F32)


def _mod_body(c_ref, w_ref, b_ref, o_ref):
    c = c_ref[...]
    ca = c * _sigmoid(c)
    o_ref[0] = _dot(ca.astype(BF16), w_ref[0].astype(BF16)) + b_ref[0]


def _modulation(c, w_mod, b_mod):
    L, D, N = w_mod.shape
    B = c.shape[0]
    tn = 1024
    return pl.pallas_call(
        _mod_body,
        grid=(L, N // tn),
        in_specs=[pl.BlockSpec((B, D), lambda l, j: (0, 0)),
                  pl.BlockSpec((1, D, tn), lambda l, j: (l, 0, j)),
                  pl.BlockSpec((1, 1, tn), lambda l, j: (l, 0, j))],
        out_specs=pl.BlockSpec((1, B, tn), lambda l, j: (l, 0, j)),
        out_shape=jax.ShapeDtypeStruct((L, B, N), F32),
        compiler_params=_cparams(("arbitrary", "arbitrary")),
        name="modulation",
    )(c, w_mod, b_mod.reshape(L, 1, N))


def _inproj_body(x_ref, sh_ref, sc_ref, w_ref, o_ref):
    h = x_ref[...] * (1.0 + sc_ref[0]) + sh_ref[0]
    o_ref[...] = _dot(h.astype(BF16), w_ref[0])


def _inproj(x2, mod_l, w_in2, l, S, tm):
    T, D = x2.shape
    nS = S // tm
    return pl.pallas_call(
        _inproj_body,
        grid=(T // tm,),
        in_specs=[pl.BlockSpec((tm, D), lambda i: (i, 0)),
                  pl.BlockSpec((1, 1, D), lambda i: (i // nS, 0, 0)),
                  pl.BlockSpec((1, 1, D), lambda i: (i // nS, 0, 1)),
                  pl.BlockSpec((1, D, IN_COLS), lambda i: (l, 0, 0))],
        out_specs=pl.BlockSpec((tm, IN_COLS), lambda i: (i, 0)),
        out_shape=jax.ShapeDtypeStruct((T, IN_COLS), F32),
        compiler_params=_cparams(("arbitrary",)),
        name="inproj",
    )(x2, mod_l, mod_l, w_in2)


def _mla_proj_body(cq_ref, ckv_ref, kra_ref, krb_ref, cosq_ref, sinq_ref, cosk_ref, sink_ref,
                   gq_ref, gkv_ref, wq_ref, wkv_ref, q_out, k_out, v_out):
    cq = cq_ref[...]
    qn = cq * lax.rsqrt(jnp.mean(cq * cq, axis=-1, keepdims=True) + RMS_EPS) * gq_ref[0]
    q2 = _dot(qn.astype(BF16), wq_ref[0])
    ckv = ckv_ref[...]
    kvn = ckv * lax.rsqrt(jnp.mean(ckv * ckv, axis=-1, keepdims=True) + RMS_EPS) * gkv_ref[0]
    kv2 = _dot(kvn.astype(BF16), wkv_ref[0])
    cq_t, sq_t = cosq_ref[...], sinq_ref[...]
    krope = kra_ref[...] * cosk_ref[...] + krb_ref[...] * sink_ref[...]
    hw = MLA_HEADS * HEAD_PAD
    for h in range(MLA_HEADS):
        a, b = h * HEAD_PAD, (h + 1) * HEAD_PAD
        q_out[:, a:b] = (q2[:, a:b] * cq_t + q2[:, hw + a:hw + b] * sq_t).astype(BF16)
        k_out[:, a:b] = (kv2[:, a:b] + krope).astype(BF16)
    vl = lax.broadcasted_iota(jnp.int32, (1, hw), 1)
    ones_lane = jnp.where((vl // HEAD_PAD) % 2 == 0, MLA_V, 0)
    v_out[...] = (kv2[:, hw:] + (vl % HEAD_PAD == ones_lane).astype(F32)).astype(BF16)


def _mla_proj(proj, tabs, gq, gkv, wq2, wkv2, l, S, ts):
    T = proj.shape[0]
    nS = S // ts
    hw = MLA_HEADS * HEAD_PAD
    tab_spec = pl.BlockSpec((ts, LANE), lambda i: (i % nS, 0))
    return pl.pallas_call(
        _mla_proj_body,
        grid=(T // ts,),
        in_specs=[pl.BlockSpec((ts, 256), lambda i: (i, _C_CQ // 256)),
                  pl.BlockSpec((ts, 128), lambda i: (i, _C_CKV // 128)),
                  pl.BlockSpec((ts, 128), lambda i: (i, _C_KRA // 128)),
                  pl.BlockSpec((ts, 128), lambda i: (i, _C_KRB // 128)),
                  tab_spec, tab_spec, tab_spec, tab_spec,
                  pl.BlockSpec((1, 1, MLA_Q_RANK), lambda i: (l, 0, 0)),
                  pl.BlockSpec((1, 1, MLA_KV_RANK), lambda i: (l, 0, 0)),
                  pl.BlockSpec((1, MLA_Q_RANK, 2 * hw), lambda i: (l, 0, 0)),
                  pl.BlockSpec((1, MLA_KV_RANK, 2 * hw), lambda i: (l, 0, 0))],
        out_specs=[pl.BlockSpec((ts, hw), lambda i: (i, 0))] * 3,
        out_shape=[jax.ShapeDtypeStruct((T, hw), BF16)] * 3,
        compiler_params=_cparams(("arbitrary",)),
        name="mla_proj",
    )(proj, proj, proj, proj, *tabs, gq, gkv, wq2, wkv2)


def _attn_body(qi_tab, kj_tab, q_ref, k_ref, v_ref, o_ref, m_s, acc_s, *, tq, n_q, n_tiles):
    row_c = lax.broadcasted_iota(jnp.int32, (tq, tq), 0) // CHUNK
    col_c = lax.broadcasted_iota(jnp.int32, (tq, tq), 1) // CHUNK
    allowed = col_c <= row_c
    lane = lax.broadcasted_iota(jnp.int32, (tq, HEAD_PAD), 1)
    sl = [slice(hh * HEAD_PAD, (hh + 1) * HEAD_PAD) for hh in range(2)]
    hs = range(2)

    def rows_of(tile):
        return pl.ds(pl.multiple_of(tile * tq, tq), tq)

    def stage_a(t):
        q = q_ref[rows_of(qi_tab[t]), :]
        kk = k_ref[rows_of(kj_tab[t]), :]
        return tuple(_dot_nt(q[:, sl[hh]], kk[:, sl[hh]]).astype(BF16) for hh in hs)

    def stage_b(t, s_pair):
        qi, kj = qi_tab[t], kj_tab[t]
        rows = rows_of(qi)
        keep = jnp.logical_or(allowed, kj < qi)
        neg = jnp.asarray(-jnp.inf, BF16)
        s = [jnp.where(keep, s_pair[hh], neg) for hh in hs]
        m_prev = [m_s[hh, rows, :] for hh in hs]
        m_new = [jnp.maximum(m_prev[hh], jnp.max(s[hh], axis=-1, keepdims=True)) for hh in hs]
        p = tuple(jnp.exp(s[hh] - jnp.concatenate([m_new[hh]] * (tq // HEAD_PAD), axis=1)) for hh in hs)
        alpha = tuple(jnp.exp(m_prev[hh] - m_new[hh]) for hh in hs)
        for hh in hs:
            m_s[hh, rows, :] = m_new[hh]
        return p, alpha

    def stage_c(t, p, alpha):
        qi, kj = qi_tab[t], kj_tab[t]
        rows = rows_of(qi)
        vv = v_ref[rows_of(kj), :]
        pv = [_dot(p[hh], vv[:, sl[hh]]) for hh in hs]
        for hh in hs:
            acc_s[hh, rows, :] = alpha[hh].astype(F32) * acc_s[hh, rows, :] + pv[hh]

    def body(t, c):
        s_next, (p, alpha) = c
        s_after = stage_a(t + 2)
        stage_c(t, p, alpha)
        return s_after, stage_b(t + 1, s_next)

    m_s[...] = jnp.full(m_s.shape, -jnp.inf, m_s.dtype)
    acc_s[...] = jnp.zeros_like(acc_s)
    pb0 = stage_b(0, stage_a(0))
    s_last, pb = lax.fori_loop(0, n_tiles - 2, body, (stage_a(1), pb0), unroll=ATTN_UNROLL)
    pb_last = stage_b(n_tiles - 1, s_last)
    stage_c(n_tiles - 2, *pb)
    stage_c(n_tiles - 1, *pb_last)

    def normalize(i, _):
        rows = rows_of(i)
        acc0, acc1 = acc_s[0, rows, :], acc_s[1, rows, :]
        o0 = jnp.where(lane < MLA_V, acc0 * (1.0 / acc0[:, MLA_V:MLA_V + 1]), 0.0)
        o1 = jnp.where(lane >= MLA_V, acc1 * (1.0 / acc1[:, 0:1]), 0.0)
        o_ref[rows, :] = (o0 + o1).astype(BF16)
        return 0

    lax.fori_loop(0, n_q, normalize, 0)


def _attention(q, k, v, B, S, tq):
    T = q.shape[0]
    nq = S // tq
    pairs = MLA_HEADS // 2
    tiles = [(i, j) for i in range(nq) for j in range(i + 1)]
    assert len(tiles) >= 2
    qi_tab = jnp.asarray([t[0] for t in tiles], jnp.int32)
    kj_tab = jnp.asarray([t[1] for t in tiles], jnp.int32)
    seq = pl.BlockSpec((S, 2 * HEAD_PAD), lambda b, p, qt, kt: (b, p))
    grid_spec = pltpu.PrefetchScalarGridSpec(
        num_scalar_prefetch=2,
        grid=(B, pairs),
        in_specs=[seq, seq, seq],
        out_specs=pl.BlockSpec((S, 2 * MLA_V), lambda b, p, qt, kt: (b, p)),
        scratch_shapes=[pltpu.VMEM((2, S, HEAD_PAD), BF16), pltpu.VMEM((2, S, HEAD_PAD), F32)],
    )
    return pl.pallas_call(
        functools.partial(_attn_body, tq=tq, n_q=nq, n_tiles=len(tiles)),
        grid_spec=grid_spec,
        out_shape=jax.ShapeDtypeStruct((T, MLA_WIDTH), BF16),
        compiler_params=_cparams(("arbitrary", "arbitrary")),
        name="mla_attention",
    )(qi_tab, kj_tab, q, k, v)


def _pool_body(p_ref, w_ref, sc_ref, o_ref, prev_ref, *, ts):
    j = pl.program_id(1)
    hist = 16

    @pl.when(j == 0)
    def _():
        prev_ref[...] = jnp.zeros_like(prev_ref)

    cur = p_ref[...]
    x = jnp.concatenate([prev_ref[...], cur], axis=0)
    s1 = x + pltpu.roll(x, 1, axis=0)
    s2 = s1 + pltpu.roll(s1, 2, axis=0)
    s4 = s2 + pltpu.roll(s2, 4, axis=0)
    s8 = s4 + pltpu.roll(s4, 8, axis=0)
    pos = (j * ts + lax.broadcasted_iota(jnp.int32, (ts, POOL_WIDTH), 0) + 1).astype(F32)
    lane = lax.broadcasted_iota(jnp.int32, (ts, POOL_WIDTH), 1)
    sums = (s1, s2, s4, s8)
    mean = None
    for gi, win in enumerate(POOL_WINDOWS):
        m_g = sums[gi][hist:] / jnp.minimum(pos, float(win))
        mean = m_g if mean is None else jnp.where(lane >= gi * POOL_GROUP_DIM, m_g, mean)
    delta = mean - cur
    o_ref[...] = (_dot(delta.astype(BF16), w_ref[0]) * sc_ref[0]).astype(BF16)
    prev_ref[...] = cur[ts - hist:]


def _pool(proj, w_bd, scale, l, B, S, ts):
    T = proj.shape[0]
    nS = S // ts
    return pl.pallas_call(
        functools.partial(_pool_body, ts=ts),
        grid=(B, nS),
        in_specs=[pl.BlockSpec((ts, POOL_WIDTH), lambda b, j: (b * nS + j, _C_POOL // 256)),
                  pl.BlockSpec((1, POOL_WIDTH, POOL_WIDTH), lambda b, j: (l, 0, 0)),
                  pl.BlockSpec((1, 1, POOL_WIDTH), lambda b, j: (l, 0, 0))],
        out_specs=pl.BlockSpec((ts, POOL_WIDTH), lambda b, j: (b * nS + j, 0)),
        out_shape=jax.ShapeDtypeStruct((T, POOL_WIDTH), BF16),
        scratch_shapes=[pltpu.VMEM((16, POOL_WIDTH), F32)],
        compiler_params=_cparams(("arbitrary", "arbitrary")),
        name="pool_mixer",
    )(proj, w_bd, scale)


def _head_sum(x, seg_ref):
    hi, lo = _split2(x)
    return _dot(hi, seg_ref[...]) + _dot(lo, seg_ref[...])


def _gdn_inputs(s, raw_q, raw_k, raw_v, raw_a, raw_b, conv_ref, alog_ref, dt_ref, seg_ref,
                q_ref, k_ref, v_ref, g_ref, b_ref, prev_ref, ts):
    hist = 8
    cw = conv_ref[0]
    outs = []
    for idx, ref in enumerate((raw_q, raw_k, raw_v)):
        cur = ref[s]
        a, b = idx * GDN_WIDTH, (idx + 1) * GDN_WIDTH
        x = jnp.concatenate([prev_ref[s, :, a:b], cur], axis=0)
        w = cw[:, a:b]
        y = (w[3:4] * x + w[2:3] * pltpu.roll(x, 1, axis=0) + w[1:2] * pltpu.roll(x, 2, axis=0)
             + w[0:1] * pltpu.roll(x, 3, axis=0))[hist:]
        outs.append(y * _sigmoid(y))
        prev_ref[s, :, a:b] = cur[ts - hist:]
    qc, kc, vc = outs
    q_ref[s] = qc * lax.rsqrt(_head_sum(qc * qc, seg_ref) + RMS_EPS) * (GDN_DIM ** -0.5)
    k_ref[s] = kc * lax.rsqrt(_head_sum(kc * kc, seg_ref) + RMS_EPS)
    v_ref[s] = vc
    z = raw_a[s] + dt_ref[0]
    softplus = jnp.maximum(z, 0.0) + jnp.log(1.0 + jnp.exp(-jnp.abs(z)))
    g_ref[s] = -jnp.exp(alog_ref[0]) * softplus
    b_ref[s] = _sigmoid(raw_b[s])


def _gdn_body(raw_q, raw_k, raw_v, raw_a, raw_b, z_ref, conv_ref, alog_ref, dt_ref, og_ref, seg_ref, tri_ref,
              o_ref, state_ref, prev_ref, q_ref, k_ref, v_ref, g_ref, b_ref,
              u_s, w_s, qk_s, qd_s, kd_s, gl_s, *, ts):
    j = pl.program_id(1)
    C, W, H = CHUNK, GDN_WIDTH, GDN_HEADS
    n_chunks = ts // C
    groups = n_chunks // INTRA_UNROLL

    @pl.when(j == 0)
    def _():
        state_ref[...] = jnp.zeros_like(state_ref)
        prev_ref[...] = jnp.zeros_like(prev_ref)

    for s in range(GDN_SEQS):
        _gdn_inputs(s, raw_q, raw_k, raw_v, raw_a, raw_b, conv_ref, alog_ref, dt_ref, seg_ref,
                    q_ref, k_ref, v_ref, g_ref, b_ref, prev_ref, ts)

    lane = lax.broadcasted_iota(jnp.int32, (C, W), 1)
    row = lax.broadcasted_iota(jnp.int32, (C, W), 0)
    col_tok = lane % C
    incl = row >= col_tok
    strict = row > col_tok
    eye = (row == col_tok).astype(F32)
    head_masks = [(lane // C == h).astype(F32) for h in range(H)]
    bd_mask = (lax.broadcasted_iota(jnp.int32, (W, W), 0) // C
               == lax.broadcasted_iota(jnp.int32, (W, W), 1) // C)

    def expand(m):
        return jnp.concatenate([m * hm for hm in head_masks], axis=0)

    def bdot(lhs, rhs):
        return _dot(lhs.astype(BF16), rhs.astype(BF16))

    def intra(i, _):
        sq = i // groups
        cs = [(i % groups) * INTRA_UNROLL + uu for uu in range(INTRA_UNROLL)]
        r0s = [pl.multiple_of(c * C, C) for c in cs]
        n = range(INTRA_UNROLL)
        tri = tri_ref[...]
        gs = [_split3(g_ref[sq, pl.ds(r0, C), :]) for r0 in r0s]
        gc = [_dot(tri, g[0]) + _dot(tri, g[1]) + _dot(tri, g[2]) for g in gs]
        ks = [k_ref[sq, pl.ds(r0, C), :] for r0 in r0s]
        qs = [q_ref[sq, pl.ds(r0, C), :] for r0 in r0s]
        betas = [b_ref[sq, pl.ds(r0, C), :] for r0 in r0s]
        kb = [ks[x] * betas[x] for x in n]
        aq = [_dot_nt(jnp.concatenate([kb[x], qs[x]], axis=0).astype(BF16), expand(ks[x]).astype(BF16)) for x in n]
        g_row = [jnp.sum(gc[x] * eye, axis=0, keepdims=True) for x in n]
        g_last = [gc[x][C - 1:C, :] for x in n]
        decay = [jnp.where(incl, jnp.exp(jnp.where(incl, gc[x] - g_row[x], 0.0)), 0.0) for x in n]
        e_gc = [jnp.exp(gc[x]) for x in n]
        a_cat = [jnp.where(strict, aq[x][:C] * decay[x], 0.0) for x in n]
        for x in n:
            r0 = r0s[x]
            qk_s[sq, pl.ds(r0, C), :] = jnp.where(incl, aq[x][C:] * decay[x], 0.0).astype(BF16)
            qd_s[sq, pl.ds(r0, C), :] = (qs[x] * e_gc[x]).astype(BF16)
            kd_s[sq, pl.ds(r0, C), :] = (ks[x] * jnp.exp(g_last[x] - gc[x])).astype(BF16)
            gl_s[sq, pl.ds(pl.multiple_of(cs[x] * 8, 8), 8), :] = jnp.broadcast_to(jnp.exp(g_last[x]), (8, W))
        x_cat = [eye - a_cat[x] for x in n]
        p_cat = [bdot(a_cat[x], expand(a_cat[x])) for x in n]
        n_fac = int(math.log2(C)) - 1
        for r in range(n_fac):
            last = r == n_fac - 1
            xp = [bdot(x_cat[x] if last else jnp.concatenate([x_cat[x], p_cat[x]], axis=0), expand(p_cat[x]))
                  for x in n]
            x_cat = [x_cat[x] + xp[x][:C] for x in n]
            if not last:
                p_cat = [xp[x][C:] for x in n]
        for x in n:
            r0 = r0s[x]
            t_cat = x_cat[x].astype(BF16)
            v = v_ref[sq, pl.ds(r0, C), :]
            u_s[sq, pl.ds(r0, C), :] = _dot(t_cat, expand(v * betas[x]).astype(BF16))
            w_s[sq, pl.ds(r0, C), :] = _dot(t_cat, expand(kb[x] * e_gc[x]).astype(BF16)).astype(BF16)
        return 0

    lax.fori_loop(0, GDN_SEQS * groups, intra, 0)

    def scan(c, _):
        r0 = pl.multiple_of(c * C, C)
        sq = range(GDN_SEQS)
        state = [state_ref[s] for s in sq]
        wq = [_dot(jnp.concatenate([w_s[s, pl.ds(r0, C), :], qd_s[s, pl.ds(r0, C), :]], axis=0),
                   state[s].astype(BF16)) for s in sq]
        v_new = [u_s[s, pl.ds(r0, C), :] - wq[s][:C] for s in sq]
        upd = [_dot_tn(kd_s[s, pl.ds(r0, C), :], v_new[s].astype(BF16)) for s in sq]
        for s in sq:
            g_l = gl_s[s, pl.ds(pl.multiple_of(c * 8, 8), 1), :]
            state_ref[s] = state[s] * g_l + jnp.where(bd_mask, upd[s], 0.0)
        o = [wq[s][C:] + _dot(qk_s[s, pl.ds(r0, C), :], expand(v_new[s]).astype(BF16)) for s in sq]
        ms = [_head_sum(o[s] * o[s], seg_ref) * (1.0 / GDN_DIM) for s in sq]
        for s in sq:
            z = z_ref[s, pl.ds(r0, C), :]
            y = o[s] * lax.rsqrt(ms[s] + RMS_EPS) * og_ref[0] * (z * _sigmoid(z))
            o_ref[s, pl.ds(r0, C), :] = y.astype(BF16)
        return 0

    lax.fori_loop(0, n_chunks, scan, 0, unroll=2)


def _gdn(proj, conv_w, alog_e, dt_e, og, seg, tri, l, B, S, ts):
    T = proj.shape[0]
    nS = S // ts
    W = GDN_WIDTH
    Q = GDN_SEQS
    proj3 = proj.reshape(B, S, proj.shape[-1])

    def col(c):
        return pl.BlockSpec((Q, ts, W), lambda b, j: (b, j, c // W))

    vec = pl.BlockSpec((1, 1, W), lambda b, j: (l, 0, 0))
    seq_f32 = pltpu.VMEM((Q, ts, W), F32)
    out = pl.pallas_call(
        functools.partial(_gdn_body, ts=ts),
        grid=(B // Q, nS),
        in_specs=[col(_C_GQ), col(_C_GK), col(_C_GV), col(_C_A), col(_C_B), col(_C_GZ),
                  pl.BlockSpec((1, GDN_CONV, 3 * W), lambda b, j: (l, 0, 0)),
                  vec, vec, vec,
                  pl.BlockSpec((W, W), lambda b, j: (0, 0)),
                  pl.BlockSpec((CHUNK, CHUNK), lambda b, j: (0, 0))],
        out_specs=pl.BlockSpec((Q, ts, W), lambda b, j: (b, j, 0)),
        out_shape=jax.ShapeDtypeStruct((B, S, W), BF16),
        scratch_shapes=[pltpu.VMEM((Q, W, W), F32), pltpu.VMEM((Q, 8, 3 * W), F32)]
        + [seq_f32] * 5 + [seq_f32] + [pltpu.VMEM((Q, ts, W), BF16)] * 4
        + [pltpu.VMEM((Q, 8 * (ts // CHUNK), W), F32)],
        compiler_params=_cparams(("arbitrary", "arbitrary")),
        name="gdn_delta_rule",
    )(proj3, proj3, proj3, proj3, proj3, proj3, conv_w, alog_e, dt_e, og, seg, tri)
    return out.reshape(T, W)


def _layer_norm(r, g, b):
    mu = jnp.mean(r, axis=-1, keepdims=True)
    d = r - mu
    var = jnp.mean(d * d, axis=-1, keepdims=True)
    return d * lax.rsqrt(var + LN_EPS) * g + b


def _outproj_body(ym_ref, yg_ref, yp_ref, w_ref, x_ref, gt_ref, lg_ref, lb_ref, o_ref, *, alpha):
    w = w_ref[0]
    y = (_dot(ym_ref[...], w[:MLA_WIDTH]) + _dot(yg_ref[...], w[MLA_WIDTH:MLA_WIDTH + GDN_WIDTH])
         + _dot(yp_ref[...], w[MLA_WIDTH + GDN_WIDTH:]))
    r = alpha * x_ref[...] + (1.0 + gt_ref[0]) * y
    o_ref[...] = _layer_norm(r, lg_ref[0], lb_ref[0])


def _outproj(y_mla, y_gdn, y_pool, w_out, x2, mod_l, ln_g, ln_b, l, S, tm, alpha):
    T, D = x2.shape
    nS = S // tm
    vec = pl.BlockSpec((1, 1, D), lambda i: (l, 0, 0))
    return pl.pallas_call(
        functools.partial(_outproj_body, alpha=alpha),
        grid=(T // tm,),
        in_specs=[pl.BlockSpec((tm, MLA_WIDTH), lambda i: (i, 0)),
                  pl.BlockSpec((tm, GDN_WIDTH), lambda i: (i, 0)),
                  pl.BlockSpec((tm, POOL_WIDTH), lambda i: (i, 0)),
                  pl.BlockSpec((1, D, D), lambda i: (l, 0, 0)),
                  pl.BlockSpec((tm, D), lambda i: (i, 0)),
                  pl.BlockSpec((1, 1, D), lambda i: (i // nS, 0, 2)),
                  vec, vec],
        out_specs=pl.BlockSpec((tm, D), lambda i: (i, 0)),
        out_shape=jax.ShapeDtypeStruct((T, D), F32),
        compiler_params=_cparams(("arbitrary",)),
        name="outproj_ln",
    )(y_mla, y_gdn, y_pool, w_out, x2, mod_l, ln_g, ln_b)


def _lane_first(cond, lane_f):
    return jnp.min(jnp.where(cond, lane_f, float(LANE)), axis=-1, keepdims=True)


def _router_body(x_ref, sh_ref, sc_ref, whi_ref, wlo_ref, br_ref, tri_ref,
                 hm_out, cnt_out, carry_ref):
    i = pl.program_id(0)

    @pl.when(i == 0)
    def _():
        carry_ref[...] = jnp.zeros_like(carry_ref)

    h = x_ref[...] * (1.0 + sc_ref[0]) + sh_ref[0]
    hm_out[:, :HALF_D] = _pack_bf16_pairs(h)
    h_hi, h_lo = _split2(h)
    logits = _dot(h_hi, whi_ref[0]) + _dot(h_lo, whi_ref[0]) + _dot(h_hi, wlo_ref[0]) + br_ref[0]
    tm = logits.shape[0]
    lane = lax.broadcasted_iota(jnp.int32, (tm, LANE), 1)
    lane_f = lane.astype(F32)
    neg = -jnp.inf
    gl = jnp.where(lane < N_GROUPS, logits, neg)
    gmax = jnp.max(gl, axis=-1, keepdims=True)
    gsel = _lane_first(gl == gmax, lane_f)
    g_p = 1.0 / jnp.sum(jnp.exp(gl - gmax), axis=-1, keepdims=True)
    lo = N_GROUPS + EXPERTS_PER_GROUP * gsel
    el = jnp.where((lane_f >= lo) & (lane_f < lo + EXPERTS_PER_GROUP), logits, neg)
    m1 = jnp.max(el, axis=-1, keepdims=True)
    i1 = _lane_first(el == m1, lane_f)
    el2 = jnp.where(lane_f == i1, neg, el)
    m2 = jnp.max(el2, axis=-1, keepdims=True)
    i2 = _lane_first(el2 == m2, lane_f)
    t = jnp.exp(m2 - m1)
    w1 = g_p / (1.0 + t)
    w2 = g_p * t / (1.0 + t)
    loc1 = i1 - lo
    loc2 = i2 - lo
    a_loc = jnp.minimum(loc1, loc2)
    b_loc = jnp.maximum(loc1, loc2)
    pair = a_loc * (2 * EXPERTS_PER_GROUP - 1 - a_loc) * 0.5 + (b_loc - a_loc - 1.0)
    bucket = gsel * PAIRS_PER_GROUP + pair
    first_is_a = loc1 < loc2
    w_a = jnp.where(first_is_a, w1, w2)
    w_b = jnp.where(first_is_a, w2, w1)
    hit = lane_f == bucket
    onehot = hit.astype(BF16)
    before = _dot(tri_ref[...], onehot) + carry_ref[0:1, :]
    rank = jnp.sum(jnp.where(hit, before, 0.0), axis=-1, keepdims=True)
    total = carry_ref[0:1, :] + jnp.sum(onehot.astype(F32), axis=0, keepdims=True)
    carry_ref[...] = jnp.broadcast_to(total, carry_ref.shape)
    cnt_out[...] = jnp.broadcast_to(total, cnt_out.shape)
    meta = jnp.zeros((tm, LANE), F32)
    for idx, val in enumerate((bucket, rank, w_a, w_b)):
        meta = jnp.where(lane == idx, val, meta)
    hm_out[:, HALF_D:] = lax.bitcast_convert_type(meta, jnp.uint32)


def _router(x2, mod_l, w_hi, w_lo, b_r, tri, l, S, tm):
    T, D = x2.shape
    nS = S // tm
    return pl.pallas_call(
        _router_body,
        grid=(T // tm,),
        in_specs=[pl.BlockSpec((tm, D), lambda i: (i, 0)),
                  pl.BlockSpec((1, 1, D), lambda i: (i // nS, 0, 3)),
                  pl.BlockSpec((1, 1, D), lambda i: (i // nS, 0, 4)),
                  pl.BlockSpec((1, D, LANE), lambda i: (l, 0, 0)),
                  pl.BlockSpec((1, D, LANE), lambda i: (l, 0, 0)),
                  pl.BlockSpec((1, 1, LANE), lambda i: (l, 0, 0)),
                  pl.BlockSpec((tm, tm), lambda i: (0, 0))],
        out_specs=[pl.BlockSpec((tm, HALF_D + LANE), lambda i: (i, 0)),
                   pl.BlockSpec((8, LANE), lambda i: (0, 0))],
        out_shape=[jax.ShapeDtypeStruct((T, HALF_D + LANE), jnp.uint32),
                   jax.ShapeDtypeStruct((8, LANE), F32)],
        scratch_shapes=[pltpu.VMEM((8, LANE), F32)],
        compiler_params=_cparams(("arbitrary",)),
        name="router",
    )(x2, mod_l, mod_l, w_hi, w_lo, b_r, tri)


def _row_copy(src, s, dst, d, sem):
    return pltpu.make_async_copy(src.at[pl.ds(s, 1)], dst.at[pl.ds(d, 1)], sem)


DMA_WAIT_UNROLL = 32
INTRA_UNROLL = 4
GDN_SEQS = 2
ATTN_UNROLL = 8
COMBINE_ROWS = 128
FFN_BLOCKS = 2


def _issue_rows(n, make_copy):
    def body(g, _):
        base = pl.multiple_of(g * SUBLANES, SUBLANES)
        for u in range(SUBLANES):
            make_copy(base + u).start()
        return 0

    lax.fori_loop(0, n // SUBLANES, body, 0)


def _wait_rows(n, make_copy):
    def body(_, c):
        for _u in range(DMA_WAIT_UNROLL):
            make_copy(0).wait()
        return c

    lax.fori_loop(0, n // DMA_WAIT_UNROLL, body, 0)


def _dispatch_body(dest_ref, hm_ref, xs_in_ref, xs_ref, sem, *, tm):
    del xs_in_ref
    copy = lambda r: _row_copy(hm_ref, r, xs_ref, dest_ref[r], sem)
    _issue_rows(tm, copy)
    _wait_rows(tm, lambda r: _row_copy(hm_ref, 0, xs_ref, 0, sem))


def _dispatch(hm, dest, xs_buf, tm):
    T, W = hm.shape
    return pl.pallas_call(
        functools.partial(_dispatch_body, tm=tm),
        grid=(T // tm,),
        in_specs=[pl.BlockSpec((tm,), lambda i: (i,), memory_space=pltpu.SMEM),
                  pl.BlockSpec((tm, W), lambda i: (i, 0)),
                  pl.BlockSpec(memory_space=pl.ANY)],
        out_specs=pl.BlockSpec(memory_space=pl.ANY),
        out_shape=jax.ShapeDtypeStruct(xs_buf.shape, xs_buf.dtype),
        input_output_aliases={2: 0},
        scratch_shapes=[pltpu.SemaphoreType.DMA],
        compiler_params=_cparams(("arbitrary",)),
        name="moe_dispatch",
    )(dest, hm, xs_buf)


def _ffn_body(grp_ref, ea_ref, eb_ref, used_ref, xs_ref, wg_ref, wu_ref, wd_ref, o_ref):
    i = pl.program_id(0)
    subs = range(FFN_BLOCKS)
    first = FFN_BLOCKS * i

    @pl.when(used_ref[first] != 0)
    def _():
        a = [ea_ref[first + s] for s in subs]
        b = [eb_ref[first + s] for s in subs]
        xm = [xs_ref[s * MOE_ROWS:(s + 1) * MOE_ROWS, :] for s in subs]
        x = [_unpack_bf16_pairs(xm[s][:, :HALF_D]).astype(BF16) for s in subs]
        g_a = [_dot(x[s], wg_ref[0, 0, a[s]]) for s in subs]
        u_a = [_dot(x[s], wu_ref[0, 0, a[s]]) for s in subs]
        g_b = [_dot(x[s], wg_ref[0, 0, b[s]]) for s in subs]
        u_b = [_dot(x[s], wu_ref[0, 0, b[s]]) for s in subs]
        meta = [lax.bitcast_convert_type(xm[s][:, HALF_D:], F32) for s in subs]
        w_a = [meta[s][:, 2:3] for s in subs]
        w_b = [meta[s][:, 3:4] for s in subs]
        act_a = [(g_a[s] * _sigmoid(g_a[s]) * u_a[s] * w_a[s]).astype(BF16) for s in subs]
        act_b = [(g_b[s] * _sigmoid(g_b[s]) * u_b[s] * w_b[s]).astype(BF16) for s in subs]
        for s in subs:
            y = _dot(act_a[s], wd_ref[0, 0, a[s]]) + _dot(act_b[s], wd_ref[0, 0, b[s]])
            o_ref[s * MOE_ROWS:(s + 1) * MOE_ROWS, :] = _pack_bf16_pairs(y)

    @pl.when(used_ref[first] == 0)
    def _():
        o_ref[...] = jnp.zeros_like(o_ref)


def _ffn(xs, grp, ea, eb, used, w_gate, w_up, w_down, l):
    P, W = xs.shape
    D = D_MODEL
    rows = MOE_ROWS * FFN_BLOCKS
    nb = P // rows
    FF = EXPERT_FF
    E = EXPERTS_PER_GROUP
    w_up_spec = pl.BlockSpec((1, 1, E, D, FF), lambda i, g, ea, eb, u: (l, g[FFN_BLOCKS * i], 0, 0, 0))
    w_dn_spec = pl.BlockSpec((1, 1, E, FF, D), lambda i, g, ea, eb, u: (l, g[FFN_BLOCKS * i], 0, 0, 0))
    grid_spec = pltpu.PrefetchScalarGridSpec(
        num_scalar_prefetch=4,
        grid=(nb,),
        in_specs=[pl.BlockSpec((rows, W), lambda i, g, ea, eb, u: (i, 0)),
                  w_up_spec, w_up_spec, w_dn_spec],
        out_specs=pl.BlockSpec((rows, HALF_D), lambda i, g, ea, eb, u: (i, 0)),
    )
    return pl.pallas_call(
        _ffn_body,
        grid_spec=grid_spec,
        out_shape=jax.ShapeDtypeStruct((P, HALF_D), jnp.uint32),
        compiler_params=_cparams(("arbitrary",)),
        name="moe_ffn",
    )(grp, ea, eb, used, xs, w_gate, w_up, w_down)


def _combine_body(dcur_ref, dnext_ref, x_ref, gt_ref, lg_ref, lb_ref, ys_ref, o_ref,
                  y0_ref, y1_ref, sems, *, tm, n_steps, alpha):
    i = pl.program_id(0)
    bufs = (y0_ref, y1_ref)

    n_chunks = tm // COMBINE_ROWS

    def finish(slot, prefetch):
        _wait_rows(tm, lambda r: _row_copy(ys_ref, 0, bufs[slot], 0, sems.at[slot]))
        gate = 1.0 + gt_ref[0]

        def chunk(c, _):
            r0 = pl.multiple_of(c * COMBINE_ROWS, COMBINE_ROWS)
            if prefetch:
                for u in range(COMBINE_ROWS):
                    _row_copy(ys_ref, dnext_ref[r0 + u], bufs[1 - slot], r0 + u, sems.at[1 - slot]).start()
            rows = pl.ds(r0, COMBINE_ROWS)
            r = alpha * x_ref[rows, :] + gate * _unpack_bf16_pairs(bufs[slot][rows, :])
            o_ref[rows, :] = _layer_norm(r, lg_ref[0], lb_ref[0])
            return 0

        lax.fori_loop(0, n_chunks, chunk, 0)

    @pl.when(i == 0)
    def _():
        _issue_rows(tm, lambda r: _row_copy(ys_ref, dcur_ref[r], bufs[0], r, sems.at[0]))

    has_next = i + 1 < n_steps
    for slot in range(2):
        mine = (i % 2) == slot

        @pl.when(mine & has_next)
        def _():
            finish(slot, True)

        @pl.when(mine & jnp.logical_not(has_next))
        def _():
            finish(slot, False)


def _combine(ys, dest, x2, mod_l, ln_g, ln_b, l, S, tm, alpha):
    T, D = x2.shape
    nS = S // tm
    n_steps = T // tm
    vec = pl.BlockSpec((1, 1, D), lambda i: (l, 0, 0))
    return pl.pallas_call(
        functools.partial(_combine_body, tm=tm, n_steps=n_steps, alpha=alpha),
        grid=(n_steps,),
        in_specs=[pl.BlockSpec((tm,), lambda i: (i,), memory_space=pltpu.SMEM),
                  pl.BlockSpec((tm,), lambda i: (jnp.minimum(i + 1, n_steps - 1),), memory_space=pltpu.SMEM),
                  pl.BlockSpec((tm, D), lambda i: (i, 0)),
                  pl.BlockSpec((1, 1, D), lambda i: (i // nS, 0, 5)),
                  vec, vec,
                  pl.BlockSpec(memory_space=pl.ANY)],
        out_specs=pl.BlockSpec((tm, D), lambda i: (i, 0)),
        out_shape=jax.ShapeDtypeStruct((T, D), F32),
        scratch_shapes=[pltpu.VMEM((tm, HALF_D), jnp.uint32), pltpu.VMEM((tm, HALF_D), jnp.uint32),
                        pltpu.SemaphoreType.DMA((2,))],
        compiler_params=_cparams(("arbitrary",)),
        name="moe_combine_ln",
    )(dest, dest, x2, mod_l, ln_g, ln_b, ys)


def _prep_w_in(w_in):
    L, D, _ = w_in.shape
    o = 0
    cq = w_in[..., o:o + MLA_Q_RANK]; o += MLA_Q_RANK
    ckv = w_in[..., o:o + MLA_KV_RANK]; o += MLA_KV_RANK
    kr = w_in[..., o:o + MLA_ROPE]; o += MLA_ROPE
    gq = w_in[..., o:o + GDN_WIDTH]; o += GDN_WIDTH
    gk = w_in[..., o:o + GDN_WIDTH]; o += GDN_WIDTH
    gv = w_in[..., o:o + GDN_WIDTH]; o += GDN_WIDTH
    gz = w_in[..., o:o + GDN_WIDTH]; o += GDN_WIDTH
    ga = w_in[..., o:o + GDN_HEADS]; o += GDN_HEADS
    gb = w_in[..., o:o + GDN_HEADS]; o += GDN_HEADS
    pw = w_in[..., o:o + POOL_WIDTH]
    half = MLA_ROPE // 2
    z = lambda n: jnp.zeros((L, D, n), w_in.dtype)
    kra = jnp.concatenate([z(MLA_NOPE), kr, z(HEAD_PAD - MLA_NOPE - MLA_ROPE)], axis=-1)
    krb = jnp.concatenate([z(MLA_NOPE), kr[..., half:], kr[..., :half], z(HEAD_PAD - MLA_NOPE - MLA_ROPE)], axis=-1)
    a_e = jnp.repeat(ga, GDN_DIM, axis=-1)
    b_e = jnp.repeat(gb, GDN_DIM, axis=-1)
    out = jnp.concatenate([gq, gk, gv, gz, a_e, b_e, cq, pw, ckv, kra, krb], axis=-1)
    assert out.shape[-1] == IN_COLS
    return out.astype(BF16)


def _prep_mla(w_uq, w_ukv):
    L = w_uq.shape[0]
    H, half = MLA_HEADS, MLA_ROPE // 2
    pad = HEAD_PAD - MLA_NOPE - MLA_ROPE
    q = w_uq.reshape(L, MLA_Q_RANK, H, MLA_NOPE + MLA_ROPE)
    nope, r1, r2 = q[..., :MLA_NOPE], q[..., MLA_NOPE:MLA_NOPE + half], q[..., MLA_NOPE + half:]
    zq = lambda n: jnp.zeros((L, MLA_Q_RANK, H, n), w_uq.dtype)
    plain = jnp.concatenate([nope, r1, r2, zq(pad)], axis=-1).reshape(L, MLA_Q_RANK, H * HEAD_PAD)
    partner = jnp.concatenate([zq(MLA_NOPE), r2, r1, zq(pad)], axis=-1).reshape(L, MLA_Q_RANK, H * HEAD_PAD)
    wq2 = jnp.concatenate([plain, partner], axis=-1).astype(BF16)
    kv = w_ukv.reshape(L, MLA_KV_RANK, H, MLA_NOPE + MLA_V)
    k_nope, v = kv[..., :MLA_NOPE], kv[..., MLA_NOPE:]
    zk = lambda n: jnp.zeros((L, MLA_KV_RANK, H, n), w_ukv.dtype)
    k_main = jnp.concatenate([k_nope, zk(HEAD_PAD - MLA_NOPE)], axis=-1).reshape(L, MLA_KV_RANK, H * HEAD_PAD)
    even = (jnp.arange(H) % 2 == 0)[None, None, :, None]
    v_pair = jnp.where(even, jnp.concatenate([v, zk(MLA_V)], axis=-1), jnp.concatenate([zk(MLA_V), v], axis=-1))
    wkv2 = jnp.concatenate([k_main, v_pair.reshape(L, MLA_KV_RANK, H * HEAD_PAD)], axis=-1).astype(BF16)
    return wq2, wkv2


def _rope_tables(S):
    half = MLA_ROPE // 2
    inv_freq = jnp.power(ROPE_THETA, -jnp.arange(0, MLA_ROPE, 2, dtype=F32) / MLA_ROPE)
    ang = jnp.arange(S, dtype=F32)[:, None] * inv_freq[None, :]
    cos, sin = jnp.cos(ang), jnp.sin(ang)
    pad = jnp.zeros((S, HEAD_PAD - MLA_NOPE - MLA_ROPE), F32)
    cos_t = jnp.concatenate([jnp.ones((S, MLA_NOPE), F32), cos, cos, pad], axis=-1)
    sin_t = jnp.concatenate([jnp.zeros((S, MLA_NOPE), F32), -sin, sin, pad], axis=-1)
    scale = (MLA_NOPE + MLA_ROPE) ** -0.5
    return cos_t * scale, sin_t * scale, cos_t, sin_t


def _block_diag(blocks):
    L, G, n, _ = blocks.shape
    eye = jnp.eye(G, dtype=blocks.dtype)
    return jnp.einsum('lgij,gh->lgihj', blocks, eye).reshape(L, G * n, G * n)


def kernel(x, c, w_in, mla_q_norm, mla_kv_norm, mla_w_uq, mla_w_ukv, gdn_conv, gdn_a_log, gdn_dt_bias, gdn_out_norm, pool_w, pool_scale, w_out, w_mod, b_mod, ln1_g, ln1_b, ln2_g, ln2_b, router_w_group, router_b_group, router_w_expert, router_b_expert, moe_w_gate, moe_w_up, moe_w_down):
    B, S, D = x.shape
    L = w_in.shape[0]
    T = B * S
    alpha = (2 * L) ** 0.25
    ts = min(512, S)
    tq = min(256, S)
    t_moe = min(512, S)
    t_disp = min(1024, S)
    assert D == D_MODEL and S % ts == 0 and S % tq == 0 and ts % CHUNK == 0
    assert S % t_disp == 0 and t_moe % DMA_WAIT_UNROLL == 0

    w_in2 = _prep_w_in(w_in)
    wq2, wkv2 = _prep_mla(mla_w_uq, mla_w_ukv)
    tabs = _rope_tables(S)
    gq = mla_q_norm.reshape(L, 1, MLA_Q_RANK)
    gkv = mla_kv_norm.reshape(L, 1, MLA_KV_RANK)
    alog_e = jnp.repeat(gdn_a_log, GDN_DIM, axis=-1).reshape(L, 1, GDN_WIDTH)
    dt_e = jnp.repeat(gdn_dt_bias, GDN_DIM, axis=-1).reshape(L, 1, GDN_WIDTH)
    og_e = jnp.tile(gdn_out_norm, (1, GDN_HEADS)).reshape(L, 1, GDN_WIDTH)
    lane_head = jnp.arange(GDN_WIDTH) // GDN_DIM
    seg = (lane_head[:, None] == lane_head[None, :]).astype(BF16)
    tri_c = (jnp.arange(CHUNK)[:, None] >= jnp.arange(CHUNK)[None, :]).astype(BF16)
    pool_bd = _block_diag(pool_w).astype(BF16)
    pool_sc = pool_scale.reshape(L, 1, POOL_WIDTH)
    w_out_b = w_out.astype(BF16)
    w_r = jnp.concatenate([router_w_group, router_w_expert,
                           jnp.zeros((L, D, LANE - N_GROUPS - N_EXPERTS), F32)], axis=-1)
    w_r_hi = w_r.astype(BF16)
    w_r_lo = (w_r - w_r_hi.astype(F32)).astype(BF16)
    b_r = jnp.concatenate([router_b_group, router_b_expert,
                           jnp.zeros((L, LANE - N_GROUPS - N_EXPERTS), F32)], axis=-1).reshape(L, 1, LANE)
    tri_r = (jnp.arange(ts)[:, None] > jnp.arange(ts)[None, :]).astype(BF16)
    ln1g, ln1b = ln1_g.reshape(L, 1, D), ln1_b.reshape(L, 1, D)
    ln2g, ln2b = ln2_g.reshape(L, 1, D), ln2_b.reshape(L, 1, D)

    assert N_BUCKETS <= LANE
    step_rows = MOE_ROWS * FFN_BLOCKS
    max_rows = T + N_BUCKETS * (MOE_ROWS - 1) + N_GROUPS * (step_rows - MOE_ROWS)
    P = (max_rows + step_rows - 1) // step_rows * step_rows
    nb = P // MOE_ROWS
    pg, pa, pb = [], [], []
    for g_ in range(N_GROUPS):
        for a_ in range(EXPERTS_PER_GROUP):
            for b_ in range(a_ + 1, EXPERTS_PER_GROUP):
                pg.append(g_)
                pa.append(a_)
                pb.append(b_)
    bucket_g = jnp.asarray(pg, jnp.int32)
    bucket_a = jnp.asarray(pa, jnp.int32)
    bucket_b = jnp.asarray(pb, jnp.int32)
    grouped = lambda w: w.astype(BF16).reshape((L, N_GROUPS, EXPERTS_PER_GROUP) + w.shape[2:])
    wg_b, wu_b, wd_b = grouped(moe_w_gate), grouped(moe_w_up), grouped(moe_w_down)
    bucket_ids = jnp.arange(N_BUCKETS, dtype=jnp.int32)
    blk0 = jnp.arange(nb, dtype=jnp.int32) * MOE_ROWS
    xs = jnp.zeros((P, HALF_D + LANE), jnp.uint32)

    mod = _modulation(c, w_mod, b_mod)
    x2 = x.reshape(T, D)
    for l in range(L):
        mod_l = mod[l].reshape(B, 1, 6 * D)
        proj = _inproj(x2, mod_l, w_in2, l, S, ts)
        q, k, v = _mla_proj(proj, tabs, gq, gkv, wq2, wkv2, l, S, ts)
        y_mla = _attention(q, k, v, B, S, tq)
        y_gdn = _gdn(proj, gdn_conv, alog_e, dt_e, og_e, seg, tri_c, l, B, S, ts)
        y_pool = _pool(proj, pool_bd, pool_sc, l, B, S, ts)
        x2 = _outproj(y_mla, y_gdn, y_pool, w_out_b, x2, mod_l, ln1g, ln1b, l, S, ts, alpha)

        hm, cnt = _router(x2, mod_l, w_r_hi, w_r_lo, b_r, tri_r, l, S, ts)
        counts = cnt[0, :N_BUCKETS].astype(jnp.int32)
        padded = ((counts + MOE_ROWS - 1) // MOE_ROWS * MOE_ROWS).reshape(N_GROUPS, PAIRS_PER_GROUP)
        group_rows = jnp.sum(padded, axis=1)
        group_pad = (step_rows - group_rows % step_rows) % step_rows
        padded = padded.at[:, PAIRS_PER_GROUP - 1].add(group_pad).reshape(N_BUCKETS)
        pad_end = jnp.cumsum(padded)
        pad_start = pad_end - padded
        route = lax.bitcast_convert_type(hm[:, HALF_D:HALF_D + 2], F32).astype(jnp.int32)
        bucket, rank = route[:, 0], route[:, 1]
        dest = jnp.sum(jnp.where(bucket[:, None] == bucket_ids[None, :], pad_start[None, :], 0), axis=1) + rank
        block_bucket = jnp.minimum(jnp.sum((pad_end[None, :] <= blk0[:, None]).astype(jnp.int32), axis=1),
                                   N_BUCKETS - 1)
        onehot_b = (block_bucket[:, None] == bucket_ids[None, :]).astype(jnp.int32)
        grp = jnp.sum(onehot_b * bucket_g[None, :], axis=1)
        ea = jnp.sum(onehot_b * bucket_a[None, :], axis=1)
        eb = jnp.sum(onehot_b * bucket_b[None, :], axis=1)
        used = (blk0 < pad_end[-1]).astype(jnp.int32)

        xs = _dispatch(hm, dest, xs, t_disp)
        ys = _ffn(xs, grp, ea, eb, used, wg_b, wu_b, wd_b, l)
        x2 = _combine(ys, dest, x2, mod_l, ln2g, ln2b, l, S, t_moe, alpha)
    return x2.reshape(B, S, D)
```

```python
import functools
import math

import jax
import jax.numpy as jnp
from jax import lax
from jax.experimental import pallas as pl
from jax.experimental.pallas import tpu as pltpu

F32 = jnp.float32
BF16 = jnp.bfloat16

D_MODEL = 1024
HALF_D = D_MODEL // 2
CHUNK = 64
MLA_HEADS = 8
MLA_NOPE = 64
MLA_ROPE = 32
MLA_V = 64
MLA_Q_RANK = 256
MLA_KV_RANK = 128
MLA_WIDTH = MLA_HEADS * MLA_V
ROPE_THETA = 10000.0
GDN_HEADS = 4
GDN_DIM = 64
GDN_WIDTH = GDN_HEADS * GDN_DIM
GDN_CONV = 4
POOL_WIDTH = 256
POOL_WINDOWS = (2, 4, 8, 16)
POOL_GROUP_DIM = 64
N_GROUPS = 4
EXPERTS_PER_GROUP = 8
N_EXPERTS = N_GROUPS * EXPERTS_PER_GROUP
TOP_K = 2
EXPERT_FF = 256
LN_EPS = 1e-5
RMS_EPS = 1e-6

LANE = 128
SUBLANES = 8
HEAD_PAD = 128
PAIRS_PER_GROUP = EXPERTS_PER_GROUP * (EXPERTS_PER_GROUP - 1) // 2
N_BUCKETS = N_GROUPS * PAIRS_PER_GROUP
MOE_ROWS = 128
VMEM_LIMIT = 48 * 1024 * 1024

_C_GQ, _C_GK, _C_GV, _C_GZ = 0, 256, 512, 768
_C_CQ, _C_POOL = 1024, 1280
_C_CKV, _C_KRA, _C_KRB = 1536, 1664, 1792
IN_COLS = 1920


def _cparams(sem):
    return pltpu.CompilerParams(dimension_semantics=sem, vmem_limit_bytes=VMEM_LIMIT)


def _sigmoid(x):
    return 1.0 / (1.0 + jnp.exp(-x))


def _split2(x):
    hi = x.astype(BF16)
    lo = (x - hi.astype(F32)).astype(BF16)
    return hi, lo


def _split3(x):
    hi = x.astype(BF16)
    r = x - hi.astype(F32)
    mid = r.astype(BF16)
    lo = (r - mid.astype(F32)).astype(BF16)
    return hi, mid, lo


def _pack_bf16_pairs(x):
    n = x.shape[1] // 2
    lo = lax.bitcast_convert_type(x[:, :n].astype(BF16).astype(F32), jnp.uint32)
    hi = lax.bitcast_convert_type(x[:, n:].astype(BF16).astype(F32), jnp.uint32)
    return (hi & jnp.uint32(0xFFFF0000)) | (lo >> 16)


def _unpack_bf16_pairs(p):
    lo = lax.bitcast_convert_type(p << 16, F32)
    hi = lax.bitcast_convert_type(p & jnp.uint32(0xFFFF0000), F32)
    return jnp.concatenate([lo, hi], axis=1)


def _dot(a, b):
    return jnp.dot(a, b, preferred_element_type=F32)


def _dot_nt(a, b):
    return lax.dot_general(a, b, (((1,), (1,)), ((), ())), preferred_element_type=F32)


def _dot_tn(a, b):
    return lax.dot_general(a, b, (((0,), (0,)), ((), ())), preferred_element_type=F32)


def _mod_body(c_ref, w_ref, b_ref, o_ref):
    c = c_ref[...]
    ca = c * _sigmoid(c)
    o_ref[0] = _dot(ca.astype(BF16), w_ref[0].astype(BF16)) + b_ref[0]


def _modulation(c, w_mod, b_mod):
    L, D, N = w_mod.shape
    B = c.shape[0]
    tn = 1024
    return pl.pallas_call(
        _mod_body,
        grid=(L, N // tn),
        in_specs=[pl.BlockSpec((B, D), lambda l, j: (0, 0)),
                  pl.BlockSpec((1, D, tn), lambda l, j: (l, 0, j)),
                  pl.BlockSpec((1, 1, tn), lambda l, j: (l, 0, j))],
        out_specs=pl.BlockSpec((1, B, tn), lambda l, j: (l, 0, j)),
        out_shape=jax.ShapeDtypeStruct((L, B, N), F32),
        compiler_params=_cparams(("arbitrary", "arbitrary")),
        name="modulation",
    )(c, w_mod, b_mod.reshape(L, 1, N))


def _inproj_body(x_ref, sh_ref, sc_ref, w_ref, o_ref):
    h = x_ref[...] * (1.0 + sc_ref[0]) + sh_ref[0]
    o_ref[...] = _dot(h.astype(BF16), w_ref[0])


def _inproj(x2, mod_l, w_in2, l, S, tm):
    T, D = x2.shape
    nS = S // tm
    return pl.pallas_call(
        _inproj_body,
        grid=(T // tm,),
        in_specs=[pl.BlockSpec((tm, D), lambda i: (i, 0)),
                  pl.BlockSpec((1, 1, D), lambda i: (i // nS, 0, 0)),
                  pl.BlockSpec((1, 1, D), lambda i: (i // nS, 0, 1)),
                  pl.BlockSpec((1, D, IN_COLS), lambda i: (l, 0, 0))],
        out_specs=pl.BlockSpec((tm, IN_COLS), lambda i: (i, 0)),
        out_shape=jax.ShapeDtypeStruct((T, IN_COLS), F32),
        compiler_params=_cparams(("arbitrary",)),
        name="inproj",
    )(x2, mod_l, mod_l, w_in2)


def _mla_proj_body(cq_ref, ckv_ref, kra_ref, krb_ref, cosq_ref, sinq_ref, cosk_ref, sink_ref,
                   gq_ref, gkv_ref, wq_ref, wkv_ref, q_out, k_out, v_out):
    cq = cq_ref[...]
    qn = cq * lax.rsqrt(jnp.mean(cq * cq, axis=-1, keepdims=True) + RMS_EPS) * gq_ref[0]
    q2 = _dot(qn.astype(BF16), wq_ref[0])
    ckv = ckv_ref[...]
    kvn = ckv * lax.rsqrt(jnp.mean(ckv * ckv, axis=-1, keepdims=True) + RMS_EPS) * gkv_ref[0]
    kv2 = _dot(kvn.astype(BF16), wkv_ref[0])
    cq_t, sq_t = cosq_ref[...], sinq_ref[...]
    krope = kra_ref[...] * cosk_ref[...] + krb_ref[...] * sink_ref[...]
    hw = MLA_HEADS * HEAD_PAD
    for h in range(MLA_HEADS):
        a, b = h * HEAD_PAD, (h + 1) * HEAD_PAD
        q_out[:, a:b] = (q2[:, a:b] * cq_t + q2[:, hw + a:hw + b] * sq_t).astype(BF16)
        k_out[:, a:b] = (kv2[:, a:b] + krope).astype(BF16)
    vl = lax.broadcasted_iota(jnp.int32, (1, hw), 1)
    ones_lane = jnp.where((vl // HEAD_PAD) % 2 == 0, MLA_V, 0)
    v_out[...] = (kv2[:, hw:] + (vl % HEAD_PAD == ones_lane).astype(F32)).astype(BF16)


def _mla_proj(proj, tabs, gq, gkv, wq2, wkv2, l, S, ts):
    T = proj.shape[0]
    nS = S // ts
    hw = MLA_HEADS * HEAD_PAD
    tab_spec = pl.BlockSpec((ts, LANE), lambda i: (i % nS, 0))
    return pl.pallas_call(
        _mla_proj_body,
        grid=(T // ts,),
        in_specs=[pl.BlockSpec((ts, 256), lambda i: (i, _C_CQ // 256)),
                  pl.BlockSpec((ts, 128), lambda i: (i, _C_CKV // 128)),
                  pl.BlockSpec((ts, 128), lambda i: (i, _C_KRA // 128)),
                  pl.BlockSpec((ts, 128), lambda i: (i, _C_KRB // 128)),
                  tab_spec, tab_spec, tab_spec, tab_spec,
                  pl.BlockSpec((1, 1, MLA_Q_RANK), lambda i: (l, 0, 0)),
                  pl.BlockSpec((1, 1, MLA_KV_RANK), lambda i: (l, 0, 0)),
                  pl.BlockSpec((1, MLA_Q_RANK, 2 * hw), lambda i: (l, 0, 0)),
                  pl.BlockSpec((1, MLA_KV_RANK, 2 * hw), lambda i: (l, 0, 0))],
        out_specs=[pl.BlockSpec((ts, hw), lambda i: (i, 0))] * 3,
        out_shape=[jax.ShapeDtypeStruct((T, hw), BF16)] * 3,
        compiler_params=_cparams(("arbitrary",)),
        name="mla_proj",
    )(proj, proj, proj, proj, *tabs, gq, gkv, wq2, wkv2)


def _attn_body(qi_tab, kj_tab, q_ref, k_ref, v_ref, o_ref, m_s, acc_s, *, tq, n_q, n_tiles):
    row_c = lax.broadcasted_iota(jnp.int32, (tq, tq), 0) // CHUNK
    col_c = lax.broadcasted_iota(jnp.int32, (tq, tq), 1) // CHUNK
    allowed = col_c <= row_c
    lane = lax.broadcasted_iota(jnp.int32, (tq, HEAD_PAD), 1)
    sl = [slice(hh * HEAD_PAD, (hh + 1) * HEAD_PAD) for hh in range(2)]
    hs = range(2)

    def rows_of(tile):
        return pl.ds(pl.multiple_of(tile * tq, tq), tq)

    def stage_a(t):
        q = q_ref[rows_of(qi_tab[t]), :]
        kk = k_ref[rows_of(kj_tab[t]), :]
        return tuple(_dot_nt(q[:, sl[hh]], kk[:, sl[hh]]).astype(BF16) for hh in hs)

    def stage_b(t, s_pair):
        qi, kj = qi_tab[t], kj_tab[t]
        rows = rows_of(qi)
        keep = jnp.logical_or(allowed, kj < qi)
        neg = jnp.asarray(-jnp.inf, BF16)
        s = [jnp.where(keep, s_pair[hh], neg) for hh in hs]
        m_prev = [m_s[hh, rows, :] for hh in hs]
        m_new = [jnp.maximum(m_prev[hh], jnp.max(s[hh], axis=-1, keepdims=True)) for hh in hs]
        p = tuple(jnp.exp(s[hh] - jnp.concatenate([m_new[hh]] * (tq // HEAD_PAD), axis=1)) for hh in hs)
        alpha = tuple(jnp.exp(m_prev[hh] - m_new[hh]) for hh in hs)
        for hh in hs:
            m_s[hh, rows, :] = m_new[hh]
        return p, alpha

    def stage_c(t, p, alpha):
        qi, kj = qi_tab[t], kj_tab[t]
        rows = rows_of(qi)
        vv = v_ref[rows_of(kj), :]
        pv = [_dot(p[hh], vv[:, sl[hh]]) for hh in hs]
        for hh in hs:
            acc_s[hh, rows, :] = alpha[hh].astype(F32) * acc_s[hh, rows, :] + pv[hh]

    def body(t, c):
        s_next, (p, alpha) = c
        s_after = stage_a(t + 2)
        stage_c(t, p, alpha)
        return s_after, stage_b(t + 1, s_next)

    m_s[...] = jnp.full(m_s.shape, -jnp.inf, m_s.dtype)
    acc_s[...] = jnp.zeros_like(acc_s)
    pb0 = stage_b(0, stage_a(0))
    s_last, pb = lax.fori_loop(0, n_tiles - 2, body, (stage_a(1), pb0), unroll=ATTN_UNROLL)
    pb_last = stage_b(n_tiles - 1, s_last)
    stage_c(n_tiles - 2, *pb)
    stage_c(n_tiles - 1, *pb_last)

    def normalize(i, _):
        rows = rows_of(i)
        acc0, acc1 = acc_s[0, rows, :], acc_s[1, rows, :]
        o0 = jnp.where(lane < MLA_V, acc0 * (1.0 / acc0[:, MLA_V:MLA_V + 1]), 0.0)
        o1 = jnp.where(lane >= MLA_V, acc1 * (1.0 / acc1[:, 0:1]), 0.0)
        o_ref[rows, :] = (o0 + o1).astype(BF16)
        return 0

    lax.fori_loop(0, n_q, normalize, 0)


def _attention(q, k, v, B, S, tq):
    T = q.shape[0]
    nq = S // tq
    pairs = MLA_HEADS // 2
    tiles = [(i, j) for i in range(nq) for j in range(i + 1)]
    assert len(tiles) >= 2
    qi_tab = jnp.asarray([t[0] for t in tiles], jnp.int32)
    kj_tab = jnp.asarray([t[1] for t in tiles], jnp.int32)
    seq = pl.BlockSpec((S, 2 * HEAD_PAD), lambda b, p, qt, kt: (b, p))
    grid_spec = pltpu.PrefetchScalarGridSpec(
        num_scalar_prefetch=2,
        grid=(B, pairs),
        in_specs=[seq, seq, seq],
        out_specs=pl.BlockSpec((S, 2 * MLA_V), lambda b, p, qt, kt: (b, p)),
        scratch_shapes=[pltpu.VMEM((2, S, HEAD_PAD), BF16), pltpu.VMEM((2, S, HEAD_PAD), F32)],
    )
    return pl.pallas_call(
        functools.partial(_attn_body, tq=tq, n_q=nq, n_tiles=len(tiles)),
        grid_spec=grid_spec,
        out_shape=jax.ShapeDtypeStruct((T, MLA_WIDTH), BF16),
        compiler_params=_cparams(("arbitrary", "arbitrary")),
        name="mla_attention",
    )(qi_tab, kj_tab, q, k, v)


def _pool_body(p_ref, w_ref, sc_ref, o_ref, prev_ref, *, ts):
    j = pl.program_id(1)
    hist = 16

    @pl.when(j == 0)
    def _():
        prev_ref[...] = jnp.zeros_like(prev_ref)

    cur = p_ref[...]
    x = jnp.concatenate([prev_ref[...], cur], axis=0)
    s1 = x + pltpu.roll(x, 1, axis=0)
    s2 = s1 + pltpu.roll(s1, 2, axis=0)
    s4 = s2 + pltpu.roll(s2, 4, axis=0)
    s8 = s4 + pltpu.roll(s4, 8, axis=0)
    pos = (j * ts + lax.broadcasted_iota(jnp.int32, (ts, POOL_WIDTH), 0) + 1).astype(F32)
    lane = lax.broadcasted_iota(jnp.int32, (ts, POOL_WIDTH), 1)
    sums = (s1, s2, s4, s8)
    mean = None
    for gi, win in enumerate(POOL_WINDOWS):
        m_g = sums[gi][hist:] / jnp.minimum(pos, float(win))
        mean = m_g if mean is None else jnp.where(lane >= gi * POOL_GROUP_DIM, m_g, mean)
    delta = mean - cur
    o_ref[...] = (_dot(delta.astype(BF16), w_ref[0]) * sc_ref[0]).astype(BF16)
    prev_ref[...] = cur[ts - hist:]


def _pool(proj, w_bd, scale, l, B, S, ts):
    T = proj.shape[0]
    nS = S // ts
    return pl.pallas_call(
        functools.partial(_pool_body, ts=ts),
        grid=(B, nS),
        in_specs=[pl.BlockSpec((ts, POOL_WIDTH), lambda b, j: (b * nS + j, _C_POOL // 256)),
                  pl.BlockSpec((1, POOL_WIDTH, POOL_WIDTH), lambda b, j: (l, 0, 0)),
                  pl.BlockSpec((1, 1, POOL_WIDTH), lambda b, j: (l, 0, 0))],
        out_specs=pl.BlockSpec((ts, POOL_WIDTH), lambda b, j: (b * nS + j, 0)),
        out_shape=jax.ShapeDtypeStruct((T, POOL_WIDTH), BF16),
        scratch_shapes=[pltpu.VMEM((16, POOL_WIDTH), F32)],
        compiler_params=_cparams(("arbitrary", "arbitrary")),
        name="pool_mixer",
    )(proj, w_bd, scale)


def _head_sum(x, seg_ref):
    hi, lo = _split2(x)
    return _dot(hi, seg_ref[...]) + _dot(lo, seg_ref[...])


def _gdn_inputs(s, raw_q, raw_k, raw_v, raw_ab, expand_ref, conv_ref, alog_ref, dt_ref, seg_ref,
                q_ref, k_ref, v_ref, g_ref, b_ref, prev_ref, ts):
    hist = 8
    cw = conv_ref[0]
    outs = []
    for idx, ref in enumerate((raw_q, raw_k, raw_v)):
        cur = ref[s]
        a, b = idx * GDN_WIDTH, (idx + 1) * GDN_WIDTH
        x = jnp.concatenate([prev_ref[s, :, a:b], cur], axis=0)
        w = cw[:, a:b]
        y = (w[3:4] * x + w[2:3] * pltpu.roll(x, 1, axis=0) + w[1:2] * pltpu.roll(x, 2, axis=0)
             + w[0:1] * pltpu.roll(x, 3, axis=0))[hist:]
        outs.append(y * _sigmoid(y))
        prev_ref[s, :, a:b] = cur[ts - hist:]
    qc, kc, vc = outs
    q_ref[s] = qc * lax.rsqrt(_head_sum(qc * qc, seg_ref) + RMS_EPS) * (GDN_DIM ** -0.5)
    k_ref[s] = kc * lax.rsqrt(_head_sum(kc * kc, seg_ref) + RMS_EPS)
    v_ref[s] = vc
    ab_hi, ab_mid, ab_lo = _split3(raw_ab[s])
    e = expand_ref[...]
    ab = _dot(ab_hi, e) + _dot(ab_mid, e) + _dot(ab_lo, e)
    z = ab[:, :GDN_WIDTH] + dt_ref[0]
    softplus = jnp.maximum(z, 0.0) + jnp.log(1.0 + jnp.exp(-jnp.abs(z)))
    g_ref[s] = -jnp.exp(alog_ref[0]) * softplus
    b_ref[s] = _sigmoid(ab[:, GDN_WIDTH:])


def _gdn_body(raw_q, raw_k, raw_v, raw_ab, z_ref, expand_ref, conv_ref, alog_ref, dt_ref, og_ref, seg_ref, tri_ref,
              o_ref, state_ref, prev_ref, q_ref, k_ref, v_ref, g_ref, b_ref,
              u_s, w_s, qk_s, qd_s, kd_s, gl_s, *, ts):
    j = pl.program_id(1)
    C, W, H = CHUNK, GDN_WIDTH, GDN_HEADS
    n_chunks = ts // C
    groups = n_chunks // INTRA_UNROLL

    @pl.when(j == 0)
    def _():
        state_ref[...] = jnp.zeros_like(state_ref)
        prev_ref[...] = jnp.zeros_like(prev_ref)

    for s in range(GDN_SEQS):
        _gdn_inputs(s, raw_q, raw_k, raw_v, raw_ab, expand_ref, conv_ref, alog_ref, dt_ref, seg_ref,
                    q_ref, k_ref, v_ref, g_ref, b_ref, prev_ref, ts)

    lane = lax.broadcasted_iota(jnp.int32, (C, W), 1)
    row = lax.broadcasted_iota(jnp.int32, (C, W), 0)
    col_tok = lane % C
    incl = row >= col_tok
    strict = row > col_tok
    eye = (row == col_tok).astype(F32)
    head_masks = [(lane // C == h).astype(F32) for h in range(H)]
    bd_mask = (lax.broadcasted_iota(jnp.int32, (W, W), 0) // C
               == lax.broadcasted_iota(jnp.int32, (W, W), 1) // C)

    def expand(m):
        return jnp.concatenate([m * hm for hm in head_masks], axis=0)

    def bdot(lhs, rhs):
        return _dot(lhs.astype(BF16), rhs.astype(BF16))

    def intra(i, _):
        sq = i // groups
        cs = [(i % groups) * INTRA_UNROLL + uu for uu in range(INTRA_UNROLL)]
        r0s = [pl.multiple_of(c * C, C) for c in cs]
        n = range(INTRA_UNROLL)
        tri = tri_ref[...]
        gs = [_split3(g_ref[sq, pl.ds(r0, C), :]) for r0 in r0s]
        gc = [_dot(tri, g[0]) + _dot(tri, g[1]) + _dot(tri, g[2]) for g in gs]
        ks = [k_ref[sq, pl.ds(r0, C), :] for r0 in r0s]
        qs = [q_ref[sq, pl.ds(r0, C), :] for r0 in r0s]
        betas = [b_ref[sq, pl.ds(r0, C), :] for r0 in r0s]
        kb = [ks[x] * betas[x] for x in n]
        aq = [_dot_nt(jnp.concatenate([kb[x], qs[x]], axis=0).astype(BF16), expand(ks[x]).astype(BF16)) for x in n]
        g_row = [jnp.sum(gc[x] * eye, axis=0, keepdims=True) for x in n]
        g_last = [gc[x][C - 1:C, :] for x in n]
        decay = [jnp.where(incl, jnp.exp(jnp.where(incl, gc[x] - g_row[x], 0.0)), 0.0) for x in n]
        e_gc = [jnp.exp(gc[x]) for x in n]
        a_cat = [jnp.where(strict, aq[x][:C] * decay[x], 0.0) for x in n]
        for x in n:
            r0 = r0s[x]
            qk_s[sq, pl.ds(r0, C), :] = jnp.where(incl, aq[x][C:] * decay[x], 0.0).astype(BF16)
            qd_s[sq, pl.ds(r0, C), :] = (qs[x] * e_gc[x]).astype(BF16)
            kd_s[sq, pl.ds(r0, C), :] = (ks[x] * jnp.exp(g_last[x] - gc[x])).astype(BF16)
            gl_s[sq, pl.ds(pl.multiple_of(cs[x] * 8, 8), 8), :] = jnp.broadcast_to(jnp.exp(g_last[x]), (8, W))
        x_cat = [eye - a_cat[x] for x in n]
        p_cat = [bdot(a_cat[x], expand(a_cat[x])) for x in n]
        n_fac = int(math.log2(C)) - 1
        for r in range(n_fac):
            last = r == n_fac - 1
            xp = [bdot(x_cat[x] if last else jnp.concatenate([x_cat[x], p_cat[x]], axis=0), expand(p_cat[x]))
                  for x in n]
            x_cat = [x_cat[x] + xp[x][:C] for x in n]
            if not last:
                p_cat = [xp[x][C:] for x in n]
        for x in n:
            r0 = r0s[x]
            t_cat = x_cat[x].astype(BF16)
            v = v_ref[sq, pl.ds(r0, C), :]
            u_s[sq, pl.ds(r0, C), :] = _dot(t_cat, expand(v * betas[x]).astype(BF16))
            w_s[sq, pl.ds(r0, C), :] = _dot(t_cat, expand(kb[x] * e_gc[x]).astype(BF16)).astype(BF16)
        return 0

    lax.fori_loop(0, GDN_SEQS * groups, intra, 0)

    def scan(c, _):
        r0 = pl.multiple_of(c * C, C)
        sq = range(GDN_SEQS)
        state = [state_ref[s] for s in sq]
        wq = [_dot(jnp.concatenate([w_s[s, pl.ds(r0, C), :], qd_s[s, pl.ds(r0, C), :]], axis=0),
                   state[s].astype(BF16)) for s in sq]
        v_new = [u_s[s, pl.ds(r0, C), :] - wq[s][:C] for s in sq]
        upd = [_dot_tn(kd_s[s, pl.ds(r0, C), :], v_new[s].astype(BF16)) for s in sq]
        for s in sq:
            g_l = gl_s[s, pl.ds(pl.multiple_of(c * 8, 8), 1), :]
            state_ref[s] = state[s] * g_l + jnp.where(bd_mask, upd[s], 0.0)
        o = [wq[s][C:] + _dot(qk_s[s, pl.ds(r0, C), :], expand(v_new[s]).astype(BF16)) for s in sq]
        ms = [_head_sum(o[s] * o[s], seg_ref) * (1.0 / GDN_DIM) for s in sq]
        for s in sq:
            z = z_ref[s, pl.ds(r0, C), :]
            y = o[s] * lax.rsqrt(ms[s] + RMS_EPS) * og_ref[0] * (z * _sigmoid(z))
            o_ref[s, pl.ds(r0, C), :] = y.astype(BF16)
        return 0

    lax.fori_loop(0, n_chunks, scan, 0, unroll=2)


def _gdn(proj, expand, conv_w, alog_e, dt_e, og, seg, tri, l, B, S, ts):
    T = proj.shape[0]
    nS = S // ts
    W = GDN_WIDTH
    Q = GDN_SEQS
    proj3 = proj.reshape(B, S, proj.shape[-1])

    def col(c):
        return pl.BlockSpec((Q, ts, W), lambda b, j: (b, j, c // W))

    vec = pl.BlockSpec((1, 1, W), lambda b, j: (l, 0, 0))
    seq_f32 = pltpu.VMEM((Q, ts, W), F32)
    out = pl.pallas_call(
        functools.partial(_gdn_body, ts=ts),
        grid=(B // Q, nS),
        in_specs=[col(_C_GQ), col(_C_GK), col(_C_GV),
                  pl.BlockSpec((Q, ts, LANE), lambda b, j: (b, j, _C_KRB // LANE)),
                  col(_C_GZ),
                  pl.BlockSpec((LANE, 2 * W), lambda b, j: (0, 0)),
                  pl.BlockSpec((1, GDN_CONV, 3 * W), lambda b, j: (l, 0, 0)),
                  vec, vec, vec,
                  pl.BlockSpec((W, W), lambda b, j: (0, 0)),
                  pl.BlockSpec((CHUNK, CHUNK), lambda b, j: (0, 0))],
        out_specs=pl.BlockSpec((Q, ts, W), lambda b, j: (b, j, 0)),
        out_shape=jax.ShapeDtypeStruct((B, S, W), BF16),
        scratch_shapes=[pltpu.VMEM((Q, W, W), F32), pltpu.VMEM((Q, 8, 3 * W), F32)]
        + [seq_f32] * 5 + [seq_f32] + [pltpu.VMEM((Q, ts, W), BF16)] * 4
        + [pltpu.VMEM((Q, 8 * (ts // CHUNK), W), F32)],
        compiler_params=_cparams(("arbitrary", "arbitrary")),
        name="gdn_delta_rule",
    )(proj3, proj3, proj3, proj3, proj3, expand, conv_w, alog_e, dt_e, og, seg, tri)
    return out.reshape(T, W)


def _layer_norm(r, g, b):
    mu = jnp.mean(r, axis=-1, keepdims=True)
    d = r - mu
    var = jnp.mean(d * d, axis=-1, keepdims=True)
    return d * lax.rsqrt(var + LN_EPS) * g + b


def _outproj_body(ym_ref, yg_ref, yp_ref, w_ref, x_ref, gt_ref, lg_ref, lb_ref, o_ref, *, alpha):
    w = w_ref[0]
    y = (_dot(ym_ref[...], w[:MLA_WIDTH]) + _dot(yg_ref[...], w[MLA_WIDTH:MLA_WIDTH + GDN_WIDTH])
         + _dot(yp_ref[...], w[MLA_WIDTH + GDN_WIDTH:]))
    r = alpha * x_ref[...] + (1.0 + gt_ref[0]) * y
    o_ref[...] = _layer_norm(r, lg_ref[0], lb_ref[0])


def _outproj(y_mla, y_gdn, y_pool, w_out, x2, mod_l, ln_g, ln_b, l, S, tm, alpha):
    T, D = x2.shape
    nS = S // tm
    vec = pl.BlockSpec((1, 1, D), lambda i: (l, 0, 0))
    return pl.pallas_call(
        functools.partial(_outproj_body, alpha=alpha),
        grid=(T // tm,),
        in_specs=[pl.BlockSpec((tm, MLA_WIDTH), lambda i: (i, 0)),
                  pl.BlockSpec((tm, GDN_WIDTH), lambda i: (i, 0)),
                  pl.BlockSpec((tm, POOL_WIDTH), lambda i: (i, 0)),
                  pl.BlockSpec((1, D, D), lambda i: (l, 0, 0)),
                  pl.BlockSpec((tm, D), lambda i: (i, 0)),
                  pl.BlockSpec((1, 1, D), lambda i: (i // nS, 0, 2)),
                  vec, vec],
        out_specs=pl.BlockSpec((tm, D), lambda i: (i, 0)),
        out_shape=jax.ShapeDtypeStruct((T, D), F32),
        compiler_params=_cparams(("arbitrary",)),
        name="outproj_ln",
    )(y_mla, y_gdn, y_pool, w_out, x2, mod_l, ln_g, ln_b)


def _lane_first(cond, lane_f):
    return jnp.min(jnp.where(cond, lane_f, float(LANE)), axis=-1, keepdims=True)


def _router_body(x_ref, sh_ref, sc_ref, whi_ref, wlo_ref, br_ref, tri_ref,
                 hm_out, cnt_out, carry_ref):
    i = pl.program_id(0)

    @pl.when(i == 0)
    def _():
        carry_ref[...] = jnp.zeros_like(carry_ref)

    h = x_ref[...] * (1.0 + sc_ref[0]) + sh_ref[0]
    hm_out[:, :HALF_D] = _pack_bf16_pairs(h)
    h_hi, h_lo = _split2(h)
    logits = _dot(h_hi, whi_ref[0]) + _dot(h_lo, whi_ref[0]) + _dot(h_hi, wlo_ref[0]) + br_ref[0]
    tm = logits.shape[0]
    lane = lax.broadcasted_iota(jnp.int32, (tm, LANE), 1)
    lane_f = lane.astype(F32)
    neg = -jnp.inf
    gl = jnp.where(lane < N_GROUPS, logits, neg)
    gmax = jnp.max(gl, axis=-1, keepdims=True)
    gsel = _lane_first(gl == gmax, lane_f)
    g_p = 1.0 / jnp.sum(jnp.exp(gl - gmax), axis=-1, keepdims=True)
    lo = N_GROUPS + EXPERTS_PER_GROUP * gsel
    el = jnp.where((lane_f >= lo) & (lane_f < lo + EXPERTS_PER_GROUP), logits, neg)
    m1 = jnp.max(el, axis=-1, keepdims=True)
    i1 = _lane_first(el == m1, lane_f)
    el2 = jnp.where(lane_f == i1, neg, el)
    m2 = jnp.max(el2, axis=-1, keepdims=True)
    i2 = _lane_first(el2 == m2, lane_f)
    t = jnp.exp(m2 - m1)
    w1 = g_p / (1.0 + t)
    w2 = g_p * t / (1.0 + t)
    loc1 = i1 - lo
    loc2 = i2 - lo
    a_loc = jnp.minimum(loc1, loc2)
    b_loc = jnp.maximum(loc1, loc2)
    pair = a_loc * (2 * EXPERTS_PER_GROUP - 1 - a_loc) * 0.5 + (b_loc - a_loc - 1.0)
    bucket = gsel * PAIRS_PER_GROUP + pair
    first_is_a = loc1 < loc2
    w_a = jnp.where(first_is_a, w1, w2)
    w_b = jnp.where(first_is_a, w2, w1)
    hit = lane_f == bucket
    onehot = hit.astype(BF16)
    before = _dot(tri_ref[...], onehot) + carry_ref[0:1, :]
    rank = jnp.sum(jnp.where(hit, before, 0.0), axis=-1, keepdims=True)
    total = carry_ref[0:1, :] + jnp.sum(onehot.astype(F32), axis=0, keepdims=True)
    carry_ref[...] = jnp.broadcast_to(total, carry_ref.shape)
    cnt_out[...] = jnp.broadcast_to(total, cnt_out.shape)
    meta = jnp.zeros((tm, LANE), F32)
    for idx, val in enumerate((bucket, rank, w_a, w_b)):
        meta = jnp.where(lane == idx, val, meta)
    hm_out[:, HALF_D:] = lax.bitcast_convert_type(meta, jnp.uint32)


def _router(x2, mod_l, w_hi, w_lo, b_r, tri, l, S, tm):
    T, D = x2.shape
    nS = S // tm
    return pl.pallas_call(
        _router_body,
        grid=(T // tm,),
        in_specs=[pl.BlockSpec((tm, D), lambda i: (i, 0)),
                  pl.BlockSpec((1, 1, D), lambda i: (i // nS, 0, 3)),
                  pl.BlockSpec((1, 1, D), lambda i: (i // nS, 0, 4)),
                  pl.BlockSpec((1, D, LANE), lambda i: (l, 0, 0)),
                  pl.BlockSpec((1, D, LANE), lambda i: (l, 0, 0)),
                  pl.BlockSpec((1, 1, LANE), lambda i: (l, 0, 0)),
                  pl.BlockSpec((tm, tm), lambda i: (0, 0))],
        out_specs=[pl.BlockSpec((tm, HALF_D + LANE), lambda i: (i, 0)),
                   pl.BlockSpec((8, LANE), lambda i: (0, 0))],
        out_shape=[jax.ShapeDtypeStruct((T, HALF_D + LANE), jnp.uint32),
                   jax.ShapeDtypeStruct((8, LANE), F32)],
        scratch_shapes=[pltpu.VMEM((8, LANE), F32)],
        compiler_params=_cparams(("arbitrary",)),
        name="router",
    )(x2, mod_l, mod_l, w_hi, w_lo, b_r, tri)


def _row_copy(src, s, dst, d, sem):
    return pltpu.make_async_copy(src.at[pl.ds(s, 1)], dst.at[pl.ds(d, 1)], sem)


DMA_WAIT_UNROLL = 32
INTRA_UNROLL = 4
GDN_SEQS = 2
ATTN_UNROLL = 8
COMBINE_ROWS = 128
FFN_BLOCKS = 2


def _issue_rows(n, make_copy):
    def body(g, _):
        base = pl.multiple_of(g * SUBLANES, SUBLANES)
        for u in range(SUBLANES):
            make_copy(base, u).start()
        return 0

    lax.fori_loop(0, n // SUBLANES, body, 0)


def _tile_row(ref, base, u):
    return ref.at[pl.ds(base, SUBLANES)].at[pl.ds(u, 1)]


def _wait_rows(n, make_copy):
    def body(_, c):
        for _u in range(DMA_WAIT_UNROLL):
            make_copy(0).wait()
        return c

    lax.fori_loop(0, n // DMA_WAIT_UNROLL, body, 0)


def _dispatch_body(dest_ref, hm_ref, xs_in_ref, xs_ref, sem, *, tm):
    del xs_in_ref
    copy = lambda base, u: pltpu.make_async_copy(
        _tile_row(hm_ref, base, u), xs_ref.at[pl.ds(dest_ref[base + u], 1)], sem)
    _issue_rows(tm, copy)
    _wait_rows(tm, lambda r: _row_copy(hm_ref, 0, xs_ref, 0, sem))


def _dispatch(hm, dest, xs_buf, tm):
    T, W = hm.shape
    return pl.pallas_call(
        functools.partial(_dispatch_body, tm=tm),
        grid=(T // tm,),
        in_specs=[pl.BlockSpec((tm,), lambda i: (i,), memory_space=pltpu.SMEM),
                  pl.BlockSpec((tm, W), lambda i: (i, 0)),
                  pl.BlockSpec(memory_space=pl.ANY)],
        out_specs=pl.BlockSpec(memory_space=pl.ANY),
        out_shape=jax.ShapeDtypeStruct(xs_buf.shape, xs_buf.dtype),
        input_output_aliases={2: 0},
        scratch_shapes=[pltpu.SemaphoreType.DMA],
        compiler_params=_cparams(("arbitrary",)),
        name="moe_dispatch",
    )(dest, hm, xs_buf)


def _ffn_body(grp_ref, ea_ref, eb_ref, used_ref, xs_ref, wg_ref, wu_ref, wd_ref, o_ref):
    i = pl.program_id(0)
    subs = range(FFN_BLOCKS)
    first = FFN_BLOCKS * i

    @pl.when(used_ref[first] != 0)
    def _():
        a = [ea_ref[first + s] for s in subs]
        b = [eb_ref[first + s] for s in subs]
        xm = [xs_ref[s * MOE_ROWS:(s + 1) * MOE_ROWS, :] for s in subs]
        x = [_unpack_bf16_pairs(xm[s][:, :HALF_D]).astype(BF16) for s in subs]
        g_a = [_dot(x[s], wg_ref[0, 0, a[s]]) for s in subs]
        u_a = [_dot(x[s], wu_ref[0, 0, a[s]]) for s in subs]
        g_b = [_dot(x[s], wg_ref[0, 0, b[s]]) for s in subs]
        u_b = [_dot(x[s], wu_ref[0, 0, b[s]]) for s in subs]
        meta = [lax.bitcast_convert_type(xm[s][:, HALF_D:], F32) for s in subs]
        w_a = [meta[s][:, 2:3] for s in subs]
        w_b = [meta[s][:, 3:4] for s in subs]
        act_a = [(g_a[s] * _sigmoid(g_a[s]) * u_a[s] * w_a[s]).astype(BF16) for s in subs]
        act_b = [(g_b[s] * _sigmoid(g_b[s]) * u_b[s] * w_b[s]).astype(BF16) for s in subs]
        for s in subs:
            y = _dot(act_a[s], wd_ref[0, 0, a[s]]) + _dot(act_b[s], wd_ref[0, 0, b[s]])
            o_ref[s * MOE_ROWS:(s + 1) * MOE_ROWS, :] = _pack_bf16_pairs(y)

    @pl.when(used_ref[first] == 0)
    def _():
        o_ref[...] = jnp.zeros_like(o_ref)


def _ffn(xs, grp, ea, eb, used, w_gate, w_up, w_down, l):
    P, W = xs.shape
    D = D_MODEL
    rows = MOE_ROWS * FFN_BLOCKS
    nb = P // rows
    FF = EXPERT_FF
    E = EXPERTS_PER_GROUP
    w_up_spec = pl.BlockSpec((1, 1, E, D, FF), lambda i, g, ea, eb, u: (l, g[FFN_BLOCKS * i], 0, 0, 0))
    w_dn_spec = pl.BlockSpec((1, 1, E, FF, D), lambda i, g, ea, eb, u: (l, g[FFN_BLOCKS * i], 0, 0, 0))
    grid_spec = pltpu.PrefetchScalarGridSpec(
        num_scalar_prefetch=4,
        grid=(nb,),
        in_specs=[pl.BlockSpec((rows, W), lambda i, g, ea, eb, u: (i, 0)),
                  w_up_spec, w_up_spec, w_dn_spec],
        out_specs=pl.BlockSpec((rows, HALF_D), lambda i, g, ea, eb, u: (i, 0)),
    )
    return pl.pallas_call(
        _ffn_body,
        grid_spec=grid_spec,
        out_shape=jax.ShapeDtypeStruct((P, HALF_D), jnp.uint32),
        compiler_params=_cparams(("arbitrary",)),
        name="moe_ffn",
    )(grp, ea, eb, used, xs, w_gate, w_up, w_down)


def _combine_body(dcur_ref, dnext_ref, x_ref, gt_ref, lg_ref, lb_ref, ys_ref, o_ref,
                  y0_ref, y1_ref, sems, *, tm, n_steps, alpha):
    i = pl.program_id(0)
    bufs = (y0_ref, y1_ref)

    n_chunks = tm // COMBINE_ROWS

    def finish(slot, prefetch):
        _wait_rows(tm, lambda r: _row_copy(ys_ref, 0, bufs[slot], 0, sems.at[slot]))
        gate = 1.0 + gt_ref[0]

        def chunk(c, _):
            r0 = pl.multiple_of(c * COMBINE_ROWS, COMBINE_ROWS)
            if prefetch:
                for t0 in range(0, COMBINE_ROWS, SUBLANES):
                    base = pl.multiple_of(r0 + t0, SUBLANES)
                    for u in range(SUBLANES):
                        pltpu.make_async_copy(ys_ref.at[pl.ds(dnext_ref[base + u], 1)],
                                              _tile_row(bufs[1 - slot], base, u), sems.at[1 - slot]).start()
            rows = pl.ds(r0, COMBINE_ROWS)
            r = alpha * x_ref[rows, :] + gate * _unpack_bf16_pairs(bufs[slot][rows, :])
            o_ref[rows, :] = _layer_norm(r, lg_ref[0], lb_ref[0])
            return 0

        lax.fori_loop(0, n_chunks, chunk, 0)

    @pl.when(i == 0)
    def _():
        _issue_rows(tm, lambda base, u: pltpu.make_async_copy(
            ys_ref.at[pl.ds(dcur_ref[base + u], 1)], _tile_row(bufs[0], base, u), sems.at[0]))

    has_next = i + 1 < n_steps
    for slot in range(2):
        mine = (i % 2) == slot

        @pl.when(mine & has_next)
        def _():
            finish(slot, True)

        @pl.when(mine & jnp.logical_not(has_next))
        def _():
            finish(slot, False)


def _combine(ys, dest, x2, mod_l, ln_g, ln_b, l, S, tm, alpha):
    T, D = x2.shape
    nS = S // tm
    n_steps = T // tm
    vec = pl.BlockSpec((1, 1, D), lambda i: (l, 0, 0))
    return pl.pallas_call(
        functools.partial(_combine_body, tm=tm, n_steps=n_steps, alpha=alpha),
        grid=(n_steps,),
        in_specs=[pl.BlockSpec((tm,), lambda i: (i,), memory_space=pltpu.SMEM),
                  pl.BlockSpec((tm,), lambda i: (jnp.minimum(i + 1, n_steps - 1),), memory_space=pltpu.SMEM),
                  pl.BlockSpec((tm, D), lambda i: (i, 0)),
                  pl.BlockSpec((1, 1, D), lambda i: (i // nS, 0, 5)),
                  vec, vec,
                  pl.BlockSpec(memory_space=pl.ANY)],
        out_specs=pl.BlockSpec((tm, D), lambda i: (i, 0)),
        out_shape=jax.ShapeDtypeStruct((T, D), F32),
        scratch_shapes=[pltpu.VMEM((tm, HALF_D), jnp.uint32), pltpu.VMEM((tm, HALF_D), jnp.uint32),
                        pltpu.SemaphoreType.DMA((2,))],
        compiler_params=_cparams(("arbitrary",)),
        name="moe_combine_ln",
    )(dest, dest, x2, mod_l, ln_g, ln_b, ys)


def _prep_w_in(w_in):
    L, D, _ = w_in.shape
    o = 0
    cq = w_in[..., o:o + MLA_Q_RANK]; o += MLA_Q_RANK
    ckv = w_in[..., o:o + MLA_KV_RANK]; o += MLA_KV_RANK
    kr = w_in[..., o:o + MLA_ROPE]; o += MLA_ROPE
    gq = w_in[..., o:o + GDN_WIDTH]; o += GDN_WIDTH
    gk = w_in[..., o:o + GDN_WIDTH]; o += GDN_WIDTH
    gv = w_in[..., o:o + GDN_WIDTH]; o += GDN_WIDTH
    gz = w_in[..., o:o + GDN_WIDTH]; o += GDN_WIDTH
    ga = w_in[..., o:o + GDN_HEADS]; o += GDN_HEADS
    gb = w_in[..., o:o + GDN_HEADS]; o += GDN_HEADS
    pw = w_in[..., o:o + POOL_WIDTH]
    half = MLA_ROPE // 2
    z = lambda n: jnp.zeros((L, D, n), w_in.dtype)
    kra = jnp.concatenate([z(MLA_NOPE), kr, z(HEAD_PAD - MLA_NOPE - MLA_ROPE)], axis=-1)
    krb = jnp.concatenate([ga, gb, z(MLA_NOPE - 2 * GDN_HEADS), kr[..., half:], kr[..., :half],
                           z(HEAD_PAD - MLA_NOPE - MLA_ROPE)], axis=-1)
    out = jnp.concatenate([gq, gk, gv, gz, cq, pw, ckv, kra, krb], axis=-1)
    assert out.shape[-1] == IN_COLS
    return out.astype(BF16)


def _prep_mla(w_uq, w_ukv):
    L = w_uq.shape[0]
    H, half = MLA_HEADS, MLA_ROPE // 2
    pad = HEAD_PAD - MLA_NOPE - MLA_ROPE
    q = w_uq.reshape(L, MLA_Q_RANK, H, MLA_NOPE + MLA_ROPE)
    nope, r1, r2 = q[..., :MLA_NOPE], q[..., MLA_NOPE:MLA_NOPE + half], q[..., MLA_NOPE + half:]
    zq = lambda n: jnp.zeros((L, MLA_Q_RANK, H, n), w_uq.dtype)
    plain = jnp.concatenate([nope, r1, r2, zq(pad)], axis=-1).reshape(L, MLA_Q_RANK, H * HEAD_PAD)
    partner = jnp.concatenate([zq(MLA_NOPE), r2, r1, zq(pad)], axis=-1).reshape(L, MLA_Q_RANK, H * HEAD_PAD)
    wq2 = jnp.concatenate([plain, partner], axis=-1).astype(BF16)
    kv = w_ukv.reshape(L, MLA_KV_RANK, H, MLA_NOPE + MLA_V)
    k_nope, v = kv[..., :MLA_NOPE], kv[..., MLA_NOPE:]
    zk = lambda n: jnp.zeros((L, MLA_KV_RANK, H, n), w_ukv.dtype)
    k_main = jnp.concatenate([k_nope, zk(HEAD_PAD - MLA_NOPE)], axis=-1).reshape(L, MLA_KV_RANK, H * HEAD_PAD)
    even = (jnp.arange(H) % 2 == 0)[None, None, :, None]
    v_pair = jnp.where(even, jnp.concatenate([v, zk(MLA_V)], axis=-1), jnp.concatenate([zk(MLA_V), v], axis=-1))
    wkv2 = jnp.concatenate([k_main, v_pair.reshape(L, MLA_KV_RANK, H * HEAD_PAD)], axis=-1).astype(BF16)
    return wq2, wkv2


def _rope_tables(S):
    half = MLA_ROPE // 2
    inv_freq = jnp.power(ROPE_THETA, -jnp.arange(0, MLA_ROPE, 2, dtype=F32) / MLA_ROPE)
    ang = jnp.arange(S, dtype=F32)[:, None] * inv_freq[None, :]
    cos, sin = jnp.cos(ang), jnp.sin(ang)
    pad = jnp.zeros((S, HEAD_PAD - MLA_NOPE - MLA_ROPE), F32)
    cos_t = jnp.concatenate([jnp.ones((S, MLA_NOPE), F32), cos, cos, pad], axis=-1)
    sin_t = jnp.concatenate([jnp.zeros((S, MLA_NOPE), F32), -sin, sin, pad], axis=-1)
    scale = (MLA_NOPE + MLA_ROPE) ** -0.5
    return cos_t * scale, sin_t * scale, cos_t, sin_t


def _block_diag(blocks):
    L, G, n, _ = blocks.shape
    eye = jnp.eye(G, dtype=blocks.dtype)
    return jnp.einsum('lgij,gh->lgihj', blocks, eye).reshape(L, G * n, G * n)


def kernel(x, c, w_in, mla_q_norm, mla_kv_norm, mla_w_uq, mla_w_ukv, gdn_conv, gdn_a_log, gdn_dt_bias, gdn_out_norm, pool_w, pool_scale, w_out, w_mod, b_mod, ln1_g, ln1_b, ln2_g, ln2_b, router_w_group, router_b_group, router_w_expert, router_b_expert, moe_w_gate, moe_w_up, moe_w_down):
    B, S, D = x.shape
    L = w_in.shape[0]
    T = B * S
    alpha = (2 * L) ** 0.25
    ts = min(512, S)
    tq = min(256, S)
    t_moe = min(512, S)
    t_disp = min(1024, S)
    assert D == D_MODEL and S % ts == 0 and S % tq == 0 and ts % CHUNK == 0
    assert S % t_disp == 0 and t_moe % DMA_WAIT_UNROLL == 0

    w_in2 = _prep_w_in(w_in)
    wq2, wkv2 = _prep_mla(mla_w_uq, mla_w_ukv)
    tabs = _rope_tables(S)
    gq = mla_q_norm.reshape(L, 1, MLA_Q_RANK)
    gkv = mla_kv_norm.reshape(L, 1, MLA_KV_RANK)
    alog_e = jnp.repeat(gdn_a_log, GDN_DIM, axis=-1).reshape(L, 1, GDN_WIDTH)
    dt_e = jnp.repeat(gdn_dt_bias, GDN_DIM, axis=-1).reshape(L, 1, GDN_WIDTH)
    og_e = jnp.tile(gdn_out_norm, (1, GDN_HEADS)).reshape(L, 1, GDN_WIDTH)
    lane_head = jnp.arange(GDN_WIDTH) // GDN_DIM
    seg = (lane_head[:, None] == lane_head[None, :]).astype(BF16)
    tri_c = (jnp.arange(CHUNK)[:, None] >= jnp.arange(CHUNK)[None, :]).astype(BF16)
    src_lane = jnp.arange(LANE)[:, None]
    out_col = jnp.arange(2 * GDN_WIDTH)[None, :]
    ab_expand = (src_lane == (out_col // GDN_WIDTH) * GDN_HEADS + (out_col % GDN_WIDTH) // GDN_DIM).astype(BF16)
    pool_bd = _block_diag(pool_w).astype(BF16)
    pool_sc = pool_scale.reshape(L, 1, POOL_WIDTH)
    w_out_b = w_out.astype(BF16)
    w_r = jnp.concatenate([router_w_group, router_w_expert,
                           jnp.zeros((L, D, LANE - N_GROUPS - N_EXPERTS), F32)], axis=-1)
    w_r_hi = w_r.astype(BF16)
    w_r_lo = (w_r - w_r_hi.astype(F32)).astype(BF16)
    b_r = jnp.concatenate([router_b_group, router_b_expert,
                           jnp.zeros((L, LANE - N_GROUPS - N_EXPERTS), F32)], axis=-1).reshape(L, 1, LANE)
    tri_r = (jnp.arange(ts)[:, None] > jnp.arange(ts)[None, :]).astype(BF16)
    ln1g, ln1b = ln1_g.reshape(L, 1, D), ln1_b.reshape(L, 1, D)
    ln2g, ln2b = ln2_g.reshape(L, 1, D), ln2_b.reshape(L, 1, D)

    assert N_BUCKETS <= LANE
    step_rows = MOE_ROWS * FFN_BLOCKS
    max_rows = T + N_BUCKETS * (MOE_ROWS - 1) + N_GROUPS * (step_rows - MOE_ROWS)
    P = (max_rows + step_rows - 1) // step_rows * step_rows
    nb = P // MOE_ROWS
    pg, pa, pb = [], [], []
    for g_ in range(N_GROUPS):
        for a_ in range(EXPERTS_PER_GROUP):
            for b_ in range(a_ + 1, EXPERTS_PER_GROUP):
                pg.append(g_)
                pa.append(a_)
                pb.append(b_)
    bucket_g = jnp.asarray(pg, jnp.int32)
    bucket_a = jnp.asarray(pa, jnp.int32)
    bucket_b = jnp.asarray(pb, jnp.int32)
    grouped = lambda w: w.astype(BF16).reshape((L, N_GROUPS, EXPERTS_PER_GROUP) + w.shape[2:])
    wg_b, wu_b, wd_b = grouped(moe_w_gate), grouped(moe_w_up), grouped(moe_w_down)
    bucket_ids = jnp.arange(N_BUCKETS, dtype=jnp.int32)
    blk0 = jnp.arange(nb, dtype=jnp.int32) * MOE_ROWS
    xs = jnp.zeros((P, HALF_D + LANE), jnp.uint32)

    mod = _modulation(c, w_mod, b_mod)
    x2 = x.reshape(T, D)
    for l in range(L):
        mod_l = mod[l].reshape(B, 1, 6 * D)
        proj = _inproj(x2, mod_l, w_in2, l, S, ts)
        q, k, v = _mla_proj(proj, tabs, gq, gkv, wq2, wkv2, l, S, ts)
        y_mla = _attention(q, k, v, B, S, tq)
        y_gdn = _gdn(proj, ab_expand, gdn_conv, alog_e, dt_e, og_e, seg, tri_c, l, B, S, ts)
        y_pool = _pool(proj, pool_bd, pool_sc, l, B, S, ts)
        x2 = _outproj(y_mla, y_gdn, y_pool, w_out_b, x2, mod_l, ln1g, ln1b, l, S, ts, alpha)

        hm, cnt = _router(x2, mod_l, w_r_hi, w_r_lo, b_r, tri_r, l, S, ts)
        counts = cnt[0, :N_BUCKETS].astype(jnp.int32)
        padded = ((counts + MOE_ROWS - 1) // MOE_ROWS * MOE_ROWS).reshape(N_GROUPS, PAIRS_PER_GROUP)
        group_rows = jnp.sum(padded, axis=1)
        group_pad = (step_rows - group_rows % step_rows) % step_rows
        padded = padded.at[:, PAIRS_PER_GROUP - 1].add(group_pad).reshape(N_BUCKETS)
        pad_end = jnp.cumsum(padded)
        pad_start = pad_end - padded
        route = lax.bitcast_convert_type(hm[:, HALF_D:HALF_D + 2], F32).astype(jnp.int32)
        bucket, rank = route[:, 0], route[:, 1]
        dest = jnp.sum(jnp.where(bucket[:, None] == bucket_ids[None, :], pad_start[None, :], 0), axis=1) + rank
        block_bucket = jnp.minimum(jnp.sum((pad_end[None, :] <= blk0[:, None]).astype(jnp.int32), axis=1),
                                   N_BUCKETS - 1)
        onehot_b = (block_bucket[:, None] == bucket_ids[None, :]).astype(jnp.int32)
        grp = jnp.sum(onehot_b * bucket_g[None, :], axis=1)
        ea = jnp.sum(onehot_b * bucket_a[None, :], axis=1)
        eb = jnp.sum(onehot_b * bucket_b[None, :], axis=1)
        used = (blk0 < pad_end[-1]).astype(jnp.int32)

        xs = _dispatch(hm, dest, xs, t_disp)
        ys = _ffn(xs, grp, ea, eb, used, wg_b, wu_b, wd_b, l)
        x2 = _combine(ys, dest, x2, mod_l, ln2g, ln2b, l, S, t_moe, alpha)
    return x2.reshape(B, S, D)
```

```python
import functools
import math

import jax
import jax.numpy as jnp
from jax import lax
from jax.experimental import pallas as pl
from jax.experimental.pallas import tpu as pltpu

F32 = jnp.float32
BF16 = jnp.bfloat16

D_MODEL = 1024
HALF_D = D_MODEL // 2
CHUNK = 64
MLA_HEADS = 8
MLA_NOPE = 64
MLA_ROPE = 32
MLA_V = 64
MLA_Q_RANK = 256
MLA_KV_RANK = 128
MLA_WIDTH = MLA_HEADS * MLA_V
ROPE_THETA = 10000.0
GDN_HEADS = 4
GDN_DIM = 64
GDN_WIDTH = GDN_HEADS * GDN_DIM
GDN_CONV = 4
POOL_WIDTH = 256
POOL_WINDOWS = (2, 4, 8, 16)
POOL_GROUP_DIM = 64
N_GROUPS = 4
EXPERTS_PER_GROUP = 8
N_EXPERTS = N_GROUPS * EXPERTS_PER_GROUP
TOP_K = 2
EXPERT_FF = 256
LN_EPS = 1e-5
RMS_EPS = 1e-6

LANE = 128
SUBLANES = 8
HEAD_PAD = 128
PAIRS_PER_GROUP = EXPERTS_PER_GROUP * (EXPERTS_PER_GROUP - 1) // 2
N_BUCKETS = N_GROUPS * PAIRS_PER_GROUP
MOE_ROWS = 128
VMEM_LIMIT = 48 * 1024 * 1024

_C_GQ, _C_GK, _C_GV, _C_GZ = 0, 256, 512, 768
_C_CQ, _C_POOL = 1024, 1280
_C_CKV, _C_KRA, _C_KRB = 1536, 1664, 1792
IN_COLS = 1920


def _cparams(sem):
    return pltpu.CompilerParams(dimension_semantics=sem, vmem_limit_bytes=VMEM_LIMIT)


def _sigmoid(x):
    return 1.0 / (1.0 + jnp.exp(-x))


def _split2(x):
    hi = x.astype(BF16)
    lo = (x - hi.astype(F32)).astype(BF16)
    return hi, lo


def _split3(x):
    hi = x.astype(BF16)
    r = x - hi.astype(F32)
    mid = r.astype(BF16)
    lo = (r - mid.astype(F32)).astype(BF16)
    return hi, mid, lo


def _pack_bf16_pairs(x):
    n = x.shape[1] // 2
    lo = lax.bitcast_convert_type(x[:, :n].astype(BF16).astype(F32), jnp.uint32)
    hi = lax.bitcast_convert_type(x[:, n:].astype(BF16).astype(F32), jnp.uint32)
    return (hi & jnp.uint32(0xFFFF0000)) | (lo >> 16)


def _unpack_bf16_pairs(p):
    lo = lax.bitcast_convert_type(p << 16, F32)
    hi = lax.bitcast_convert_type(p & jnp.uint32(0xFFFF0000), F32)
    return jnp.concatenate([lo, hi], axis=1)


def _dot(a, b):
    return jnp.dot(a, b, preferred_element_type=F32)


def _dot_nt(a, b):
    return lax.dot_general(a, b, (((1,), (1,)), ((), ())), preferred_element_type=F32)


def _dot_tn(a, b):
    return lax.dot_general(a, b, (((0,), (0,)), ((), ())), preferred_element_type=F32)


def _mod_body(c_ref, w_ref, b_ref, o_ref):
    c = c_ref[...]
    ca = c * _sigmoid(c)
    o_ref[0] = _dot(ca.astype(BF16), w_ref[0].astype(BF16)) + b_ref[0]


def _modulation(c, w_mod, b_mod):
    L, D, N = w_mod.shape
    B = c.shape[0]
    tn = 1024
    return pl.pallas_call(
        _mod_body,
        grid=(L, N // tn),
        in_specs=[pl.BlockSpec((B, D), lambda l, j: (0, 0)),
                  pl.BlockSpec((1, D, tn), lambda l, j: (l, 0, j)),
                  pl.BlockSpec((1, 1, tn), lambda l, j: (l, 0, j))],
        out_specs=pl.BlockSpec((1, B, tn), lambda l, j: (l, 0, j)),
        out_shape=jax.ShapeDtypeStruct((L, B, N), F32),
        compiler_params=_cparams(("arbitrary", "arbitrary")),
        name="modulation",
    )(c, w_mod, b_mod.reshape(L, 1, N))


def _inproj_body(x_ref, sh_ref, sc_ref, w_ref, o_ref):
    h = x_ref[...] * (1.0 + sc_ref[0]) + sh_ref[0]
    o_ref[...] = _dot(h.astype(BF16), w_ref[0])


def _inproj(x2, mod_l, w_in2, l, S, tm):
    T, D = x2.shape
    nS = S // tm
    return pl.pallas_call(
        _inproj_body,
        grid=(T // tm,),
        in_specs=[pl.BlockSpec((tm, D), lambda i: (i, 0)),
                  pl.BlockSpec((1, 1, D), lambda i: (i // nS, 0, 0)),
                  pl.BlockSpec((1, 1, D), lambda i: (i // nS, 0, 1)),
                  pl.BlockSpec((1, D, IN_COLS), lambda i: (l, 0, 0))],
        out_specs=pl.BlockSpec((tm, IN_COLS), lambda i: (i, 0)),
        out_shape=jax.ShapeDtypeStruct((T, IN_COLS), F32),
        compiler_params=_cparams(("arbitrary",)),
        name="inproj",
    )(x2, mod_l, mod_l, w_in2)


def _mla_proj_body(cq_ref, ckv_ref, kra_ref, krb_ref, cosq_ref, sinq_ref, cosk_ref, sink_ref,
                   gq_ref, gkv_ref, wq_ref, wkv_ref, q_out, k_out, v_out):
    cq = cq_ref[...]
    qn = cq * lax.rsqrt(jnp.mean(cq * cq, axis=-1, keepdims=True) + RMS_EPS) * gq_ref[0]
    q2 = _dot(qn.astype(BF16), wq_ref[0])
    ckv = ckv_ref[...]
    kvn = ckv * lax.rsqrt(jnp.mean(ckv * ckv, axis=-1, keepdims=True) + RMS_EPS) * gkv_ref[0]
    kv2 = _dot(kvn.astype(BF16), wkv_ref[0])
    cq_t, sq_t = cosq_ref[...], sinq_ref[...]
    krope = kra_ref[...] * cosk_ref[...] + krb_ref[...] * sink_ref[...]
    hw = MLA_HEADS * HEAD_PAD
    for h in range(MLA_HEADS):
        a, b = h * HEAD_PAD, (h + 1) * HEAD_PAD
        q_out[:, a:b] = (q2[:, a:b] * cq_t + q2[:, hw + a:hw + b] * sq_t).astype(BF16)
        k_out[:, a:b] = (kv2[:, a:b] + krope).astype(BF16)
    vl = lax.broadcasted_iota(jnp.int32, (1, hw), 1)
    ones_lane = jnp.where((vl // HEAD_PAD) % 2 == 0, MLA_V, 0)
    v_out[...] = (kv2[:, hw:] + (vl % HEAD_PAD == ones_lane).astype(F32)).astype(BF16)


def _mla_proj(proj, tabs, gq, gkv, wq2, wkv2, l, S, ts):
    T = proj.shape[0]
    nS = S // ts
    hw = MLA_HEADS * HEAD_PAD
    tab_spec = pl.BlockSpec((ts, LANE), lambda i: (i % nS, 0))
    return pl.pallas_call(
        _mla_proj_body,
        grid=(T // ts,),
        in_specs=[pl.BlockSpec((ts, 256), lambda i: (i, _C_CQ // 256)),
                  pl.BlockSpec((ts, 128), lambda i: (i, _C_CKV // 128)),
                  pl.BlockSpec((ts, 128), lambda i: (i, _C_KRA // 128)),
                  pl.BlockSpec((ts, 128), lambda i: (i, _C_KRB // 128)),
                  tab_spec, tab_spec, tab_spec, tab_spec,
                  pl.BlockSpec((1, 1, MLA_Q_RANK), lambda i: (l, 0, 0)),
                  pl.BlockSpec((1, 1, MLA_KV_RANK), lambda i: (l, 0, 0)),
                  pl.BlockSpec((1, MLA_Q_RANK, 2 * hw), lambda i: (l, 0, 0)),
                  pl.BlockSpec((1, MLA_KV_RANK, 2 * hw), lambda i: (l, 0, 0))],
        out_specs=[pl.BlockSpec((ts, hw), lambda i: (i, 0))] * 3,
        out_shape=[jax.ShapeDtypeStruct((T, hw), BF16)] * 3,
        compiler_params=_cparams(("arbitrary",)),
        name="mla_proj",
    )(proj, proj, proj, proj, *tabs, gq, gkv, wq2, wkv2)


def _attn_body(qi_tab, kj_tab, q_ref, k_ref, v_ref, o_ref, m_s, acc_s, *, tq, n_q, n_tiles):
    row_c = lax.broadcasted_iota(jnp.int32, (tq, tq), 0) // CHUNK
    col_c = lax.broadcasted_iota(jnp.int32, (tq, tq), 1) // CHUNK
    allowed = col_c <= row_c
    lane = lax.broadcasted_iota(jnp.int32, (tq, HEAD_PAD), 1)
    sl = [slice(hh * HEAD_PAD, (hh + 1) * HEAD_PAD) for hh in range(2)]
    hs = range(2)

    def rows_of(tile):
        return pl.ds(pl.multiple_of(tile * tq, tq), tq)

    def stage_a(t):
        q = q_ref[rows_of(qi_tab[t]), :]
        kk = k_ref[rows_of(kj_tab[t]), :]
        return tuple(_dot_nt(q[:, sl[hh]], kk[:, sl[hh]]).astype(BF16) for hh in hs)

    def stage_b(t, s_pair):
        qi, kj = qi_tab[t], kj_tab[t]
        rows = rows_of(qi)
        keep = jnp.logical_or(allowed, kj < qi)
        neg = jnp.asarray(-jnp.inf, BF16)
        s = [jnp.where(keep, s_pair[hh], neg) for hh in hs]
        m_prev = [m_s[hh, rows, :] for hh in hs]
        m_new = [jnp.maximum(m_prev[hh], jnp.max(s[hh], axis=-1, keepdims=True)) for hh in hs]
        p = tuple(jnp.exp(s[hh] - jnp.concatenate([m_new[hh]] * (tq // HEAD_PAD), axis=1)) for hh in hs)
        alpha = tuple(jnp.exp(m_prev[hh] - m_new[hh]) for hh in hs)
        for hh in hs:
            m_s[hh, rows, :] = m_new[hh]
        return p, alpha

    def stage_c(t, p, alpha):
        qi, kj = qi_tab[t], kj_tab[t]
        rows = rows_of(qi)
        vv = v_ref[rows_of(kj), :]
        pv = [_dot(p[hh], vv[:, sl[hh]]) for hh in hs]
        for hh in hs:
            acc_s[hh, rows, :] = alpha[hh].astype(F32) * acc_s[hh, rows, :] + pv[hh]

    def body(t, c):
        s_next, (p, alpha) = c
        s_after = stage_a(t + 2)
        stage_c(t, p, alpha)
        return s_after, stage_b(t + 1, s_next)

    m_s[...] = jnp.full(m_s.shape, -jnp.inf, m_s.dtype)
    acc_s[...] = jnp.zeros_like(acc_s)
    pb0 = stage_b(0, stage_a(0))
    s_last, pb = lax.fori_loop(0, n_tiles - 2, body, (stage_a(1), pb0), unroll=ATTN_UNROLL)
    pb_last = stage_b(n_tiles - 1, s_last)
    stage_c(n_tiles - 2, *pb)
    stage_c(n_tiles - 1, *pb_last)

    def normalize(i, _):
        rows = rows_of(i)
        acc0, acc1 = acc_s[0, rows, :], acc_s[1, rows, :]
        o0 = jnp.where(lane < MLA_V, acc0 * (1.0 / acc0[:, MLA_V:MLA_V + 1]), 0.0)
        o1 = jnp.where(lane >= MLA_V, acc1 * (1.0 / acc1[:, 0:1]), 0.0)
        o_ref[rows, :] = (o0 + o1).astype(BF16)
        return 0

    lax.fori_loop(0, n_q, normalize, 0, unroll=2)


def _attention(q, k, v, B, S, tq):
    T = q.shape[0]
    nq = S // tq
    pairs = MLA_HEADS // 2
    tiles = [(i, j) for i in range(nq) for j in range(i + 1)]
    assert len(tiles) >= 2
    qi_tab = jnp.asarray([t[0] for t in tiles], jnp.int32)
    kj_tab = jnp.asarray([t[1] for t in tiles], jnp.int32)
    seq = pl.BlockSpec((S, 2 * HEAD_PAD), lambda b, p, qt, kt: (b, p))
    grid_spec = pltpu.PrefetchScalarGridSpec(
        num_scalar_prefetch=2,
        grid=(B, pairs),
        in_specs=[seq, seq, seq],
        out_specs=pl.BlockSpec((S, 2 * MLA_V), lambda b, p, qt, kt: (b, p)),
        scratch_shapes=[pltpu.VMEM((2, S, HEAD_PAD), BF16), pltpu.VMEM((2, S, HEAD_PAD), F32)],
    )
    return pl.pallas_call(
        functools.partial(_attn_body, tq=tq, n_q=nq, n_tiles=len(tiles)),
        grid_spec=grid_spec,
        out_shape=jax.ShapeDtypeStruct((T, MLA_WIDTH), BF16),
        compiler_params=_cparams(("arbitrary", "arbitrary")),
        name="mla_attention",
    )(qi_tab, kj_tab, q, k, v)


def _pool_body(p_ref, w_ref, sc_ref, o_ref, prev_ref, *, ts):
    j = pl.program_id(1)
    hist = 16

    @pl.when(j == 0)
    def _():
        prev_ref[...] = jnp.zeros_like(prev_ref)

    cur = p_ref[...]
    x = jnp.concatenate([prev_ref[...], cur], axis=0)
    s1 = x + pltpu.roll(x, 1, axis=0)
    s2 = s1 + pltpu.roll(s1, 2, axis=0)
    s4 = s2 + pltpu.roll(s2, 4, axis=0)
    s8 = s4 + pltpu.roll(s4, 8, axis=0)
    pos = (j * ts + lax.broadcasted_iota(jnp.int32, (ts, POOL_WIDTH), 0) + 1).astype(F32)
    lane = lax.broadcasted_iota(jnp.int32, (ts, POOL_WIDTH), 1)
    sums = (s1, s2, s4, s8)
    mean = None
    for gi, win in enumerate(POOL_WINDOWS):
        m_g = sums[gi][hist:] / jnp.minimum(pos, float(win))
        mean = m_g if mean is None else jnp.where(lane >= gi * POOL_GROUP_DIM, m_g, mean)
    delta = mean - cur
    o_ref[...] = (_dot(delta.astype(BF16), w_ref[0]) * sc_ref[0]).astype(BF16)
    prev_ref[...] = cur[ts - hist:]


def _pool(proj, w_bd, scale, l, B, S, ts):
    T = proj.shape[0]
    nS = S // ts
    return pl.pallas_call(
        functools.partial(_pool_body, ts=ts),
        grid=(B, nS),
        in_specs=[pl.BlockSpec((ts, POOL_WIDTH), lambda b, j: (b * nS + j, _C_POOL // 256)),
                  pl.BlockSpec((1, POOL_WIDTH, POOL_WIDTH), lambda b, j: (l, 0, 0)),
                  pl.BlockSpec((1, 1, POOL_WIDTH), lambda b, j: (l, 0, 0))],
        out_specs=pl.BlockSpec((ts, POOL_WIDTH), lambda b, j: (b * nS + j, 0)),
        out_shape=jax.ShapeDtypeStruct((T, POOL_WIDTH), BF16),
        scratch_shapes=[pltpu.VMEM((16, POOL_WIDTH), F32)],
        compiler_params=_cparams(("arbitrary", "arbitrary")),
        name="pool_mixer",
    )(proj, w_bd, scale)


def _head_sum(x, seg_ref):
    return _dot(x.astype(BF16), seg_ref[...])


def _gdn_inputs(s, raw_q, raw_k, raw_v, raw_ab, expand_ref, conv_ref, alog_ref, dt_ref, seg_ref,
                q_ref, k_ref, v_ref, g_ref, b_ref, prev_ref, ts):
    hist = SUBLANES
    cw = conv_ref[0]
    outs = []
    for idx, ref in enumerate((raw_q, raw_k, raw_v)):
        cur = ref[s]
        a, b = idx * GDN_WIDTH, (idx + 1) * GDN_WIDTH
        x = jnp.concatenate([prev_ref[s, :, a:b], cur], axis=0)
        w = cw[:, a:b]
        y = (w[3:4] * x + w[2:3] * pltpu.roll(x, 1, axis=0) + w[1:2] * pltpu.roll(x, 2, axis=0)
             + w[0:1] * pltpu.roll(x, 3, axis=0))[hist:]
        outs.append(y * _sigmoid(y))
        prev_ref[s, :, a:b] = cur[ts - hist:]
    qc, kc, vc = outs
    q_ref[s] = qc * lax.rsqrt(_head_sum(qc * qc, seg_ref) + RMS_EPS) * (GDN_DIM ** -0.5)
    k_ref[s] = kc * lax.rsqrt(_head_sum(kc * kc, seg_ref) + RMS_EPS)
    v_ref[s] = vc
    ab_hi, ab_mid, ab_lo = _split3(raw_ab[s])
    e = expand_ref[...]
    ab = _dot(ab_hi, e) + _dot(ab_mid, e) + _dot(ab_lo, e)
    z = ab[:, :GDN_WIDTH] + dt_ref[0]
    softplus = jnp.maximum(z, 0.0) + jnp.log(1.0 + jnp.exp(-jnp.abs(z)))
    g_ref[s] = -jnp.exp(alog_ref[0]) * softplus
    b_ref[s] = _sigmoid(ab[:, GDN_WIDTH:])


def _gdn_body(raw_q, raw_k, raw_v, raw_ab, z_ref, expand_ref, conv_ref, alog_ref, dt_ref, og_ref, seg_ref, tri_ref,
              o_ref, state_ref, prev_ref, q_ref, k_ref, v_ref, g_ref, b_ref,
              u_s, w_s, qk_s, qd_s, kd_s, gl_s, *, ts):
    j = pl.program_id(1)
    C, W, H = CHUNK, GDN_WIDTH, GDN_HEADS
    n_chunks = ts // C
    groups = n_chunks // INTRA_UNROLL

    @pl.when(j == 0)
    def _():
        state_ref[...] = jnp.zeros_like(state_ref)
        prev_ref[...] = jnp.zeros_like(prev_ref)

    for s in range(GDN_SEQS):
        _gdn_inputs(s, raw_q, raw_k, raw_v, raw_ab, expand_ref, conv_ref, alog_ref, dt_ref, seg_ref,
                    q_ref, k_ref, v_ref, g_ref, b_ref, prev_ref, ts)

    lane = lax.broadcasted_iota(jnp.int32, (C, W), 1)
    row = lax.broadcasted_iota(jnp.int32, (C, W), 0)
    col_tok = lane % C
    incl = row >= col_tok
    strict = row > col_tok
    eye = (row == col_tok).astype(F32)
    head_masks = [(lane // C == h).astype(F32) for h in range(H)]
    bd_mask = (lax.broadcasted_iota(jnp.int32, (W, W), 0) // C
               == lax.broadcasted_iota(jnp.int32, (W, W), 1) // C)

    def expand(m):
        return jnp.concatenate([m * hm for hm in head_masks], axis=0)

    def bdot(lhs, rhs):
        return _dot(lhs.astype(BF16), rhs.astype(BF16))

    def intra(i, _):
        sq = i // groups
        cs = [(i % groups) * INTRA_UNROLL + uu for uu in range(INTRA_UNROLL)]
        r0s = [pl.multiple_of(c * C, C) for c in cs]
        n = range(INTRA_UNROLL)
        tri = tri_ref[...]
        gs = [_split3(g_ref[sq, pl.ds(r0, C), :]) for r0 in r0s]
        gc = [_dot(tri, g[0]) + _dot(tri, g[1]) + _dot(tri, g[2]) for g in gs]
        ks = [k_ref[sq, pl.ds(r0, C), :] for r0 in r0s]
        qs = [q_ref[sq, pl.ds(r0, C), :] for r0 in r0s]
        betas = [b_ref[sq, pl.ds(r0, C), :] for r0 in r0s]
        kb = [ks[x] * betas[x] for x in n]
        aq = [_dot_nt(jnp.concatenate([kb[x], qs[x]], axis=0).astype(BF16), expand(ks[x]).astype(BF16)) for x in n]
        g_row = [jnp.sum(gc[x] * eye, axis=0, keepdims=True) for x in n]
        g_last = [gc[x][C - 1:C, :] for x in n]
        decay = [jnp.where(incl, jnp.exp(jnp.where(incl, gc[x] - g_row[x], 0.0)), 0.0) for x in n]
        e_gc = [jnp.exp(gc[x]) for x in n]
        a_cat = [jnp.where(strict, aq[x][:C] * decay[x], 0.0) for x in n]
        for x in n:
            r0 = r0s[x]
            qk_s[sq, pl.ds(r0, C), :] = jnp.where(incl, aq[x][C:] * decay[x], 0.0).astype(BF16)
            qd_s[sq, pl.ds(r0, C), :] = (qs[x] * e_gc[x]).astype(BF16)
            kd_s[sq, pl.ds(r0, C), :] = (ks[x] * jnp.exp(g_last[x] - gc[x])).astype(BF16)
            gl_s[sq, pl.ds(pl.multiple_of(cs[x] * 8, 8), 8), :] = jnp.broadcast_to(jnp.exp(g_last[x]), (8, W))
        x_cat = [eye - a_cat[x] for x in n]
        p_cat = [bdot(a_cat[x], expand(a_cat[x])) for x in n]
        n_fac = int(math.log2(C)) - 1
        for r in range(n_fac):
            last = r == n_fac - 1
            xp = [bdot(x_cat[x] if last else jnp.concatenate([x_cat[x], p_cat[x]], axis=0), expand(p_cat[x]))
                  for x in n]
            x_cat = [x_cat[x] + xp[x][:C] for x in n]
            if not last:
                p_cat = [xp[x][C:] for x in n]
        for x in n:
            r0 = r0s[x]
            t_cat = x_cat[x].astype(BF16)
            v = v_ref[sq, pl.ds(r0, C), :]
            u_s[sq, pl.ds(r0, C), :] = _dot(t_cat, expand(v * betas[x]).astype(BF16))
            w_s[sq, pl.ds(r0, C), :] = _dot(t_cat, expand(kb[x] * e_gc[x]).astype(BF16)).astype(BF16)
        return 0

    lax.fori_loop(0, GDN_SEQS * groups, intra, 0)

    def scan(c, _):
        r0 = pl.multiple_of(c * C, C)
        sq = range(GDN_SEQS)
        state = [state_ref[s] for s in sq]
        wq = [_dot(jnp.concatenate([w_s[s, pl.ds(r0, C), :], qd_s[s, pl.ds(r0, C), :]], axis=0),
                   state[s].astype(BF16)) for s in sq]
        v_new = [u_s[s, pl.ds(r0, C), :] - wq[s][:C] for s in sq]
        upd = [_dot_tn(kd_s[s, pl.ds(r0, C), :], v_new[s].astype(BF16)) for s in sq]
        for s in sq:
            g_l = gl_s[s, pl.ds(pl.multiple_of(c * 8, 8), 1), :]
            state_ref[s] = state[s] * g_l + jnp.where(bd_mask, upd[s], 0.0)
        o = [wq[s][C:] + _dot(qk_s[s, pl.ds(r0, C), :], expand(v_new[s]).astype(BF16)) for s in sq]
        ms = [_head_sum(o[s] * o[s], seg_ref) * (1.0 / GDN_DIM) for s in sq]
        for s in sq:
            z = z_ref[s, pl.ds(r0, C), :]
            y = o[s] * lax.rsqrt(ms[s] + RMS_EPS) * og_ref[0] * (z * _sigmoid(z))
            o_ref[s, pl.ds(r0, C), :] = y.astype(BF16)
        return 0

    lax.fori_loop(0, n_chunks, scan, 0, unroll=2)


def _gdn(proj, expand, conv_w, alog_e, dt_e, og, seg, tri, l, B, S, ts):
    T = proj.shape[0]
    nS = S // ts
    W = GDN_WIDTH
    Q = GDN_SEQS
    proj3 = proj.reshape(B, S, proj.shape[-1])

    def col(c):
        return pl.BlockSpec((Q, ts, W), lambda b, j: (b, j, c // W))

    vec = pl.BlockSpec((1, 1, W), lambda b, j: (l, 0, 0))
    seq_f32 = pltpu.VMEM((Q, ts, W), F32)
    out = pl.pallas_call(
        functools.partial(_gdn_body, ts=ts),
        grid=(B // Q, nS),
        in_specs=[col(_C_GQ), col(_C_GK), col(_C_GV),
                  pl.BlockSpec((Q, ts, LANE), lambda b, j: (b, j, _C_KRB // LANE)),
                  col(_C_GZ),
                  pl.BlockSpec((LANE, 2 * W), lambda b, j: (0, 0)),
                  pl.BlockSpec((1, GDN_CONV, 3 * W), lambda b, j: (l, 0, 0)),
                  vec, vec, vec,
                  pl.BlockSpec((W, W), lambda b, j: (0, 0)),
                  pl.BlockSpec((CHUNK, CHUNK), lambda b, j: (0, 0))],
        out_specs=pl.BlockSpec((Q, ts, W), lambda b, j: (b, j, 0)),
        out_shape=jax.ShapeDtypeStruct((B, S, W), BF16),
        scratch_shapes=[pltpu.VMEM((Q, W, W), F32), pltpu.VMEM((Q, SUBLANES, 3 * W), F32)]
        + [seq_f32] * 5 + [seq_f32] + [pltpu.VMEM((Q, ts, W), BF16)] * 4
        + [pltpu.VMEM((Q, 8 * (ts // CHUNK), W), F32)],
        compiler_params=_cparams(("arbitrary", "arbitrary")),
        name="gdn_delta_rule",
    )(proj3, proj3, proj3, proj3, proj3, expand, conv_w, alog_e, dt_e, og, seg, tri)
    return out.reshape(T, W)


def _layer_norm(r, g, b):
    mu = jnp.mean(r, axis=-1, keepdims=True)
    d = r - mu
    var = jnp.mean(d * d, axis=-1, keepdims=True)
    return d * lax.rsqrt(var + LN_EPS) * g + b


def _outproj_body(ym_ref, yg_ref, yp_ref, w_ref, x_ref, gt_ref, lg_ref, lb_ref, o_ref, *, alpha):
    w = w_ref[0]
    y = (_dot(ym_ref[...], w[:MLA_WIDTH]) + _dot(yg_ref[...], w[MLA_WIDTH:MLA_WIDTH + GDN_WIDTH])
         + _dot(yp_ref[...], w[MLA_WIDTH + GDN_WIDTH:]))
    r = alpha * x_ref[...] + (1.0 + gt_ref[0]) * y
    o_ref[...] = _layer_norm(r, lg_ref[0], lb_ref[0])


def _outproj(y_mla, y_gdn, y_pool, w_out, x2, mod_l, ln_g, ln_b, l, S, tm, alpha):
    T, D = x2.shape
    nS = S // tm
    vec = pl.BlockSpec((1, 1, D), lambda i: (l, 0, 0))
    return pl.pallas_call(
        functools.partial(_outproj_body, alpha=alpha),
        grid=(T // tm,),
        in_specs=[pl.BlockSpec((tm, MLA_WIDTH), lambda i: (i, 0)),
                  pl.BlockSpec((tm, GDN_WIDTH), lambda i: (i, 0)),
                  pl.BlockSpec((tm, POOL_WIDTH), lambda i: (i, 0)),
                  pl.BlockSpec((1, D, D), lambda i: (l, 0, 0)),
                  pl.BlockSpec((tm, D), lambda i: (i, 0)),
                  pl.BlockSpec((1, 1, D), lambda i: (i // nS, 0, 2)),
                  vec, vec],
        out_specs=pl.BlockSpec((tm, D), lambda i: (i, 0)),
        out_shape=jax.ShapeDtypeStruct((T, D), F32),
        compiler_params=_cparams(("arbitrary",)),
        name="outproj_ln",
    )(y_mla, y_gdn, y_pool, w_out, x2, mod_l, ln_g, ln_b)


def _lane_first(cond, lane_f):
    return jnp.min(jnp.where(cond, lane_f, float(LANE)), axis=-1, keepdims=True)


def _router_body(x_ref, sh_ref, sc_ref, whi_ref, wlo_ref, br_ref, tri_ref,
                 hm_out, cnt_out, carry_ref):
    i = pl.program_id(0)

    @pl.when(i == 0)
    def _():
        carry_ref[...] = jnp.zeros_like(carry_ref)

    h = x_ref[...] * (1.0 + sc_ref[0]) + sh_ref[0]
    hm_out[:, :HALF_D] = _pack_bf16_pairs(h)
    h_hi, h_lo = _split2(h)
    logits = _dot(h_hi, whi_ref[0]) + _dot(h_lo, whi_ref[0]) + _dot(h_hi, wlo_ref[0]) + br_ref[0]
    tm = logits.shape[0]
    lane = lax.broadcasted_iota(jnp.int32, (tm, LANE), 1)
    lane_f = lane.astype(F32)
    neg = -jnp.inf
    gl = jnp.where(lane < N_GROUPS, logits, neg)
    gmax = jnp.max(gl, axis=-1, keepdims=True)
    gsel = _lane_first(gl == gmax, lane_f)
    g_p = 1.0 / jnp.sum(jnp.exp(gl - gmax), axis=-1, keepdims=True)
    lo = N_GROUPS + EXPERTS_PER_GROUP * gsel
    el = jnp.where((lane_f >= lo) & (lane_f < lo + EXPERTS_PER_GROUP), logits, neg)
    m1 = jnp.max(el, axis=-1, keepdims=True)
    i1 = _lane_first(el == m1, lane_f)
    el2 = jnp.where(lane_f == i1, neg, el)
    m2 = jnp.max(el2, axis=-1, keepdims=True)
    i2 = _lane_first(el2 == m2, lane_f)
    t = jnp.exp(m2 - m1)
    w1 = g_p / (1.0 + t)
    w2 = g_p * t / (1.0 + t)
    loc1 = i1 - lo
    loc2 = i2 - lo
    a_loc = jnp.minimum(loc1, loc2)
    b_loc = jnp.maximum(loc1, loc2)
    pair = a_loc * (2 * EXPERTS_PER_GROUP - 1 - a_loc) * 0.5 + (b_loc - a_loc - 1.0)
    bucket = gsel * PAIRS_PER_GROUP + pair
    first_is_a = loc1 < loc2
    w_a = jnp.where(first_is_a, w1, w2)
    w_b = jnp.where(first_is_a, w2, w1)
    hit = lane_f == bucket
    onehot = hit.astype(BF16)
    before = _dot(tri_ref[...], onehot) + carry_ref[0:1, :]
    rank = jnp.sum(jnp.where(hit, before, 0.0), axis=-1, keepdims=True)
    total = carry_ref[0:1, :] + jnp.sum(onehot.astype(F32), axis=0, keepdims=True)
    carry_ref[...] = jnp.broadcast_to(total, carry_ref.shape)
    cnt_out[...] = jnp.broadcast_to(total, cnt_out.shape)
    meta = jnp.zeros((tm, LANE), F32)
    for idx, val in enumerate((bucket, rank, w_a, w_b)):
        meta = jnp.where(lane == idx, val, meta)
    hm_out[:, HALF_D:] = lax.bitcast_convert_type(meta, jnp.uint32)


def _router(x2, mod_l, w_hi, w_lo, b_r, tri, l, S, tm):
    T, D = x2.shape
    nS = S // tm
    return pl.pallas_call(
        _router_body,
        grid=(T // tm,),
        in_specs=[pl.BlockSpec((tm, D), lambda i: (i, 0)),
                  pl.BlockSpec((1, 1, D), lambda i: (i // nS, 0, 3)),
                  pl.BlockSpec((1, 1, D), lambda i: (i // nS, 0, 4)),
                  pl.BlockSpec((1, D, LANE), lambda i: (l, 0, 0)),
                  pl.BlockSpec((1, D, LANE), lambda i: (l, 0, 0)),
                  pl.BlockSpec((1, 1, LANE), lambda i: (l, 0, 0)),
                  pl.BlockSpec((tm, tm), lambda i: (0, 0))],
        out_specs=[pl.BlockSpec((tm, HALF_D + LANE), lambda i: (i, 0)),
                   pl.BlockSpec((8, LANE), lambda i: (0, 0))],
        out_shape=[jax.ShapeDtypeStruct((T, HALF_D + LANE), jnp.uint32),
                   jax.ShapeDtypeStruct((8, LANE), F32)],
        scratch_shapes=[pltpu.VMEM((8, LANE), F32)],
        compiler_params=_cparams(("arbitrary",)),
        name="router",
    )(x2, mod_l, mod_l, w_hi, w_lo, b_r, tri)


def _row_copy(src, s, dst, d, sem):
    return pltpu.make_async_copy(src.at[pl.ds(s, 1)], dst.at[pl.ds(d, 1)], sem)


DMA_WAIT_UNROLL = 32
INTRA_UNROLL = 4
GDN_SEQS = 2
ATTN_UNROLL = 8
COMBINE_ROWS = 128
FFN_BLOCKS = 2


def _issue_rows(n, make_copy):
    def body(g, _):
        base = pl.multiple_of(g * SUBLANES, SUBLANES)
        for u in range(SUBLANES):
            make_copy(base, u).start()
        return 0

    lax.fori_loop(0, n // SUBLANES, body, 0)


def _tile_row(ref, base, u):
    return ref.at[pl.ds(base, SUBLANES)].at[pl.ds(u, 1)]


def _wait_rows(n, make_copy):
    def body(_, c):
        for _u in range(DMA_WAIT_UNROLL):
            make_copy(0).wait()
        return c

    lax.fori_loop(0, n // DMA_WAIT_UNROLL, body, 0)


def _dispatch_body(dest_ref, hm_ref, xs_in_ref, xs_ref, sem, *, tm):
    del xs_in_ref
    copy = lambda base, u: pltpu.make_async_copy(
        _tile_row(hm_ref, base, u), xs_ref.at[pl.ds(dest_ref[base + u], 1)], sem)
    _issue_rows(tm, copy)
    _wait_rows(tm, lambda r: _row_copy(hm_ref, 0, xs_ref, 0, sem))


def _dispatch(hm, dest, xs_buf, tm):
    T, W = hm.shape
    return pl.pallas_call(
        functools.partial(_dispatch_body, tm=tm),
        grid=(T // tm,),
        in_specs=[pl.BlockSpec((tm,), lambda i: (i,), memory_space=pltpu.SMEM),
                  pl.BlockSpec((tm, W), lambda i: (i, 0)),
                  pl.BlockSpec(memory_space=pl.ANY)],
        out_specs=pl.BlockSpec(memory_space=pl.ANY),
        out_shape=jax.ShapeDtypeStruct(xs_buf.shape, xs_buf.dtype),
        input_output_aliases={2: 0},
        scratch_shapes=[pltpu.SemaphoreType.DMA],
        compiler_params=_cparams(("arbitrary",)),
        name="moe_dispatch",
    )(dest, hm, xs_buf)


def _ffn_body(grp_ref, ea_ref, eb_ref, used_ref, xs_ref, wg_ref, wu_ref, wd_ref, o_ref):
    i = pl.program_id(0)
    subs = range(FFN_BLOCKS)
    first = FFN_BLOCKS * i

    @pl.when(used_ref[first] != 0)
    def _():
        a = [ea_ref[first + s] for s in subs]
        b = [eb_ref[first + s] for s in subs]
        xm = [xs_ref[s * MOE_ROWS:(s + 1) * MOE_ROWS, :] for s in subs]
        x = [_unpack_bf16_pairs(xm[s][:, :HALF_D]).astype(BF16) for s in subs]
        g_a = [_dot(x[s], wg_ref[0, 0, a[s]]) for s in subs]
        u_a = [_dot(x[s], wu_ref[0, 0, a[s]]) for s in subs]
        g_b = [_dot(x[s], wg_ref[0, 0, b[s]]) for s in subs]
        u_b = [_dot(x[s], wu_ref[0, 0, b[s]]) for s in subs]
        meta = [lax.bitcast_convert_type(xm[s][:, HALF_D:], F32) for s in subs]
        w_a = [meta[s][:, 2:3] for s in subs]
        w_b = [meta[s][:, 3:4] for s in subs]
        act_a = [(g_a[s] * _sigmoid(g_a[s]) * u_a[s] * w_a[s]).astype(BF16) for s in subs]
        act_b = [(g_b[s] * _sigmoid(g_b[s]) * u_b[s] * w_b[s]).astype(BF16) for s in subs]
        for s in subs:
            y = _dot(act_a[s], wd_ref[0, 0, a[s]]) + _dot(act_b[s], wd_ref[0, 0, b[s]])
            o_ref[s * MOE_ROWS:(s + 1) * MOE_ROWS, :] = _pack_bf16_pairs(y)

    @pl.when(used_ref[first] == 0)
    def _():
        o_ref[...] = jnp.zeros_like(o_ref)


def _ffn(xs, grp, ea, eb, used, w_gate, w_up, w_down, l):
    P, W = xs.shape
    D = D_MODEL
    rows = MOE_ROWS * FFN_BLOCKS
    nb = P // rows
    FF = EXPERT_FF
    E = EXPERTS_PER_GROUP
    w_up_spec = pl.BlockSpec((1, 1, E, D, FF), lambda i, g, ea, eb, u: (l, g[FFN_BLOCKS * i], 0, 0, 0))
    w_dn_spec = pl.BlockSpec((1, 1, E, FF, D), lambda i, g, ea, eb, u: (l, g[FFN_BLOCKS * i], 0, 0, 0))
    grid_spec = pltpu.PrefetchScalarGridSpec(
        num_scalar_prefetch=4,
        grid=(nb,),
        in_specs=[pl.BlockSpec((rows, W), lambda i, g, ea, eb, u: (i, 0)),
                  w_up_spec, w_up_spec, w_dn_spec],
        out_specs=pl.BlockSpec((rows, HALF_D), lambda i, g, ea, eb, u: (i, 0)),
    )
    return pl.pallas_call(
        _ffn_body,
        grid_spec=grid_spec,
        out_shape=jax.ShapeDtypeStruct((P, HALF_D), jnp.uint32),
        compiler_params=_cparams(("arbitrary",)),
        name="moe_ffn",
    )(grp, ea, eb, used, xs, w_gate, w_up, w_down)


def _combine_body(dcur_ref, dnext_ref, x_ref, gt_ref, lg_ref, lb_ref, ys_ref, o_ref,
                  y0_ref, y1_ref, sems, *, tm, n_steps, alpha):
    i = pl.program_id(0)
    bufs = (y0_ref, y1_ref)

    n_chunks = tm // COMBINE_ROWS

    def finish(slot, prefetch):
        _wait_rows(tm, lambda r: _row_copy(ys_ref, 0, bufs[slot], 0, sems.at[slot]))
        gate = 1.0 + gt_ref[0]

        def chunk(c, _):
            r0 = pl.multiple_of(c * COMBINE_ROWS, COMBINE_ROWS)
            if prefetch:
                for t0 in range(0, COMBINE_ROWS, SUBLANES):
                    base = pl.multiple_of(r0 + t0, SUBLANES)
                    for u in range(SUBLANES):
                        pltpu.make_async_copy(ys_ref.at[pl.ds(dnext_ref[base + u], 1)],
                                              _tile_row(bufs[1 - slot], base, u), sems.at[1 - slot]).start()
            rows = pl.ds(r0, COMBINE_ROWS)
            r = alpha * x_ref[rows, :] + gate * _unpack_bf16_pairs(bufs[slot][rows, :])
            o_ref[rows, :] = _layer_norm(r, lg_ref[0], lb_ref[0])
            return 0

        lax.fori_loop(0, n_chunks, chunk, 0)

    @pl.when(i == 0)
    def _():
        _issue_rows(tm, lambda base, u: pltpu.make_async_copy(
            ys_ref.at[pl.ds(dcur_ref[base + u], 1)], _tile_row(bufs[0], base, u), sems.at[0]))

    has_next = i + 1 < n_steps
    for slot in range(2):
        mine = (i % 2) == slot

        @pl.when(mine & has_next)
        def _():
            finish(slot, True)

        @pl.when(mine & jnp.logical_not(has_next))
        def _():
            finish(slot, False)


def _combine(ys, dest, x2, mod_l, ln_g, ln_b, l, S, tm, alpha):
    T, D = x2.shape
    nS = S // tm
    n_steps = T // tm
    vec = pl.BlockSpec((1, 1, D), lambda i: (l, 0, 0))
    return pl.pallas_call(
        functools.partial(_combine_body, tm=tm, n_steps=n_steps, alpha=alpha),
        grid=(n_steps,),
        in_specs=[pl.BlockSpec((tm,), lambda i: (i,), memory_space=pltpu.SMEM),
                  pl.BlockSpec((tm,), lambda i: (jnp.minimum(i + 1, n_steps - 1),), memory_space=pltpu.SMEM),
                  pl.BlockSpec((tm, D), lambda i: (i, 0)),
                  pl.BlockSpec((1, 1, D), lambda i: (i // nS, 0, 5)),
                  vec, vec,
                  pl.BlockSpec(memory_space=pl.ANY)],
        out_specs=pl.BlockSpec((tm, D), lambda i: (i, 0)),
        out_shape=jax.ShapeDtypeStruct((T, D), F32),
        scratch_shapes=[pltpu.VMEM((tm, HALF_D), jnp.uint32), pltpu.VMEM((tm, HALF_D), jnp.uint32),
                        pltpu.SemaphoreType.DMA((2,))],
        compiler_params=_cparams(("arbitrary",)),
        name="moe_combine_ln",
    )(dest, dest, x2, mod_l, ln_g, ln_b, ys)


def _prep_w_in(w_in):
    L, D, _ = w_in.shape
    o = 0
    cq = w_in[..., o:o + MLA_Q_RANK]; o += MLA_Q_RANK
    ckv = w_in[..., o:o + MLA_KV_RANK]; o += MLA_KV_RANK
    kr = w_in[..., o:o + MLA_ROPE]; o += MLA_ROPE
    gq = w_in[..., o:o + GDN_WIDTH]; o += GDN_WIDTH
    gk = w_in[..., o:o + GDN_WIDTH]; o += GDN_WIDTH
    gv = w_in[..., o:o + GDN_WIDTH]; o += GDN_WIDTH
    gz = w_in[..., o:o + GDN_WIDTH]; o += GDN_WIDTH
    ga = w_in[..., o:o + GDN_HEADS]; o += GDN_HEADS
    gb = w_in[..., o:o + GDN_HEADS]; o += GDN_HEADS
    pw = w_in[..., o:o + POOL_WIDTH]
    half = MLA_ROPE // 2
    z = lambda n: jnp.zeros((L, D, n), w_in.dtype)
    kra = jnp.concatenate([z(MLA_NOPE), kr, z(HEAD_PAD - MLA_NOPE - MLA_ROPE)], axis=-1)
    krb = jnp.concatenate([ga, gb, z(MLA_NOPE - 2 * GDN_HEADS), kr[..., half:], kr[..., :half],
                           z(HEAD_PAD - MLA_NOPE - MLA_ROPE)], axis=-1)
    out = jnp.concatenate([gq, gk, gv, gz, cq, pw, ckv, kra, krb], axis=-1)
    assert out.shape[-1] == IN_COLS
    return out.astype(BF16)


def _prep_mla(w_uq, w_ukv):
    L = w_uq.shape[0]
    H, half = MLA_HEADS, MLA_ROPE // 2
    pad = HEAD_PAD - MLA_NOPE - MLA_ROPE
    q = w_uq.reshape(L, MLA_Q_RANK, H, MLA_NOPE + MLA_ROPE)
    nope, r1, r2 = q[..., :MLA_NOPE], q[..., MLA_NOPE:MLA_NOPE + half], q[..., MLA_NOPE + half:]
    zq = lambda n: jnp.zeros((L, MLA_Q_RANK, H, n), w_uq.dtype)
    plain = jnp.concatenate([nope, r1, r2, zq(pad)], axis=-1).reshape(L, MLA_Q_RANK, H * HEAD_PAD)
    partner = jnp.concatenate([zq(MLA_NOPE), r2, r1, zq(pad)], axis=-1).reshape(L, MLA_Q_RANK, H * HEAD_PAD)
    wq2 = jnp.concatenate([plain, partner], axis=-1).astype(BF16)
    kv = w_ukv.reshape(L, MLA_KV_RANK, H, MLA_NOPE + MLA_V)
    k_nope, v = kv[..., :MLA_NOPE], kv[..., MLA_NOPE:]
    zk = lambda n: jnp.zeros((L, MLA_KV_RANK, H, n), w_ukv.dtype)
    k_main = jnp.concatenate([k_nope, zk(HEAD_PAD - MLA_NOPE)], axis=-1).reshape(L, MLA_KV_RANK, H * HEAD_PAD)
    even = (jnp.arange(H) % 2 == 0)[None, None, :, None]
    v_pair = jnp.where(even, jnp.concatenate([v, zk(MLA_V)], axis=-1), jnp.concatenate([zk(MLA_V), v], axis=-1))
    wkv2 = jnp.concatenate([k_main, v_pair.reshape(L, MLA_KV_RANK, H * HEAD_PAD)], axis=-1).astype(BF16)
    return wq2, wkv2


def _rope_tables(S):
    half = MLA_ROPE // 2
    inv_freq = jnp.power(ROPE_THETA, -jnp.arange(0, MLA_ROPE, 2, dtype=F32) / MLA_ROPE)
    ang = jnp.arange(S, dtype=F32)[:, None] * inv_freq[None, :]
    cos, sin = jnp.cos(ang), jnp.sin(ang)
    pad = jnp.zeros((S, HEAD_PAD - MLA_NOPE - MLA_ROPE), F32)
    cos_t = jnp.concatenate([jnp.ones((S, MLA_NOPE), F32), cos, cos, pad], axis=-1)
    sin_t = jnp.concatenate([jnp.zeros((S, MLA_NOPE), F32), -sin, sin, pad], axis=-1)
    scale = (MLA_NOPE + MLA_ROPE) ** -0.5
    return cos_t * scale, sin_t * scale, cos_t, sin_t


def _block_diag(blocks):
    L, G, n, _ = blocks.shape
    eye = jnp.eye(G, dtype=blocks.dtype)
    return jnp.einsum('lgij,gh->lgihj', blocks, eye).reshape(L, G * n, G * n)


def kernel(x, c, w_in, mla_q_norm, mla_kv_norm, mla_w_uq, mla_w_ukv, gdn_conv, gdn_a_log, gdn_dt_bias, gdn_out_norm, pool_w, pool_scale, w_out, w_mod, b_mod, ln1_g, ln1_b, ln2_g, ln2_b, router_w_group, router_b_group, router_w_expert, router_b_expert, moe_w_gate, moe_w_up, moe_w_down):
    B, S, D = x.shape
    L = w_in.shape[0]
    T = B * S
    alpha = (2 * L) ** 0.25
    ts = min(512, S)
    tq = min(256, S)
    t_moe = min(512, S)
    t_disp = min(1024, S)
    assert D == D_MODEL and S % ts == 0 and S % tq == 0 and ts % CHUNK == 0
    assert S % t_disp == 0 and t_moe % DMA_WAIT_UNROLL == 0

    w_in2 = _prep_w_in(w_in)
    wq2, wkv2 = _prep_mla(mla_w_uq, mla_w_ukv)
    tabs = _rope_tables(S)
    gq = mla_q_norm.reshape(L, 1, MLA_Q_RANK)
    gkv = mla_kv_norm.reshape(L, 1, MLA_KV_RANK)
    alog_e = jnp.repeat(gdn_a_log, GDN_DIM, axis=-1).reshape(L, 1, GDN_WIDTH)
    dt_e = jnp.repeat(gdn_dt_bias, GDN_DIM, axis=-1).reshape(L, 1, GDN_WIDTH)
    og_e = jnp.tile(gdn_out_norm, (1, GDN_HEADS)).reshape(L, 1, GDN_WIDTH)
    lane_head = jnp.arange(GDN_WIDTH) // GDN_DIM
    seg = (lane_head[:, None] == lane_head[None, :]).astype(BF16)
    tri_c = (jnp.arange(CHUNK)[:, None] >= jnp.arange(CHUNK)[None, :]).astype(BF16)
    src_lane = jnp.arange(LANE)[:, None]
    out_col = jnp.arange(2 * GDN_WIDTH)[None, :]
    ab_expand = (src_lane == (out_col // GDN_WIDTH) * GDN_HEADS + (out_col % GDN_WIDTH) // GDN_DIM).astype(BF16)
    pool_bd = _block_diag(pool_w).astype(BF16)
    pool_sc = pool_scale.reshape(L, 1, POOL_WIDTH)
    w_out_b = w_out.astype(BF16)
    w_r = jnp.concatenate([router_w_group, router_w_expert,
                           jnp.zeros((L, D, LANE - N_GROUPS - N_EXPERTS), F32)], axis=-1)
    w_r_hi = w_r.astype(BF16)
    w_r_lo = (w_r - w_r_hi.astype(F32)).astype(BF16)
    b_r = jnp.concatenate([router_b_group, router_b_expert,
                           jnp.zeros((L, LANE - N_GROUPS - N_EXPERTS), F32)], axis=-1).reshape(L, 1, LANE)
    tri_r = (jnp.arange(ts)[:, None] > jnp.arange(ts)[None, :]).astype(BF16)
    ln1g, ln1b = ln1_g.reshape(L, 1, D), ln1_b.reshape(L, 1, D)
    ln2g, ln2b = ln2_g.reshape(L, 1, D), ln2_b.reshape(L, 1, D)

    assert N_BUCKETS <= LANE
    step_rows = MOE_ROWS * FFN_BLOCKS
    max_rows = T + N_BUCKETS * (MOE_ROWS - 1) + N_GROUPS * (step_rows - MOE_ROWS)
    P = (max_rows + step_rows - 1) // step_rows * step_rows
    nb = P // MOE_ROWS
    pg, pa, pb = [], [], []
    for g_ in range(N_GROUPS):
        for a_ in range(EXPERTS_PER_GROUP):
            for b_ in range(a_ + 1, EXPERTS_PER_GROUP):
                pg.append(g_)
                pa.append(a_)
                pb.append(b_)
    bucket_g = jnp.asarray(pg, jnp.int32)
    bucket_a = jnp.asarray(pa, jnp.int32)
    bucket_b = jnp.asarray(pb, jnp.int32)
    grouped = lambda w: w.astype(BF16).reshape((L, N_GROUPS, EXPERTS_PER_GROUP) + w.shape[2:])
    wg_b, wu_b, wd_b = grouped(moe_w_gate), grouped(moe_w_up), grouped(moe_w_down)
    bucket_ids = jnp.arange(N_BUCKETS, dtype=jnp.int32)
    blk0 = jnp.arange(nb, dtype=jnp.int32) * MOE_ROWS
    xs = jnp.zeros((P, HALF_D + LANE), jnp.uint32)

    mod = _modulation(c, w_mod, b_mod)
    x2 = x.reshape(T, D)
    for l in range(L):
        mod_l = mod[l].reshape(B, 1, 6 * D)
        proj = _inproj(x2, mod_l, w_in2, l, S, ts)
        q, k, v = _mla_proj(proj, tabs, gq, gkv, wq2, wkv2, l, S, ts)
        y_mla = _attention(q, k, v, B, S, tq)
        y_gdn = _gdn(proj, ab_expand, gdn_conv, alog_e, dt_e, og_e, seg, tri_c, l, B, S, ts)
        y_pool = _pool(proj, pool_bd, pool_sc, l, B, S, ts)
        x2 = _outproj(y_mla, y_gdn, y_pool, w_out_b, x2, mod_l, ln1g, ln1b, l, S, ts, alpha)

        hm, cnt = _router(x2, mod_l, w_r_hi, w_r_lo, b_r, tri_r, l, S, ts)
        counts = cnt[0, :N_BUCKETS].astype(jnp.int32)
        padded = ((counts + MOE_ROWS - 1) // MOE_ROWS * MOE_ROWS).reshape(N_GROUPS, PAIRS_PER_GROUP)
        group_rows = jnp.sum(padded, axis=1)
        group_pad = (step_rows - group_rows % step_rows) % step_rows
        padded = padded.at[:, PAIRS_PER_GROUP - 1].add(group_pad).reshape(N_BUCKETS)
        pad_end = jnp.cumsum(padded)
        pad_start = pad_end - padded
        route = lax.bitcast_convert_type(hm[:, HALF_D:HALF_D + 2], F32).astype(jnp.int32)
        bucket, rank = route[:, 0], route[:, 1]
        dest = jnp.sum(jnp.where(bucket[:, None] == bucket_ids[None, :], pad_start[None, :], 0), axis=1) + rank
        block_bucket = jnp.minimum(jnp.sum((pad_end[None, :] <= blk0[:, None]).astype(jnp.int32), axis=1),
                                   N_BUCKETS - 1)
        onehot_b = (block_bucket[:, None] == bucket_ids[None, :]).astype(jnp.int32)
        grp = jnp.sum(onehot_b * bucket_g[None, :], axis=1)
        ea = jnp.sum(onehot_b * bucket_a[None, :], axis=1)
        eb = jnp.sum(onehot_b * bucket_b[None, :], axis=1)
        used = (blk0 < pad_end[-1]).astype(jnp.int32)

        xs = _dispatch(hm, dest, xs, t_disp)
        ys = _ffn(xs, grp, ea, eb, used, wg_b, wu_b, wd_b, l)
        x2 = _combine(ys, dest, x2, mod_l, ln2g, ln2b, l, S, t_moe, alpha)
    return x2.reshape(B, S, D)
```

```python
import functools
import math

import jax
import jax.numpy as jnp
from jax import lax
from jax.experimental import pallas as pl
from jax.experimental.pallas import tpu as pltpu

F32 = jnp.float32
BF16 = jnp.bfloat16

D_MODEL = 1024
HALF_D = D_MODEL // 2
CHUNK = 64
MLA_HEADS = 8
MLA_NOPE = 64
MLA_ROPE = 32
MLA_V = 64
MLA_Q_RANK = 256
MLA_KV_RANK = 128
MLA_WIDTH = MLA_HEADS * MLA_V
ROPE_THETA = 10000.0
GDN_HEADS = 4
GDN_DIM = 64
GDN_WIDTH = GDN_HEADS * GDN_DIM
GDN_CONV = 4
POOL_WIDTH = 256
POOL_WINDOWS = (2, 4, 8, 16)
POOL_GROUP_DIM = 64
N_GROUPS = 4
EXPERTS_PER_GROUP = 8
N_EXPERTS = N_GROUPS * EXPERTS_PER_GROUP
TOP_K = 2
EXPERT_FF = 256
LN_EPS = 1e-5
RMS_EPS = 1e-6

LANE = 128
SUBLANES = 8
HEAD_PAD = 128
PAIRS_PER_GROUP = EXPERTS_PER_GROUP * (EXPERTS_PER_GROUP - 1) // 2
N_BUCKETS = N_GROUPS * PAIRS_PER_GROUP
MOE_ROWS = 128
VMEM_LIMIT = 48 * 1024 * 1024

_C_GQ, _C_GK, _C_GV, _C_GZ = 0, 256, 512, 768
_C_CQ, _C_POOL = 1024, 1280
_C_CKV, _C_KRA, _C_KRB = 1536, 1664, 1792
IN_COLS = 1920


def _cparams(sem):
    return pltpu.CompilerParams(dimension_semantics=sem, vmem_limit_bytes=VMEM_LIMIT)


def _sigmoid(x):
    return 1.0 / (1.0 + jnp.exp(-x))


def _split2(x):
    hi = x.astype(BF16)
    lo = (x - hi.astype(F32)).astype(BF16)
    return hi, lo


def _split3(x):
    hi = x.astype(BF16)
    r = x - hi.astype(F32)
    mid = r.astype(BF16)
    lo = (r - mid.astype(F32)).astype(BF16)
    return hi, mid, lo


def _pack_bf16_pairs(x):
    n = x.shape[1] // 2
    lo = lax.bitcast_convert_type(x[:, :n].astype(BF16).astype(F32), jnp.uint32)
    hi = lax.bitcast_convert_type(x[:, n:].astype(BF16).astype(F32), jnp.uint32)
    return (hi & jnp.uint32(0xFFFF0000)) | (lo >> 16)


def _unpack_bf16_pairs(p):
    lo = lax.bitcast_convert_type(p << 16, F32)
    hi = lax.bitcast_convert_type(p & jnp.uint32(0xFFFF0000), F32)
    return jnp.concatenate([lo, hi], axis=1)


def _dot(a, b):
    return jnp.dot(a, b, preferred_element_type=F32)


def _dot_nt(a, b):
    return lax.dot_general(a, b, (((1,), (1,)), ((), ())), preferred_element_type=F32)


def _dot_tn(a, b):
    return lax.dot_general(a, b, (((0,), (0,)), ((), ())), preferred_element_type=F32)


def _mod_body(c_ref, w_ref, b_ref, o_ref):
    c = c_ref[...]
    ca = c * _sigmoid(c)
    o_ref[0] = _dot(ca.astype(BF16), w_ref[0].astype(BF16)) + b_ref[0]


def _modulation(c, w_mod, b_mod):
    L, D, N = w_mod.shape
    B = c.shape[0]
    tn = 1024
    return pl.pallas_call(
        _mod_body,
        grid=(L, N // tn),
        in_specs=[pl.BlockSpec((B, D), lambda l, j: (0, 0)),
                  pl.BlockSpec((1, D, tn), lambda l, j: (l, 0, j)),
                  pl.BlockSpec((1, 1, tn), lambda l, j: (l, 0, j))],
        out_specs=pl.BlockSpec((1, B, tn), lambda l, j: (l, 0, j)),
        out_shape=jax.ShapeDtypeStruct((L, B, N), F32),
        compiler_params=_cparams(("arbitrary", "arbitrary")),
        name="modulation",
    )(c, w_mod, b_mod.reshape(L, 1, N))


def _inproj_body(x_ref, sh_ref, sc_ref, w_ref, o_ref):
    h = x_ref[...] * (1.0 + sc_ref[0]) + sh_ref[0]
    o_ref[...] = _dot(h.astype(BF16), w_ref[0])


def _inproj(x2, mod_l, w_in2, l, S, tm):
    T, D = x2.shape
    nS = S // tm
    return pl.pallas_call(
        _inproj_body,
        grid=(T // tm,),
        in_specs=[pl.BlockSpec((tm, D), lambda i: (i, 0)),
                  pl.BlockSpec((1, 1, D), lambda i: (i // nS, 0, 0)),
                  pl.BlockSpec((1, 1, D), lambda i: (i // nS, 0, 1)),
                  pl.BlockSpec((1, D, IN_COLS), lambda i: (l, 0, 0))],
        out_specs=pl.BlockSpec((tm, IN_COLS), lambda i: (i, 0)),
        out_shape=jax.ShapeDtypeStruct((T, IN_COLS), F32),
        compiler_params=_cparams(("arbitrary",)),
        name="inproj",
    )(x2, mod_l, mod_l, w_in2)


def _mla_proj_body(cq_ref, ckv_ref, kra_ref, krb_ref, cosq_ref, sinq_ref, cosk_ref, sink_ref,
                   gq_ref, gkv_ref, wq_ref, wkv_ref, q_out, k_out, v_out):
    cq = cq_ref[...]
    qn = cq * lax.rsqrt(jnp.mean(cq * cq, axis=-1, keepdims=True) + RMS_EPS) * gq_ref[0]
    q2 = _dot(qn.astype(BF16), wq_ref[0])
    ckv = ckv_ref[...]
    kvn = ckv * lax.rsqrt(jnp.mean(ckv * ckv, axis=-1, keepdims=True) + RMS_EPS) * gkv_ref[0]
    kv2 = _dot(kvn.astype(BF16), wkv_ref[0])
    cq_t, sq_t = cosq_ref[...], sinq_ref[...]
    krope = kra_ref[...] * cosk_ref[...] + krb_ref[...] * sink_ref[...]
    hw = MLA_HEADS * HEAD_PAD
    for h in range(MLA_HEADS):
        a, b = h * HEAD_PAD, (h + 1) * HEAD_PAD
        q_out[:, a:b] = (q2[:, a:b] * cq_t + q2[:, hw + a:hw + b] * sq_t).astype(BF16)
        k_out[:, a:b] = (kv2[:, a:b] + krope).astype(BF16)
    vl = lax.broadcasted_iota(jnp.int32, (1, hw), 1)
    ones_lane = jnp.where((vl // HEAD_PAD) % 2 == 0, MLA_V, 0)
    v_out[...] = (kv2[:, hw:] + (vl % HEAD_PAD == ones_lane).astype(F32)).astype(BF16)


def _mla_proj(proj, tabs, gq, gkv, wq2, wkv2, l, S, ts):
    T = proj.shape[0]
    nS = S // ts
    hw = MLA_HEADS * HEAD_PAD
    tab_spec = pl.BlockSpec((ts, LANE), lambda i: (i % nS, 0))
    return pl.pallas_call(
        _mla_proj_body,
        grid=(T // ts,),
        in_specs=[pl.BlockSpec((ts, 256), lambda i: (i, _C_CQ // 256)),
                  pl.BlockSpec((ts, 128), lambda i: (i, _C_CKV // 128)),
                  pl.BlockSpec((ts, 128), lambda i: (i, _C_KRA // 128)),
                  pl.BlockSpec((ts, 128), lambda i: (i, _C_KRB // 128)),
                  tab_spec, tab_spec, tab_spec, tab_spec,
                  pl.BlockSpec((1, 1, MLA_Q_RANK), lambda i: (l, 0, 0)),
                  pl.BlockSpec((1, 1, MLA_KV_RANK), lambda i: (l, 0, 0)),
                  pl.BlockSpec((1, MLA_Q_RANK, 2 * hw), lambda i: (l, 0, 0)),
                  pl.BlockSpec((1, MLA_KV_RANK, 2 * hw), lambda i: (l, 0, 0))],
        out_specs=[pl.BlockSpec((ts, hw), lambda i: (i, 0))] * 3,
        out_shape=[jax.ShapeDtypeStruct((T, hw), BF16)] * 3,
        compiler_params=_cparams(("arbitrary",)),
        name="mla_proj",
    )(proj, proj, proj, proj, *tabs, gq, gkv, wq2, wkv2)


def _attn_body(qi_tab, kj_tab, q_ref, k_ref, v_ref, o_ref, m_s, acc_s, *, tq, n_q, n_tiles):
    row_c = lax.broadcasted_iota(jnp.int32, (tq, tq), 0) // CHUNK
    col_c = lax.broadcasted_iota(jnp.int32, (tq, tq), 1) // CHUNK
    allowed = col_c <= row_c
    lane = lax.broadcasted_iota(jnp.int32, (tq, HEAD_PAD), 1)
    sl = [slice(hh * HEAD_PAD, (hh + 1) * HEAD_PAD) for hh in range(2)]
    hs = range(2)

    def rows_of(tile):
        return pl.ds(pl.multiple_of(tile * tq, tq), tq)

    def stage_a(t):
        q = q_ref[rows_of(qi_tab[t]), :]
        kk = k_ref[rows_of(kj_tab[t]), :]
        return tuple(_dot_nt(q[:, sl[hh]], kk[:, sl[hh]]).astype(BF16) for hh in hs)

    def stage_b(t, s_pair):
        qi, kj = qi_tab[t], kj_tab[t]
        rows = rows_of(qi)
        keep = jnp.logical_or(allowed, kj < qi)
        neg = jnp.asarray(-jnp.inf, BF16)
        s = [jnp.where(keep, s_pair[hh], neg) for hh in hs]
        m_prev = [m_s[hh, rows, :] for hh in hs]
        m_new = [jnp.maximum(m_prev[hh], jnp.max(s[hh], axis=-1, keepdims=True)) for hh in hs]
        p = tuple(jnp.exp(s[hh] - jnp.concatenate([m_new[hh]] * (tq // HEAD_PAD), axis=1)) for hh in hs)
        alpha = tuple(jnp.exp(m_prev[hh] - m_new[hh]) for hh in hs)
        for hh in hs:
            m_s[hh, rows, :] = m_new[hh]
        return p, alpha

    def stage_c(t, p, alpha):
        qi, kj = qi_tab[t], kj_tab[t]
        rows = rows_of(qi)
        vv = v_ref[rows_of(kj), :]
        pv = [_dot(p[hh], vv[:, sl[hh]]) for hh in hs]
        for hh in hs:
            acc_s[hh, rows, :] = alpha[hh].astype(F32) * acc_s[hh, rows, :] + pv[hh]

    def body(t, c):
        s_next, (p, alpha) = c
        s_after = stage_a(t + 2)
        stage_c(t, p, alpha)
        return s_after, stage_b(t + 1, s_next)

    m_s[...] = jnp.full(m_s.shape, -jnp.inf, m_s.dtype)
    acc_s[...] = jnp.zeros_like(acc_s)
    pb0 = stage_b(0, stage_a(0))
    s_last, pb = lax.fori_loop(0, n_tiles - 2, body, (stage_a(1), pb0), unroll=ATTN_UNROLL)
    pb_last = stage_b(n_tiles - 1, s_last)
    stage_c(n_tiles - 2, *pb)
    stage_c(n_tiles - 1, *pb_last)

    def normalize(i, _):
        rows = rows_of(i)
        acc0, acc1 = acc_s[0, rows, :], acc_s[1, rows, :]
        o0 = jnp.where(lane < MLA_V, acc0 * (1.0 / acc0[:, MLA_V:MLA_V + 1]), 0.0)
        o1 = jnp.where(lane >= MLA_V, acc1 * (1.0 / acc1[:, 0:1]), 0.0)
        o_ref[rows, :] = (o0 + o1).astype(BF16)
        return 0

    lax.fori_loop(0, n_q, normalize, 0, unroll=2)


def _attention(q, k, v, B, S, tq):
    T = q.shape[0]
    nq = S // tq
    pairs = MLA_HEADS // 2
    tiles = [(i, j) for i in range(nq) for j in range(i + 1)]
    assert len(tiles) >= 2
    qi_tab = jnp.asarray([t[0] for t in tiles], jnp.int32)
    kj_tab = jnp.asarray([t[1] for t in tiles], jnp.int32)
    seq = pl.BlockSpec((S, 2 * HEAD_PAD), lambda b, p, qt, kt: (b, p))
    grid_spec = pltpu.PrefetchScalarGridSpec(
        num_scalar_prefetch=2,
        grid=(B, pairs),
        in_specs=[seq, seq, seq],
        out_specs=pl.BlockSpec((S, 2 * MLA_V), lambda b, p, qt, kt: (b, p)),
        scratch_shapes=[pltpu.VMEM((2, S, HEAD_PAD), BF16), pltpu.VMEM((2, S, HEAD_PAD), F32)],
    )
    return pl.pallas_call(
        functools.partial(_attn_body, tq=tq, n_q=nq, n_tiles=len(tiles)),
        grid_spec=grid_spec,
        out_shape=jax.ShapeDtypeStruct((T, MLA_WIDTH), BF16),
        compiler_params=_cparams(("arbitrary", "arbitrary")),
        name="mla_attention",
    )(qi_tab, kj_tab, q, k, v)


def _pool_body(p_ref, w_ref, sc_ref, o_ref, prev_ref, *, ts):
    j = pl.program_id(1)
    hist = 16

    @pl.when(j == 0)
    def _():
        prev_ref[...] = jnp.zeros_like(prev_ref)

    cur = p_ref[...]
    x = jnp.concatenate([prev_ref[...], cur], axis=0)
    s1 = x + pltpu.roll(x, 1, axis=0)
    s2 = s1 + pltpu.roll(s1, 2, axis=0)
    s4 = s2 + pltpu.roll(s2, 4, axis=0)
    s8 = s4 + pltpu.roll(s4, 8, axis=0)
    pos = (j * ts + lax.broadcasted_iota(jnp.int32, (ts, POOL_WIDTH), 0) + 1).astype(F32)
    lane = lax.broadcasted_iota(jnp.int32, (ts, POOL_WIDTH), 1)
    sums = (s1, s2, s4, s8)
    mean = None
    for gi, win in enumerate(POOL_WINDOWS):
        m_g = sums[gi][hist:] / jnp.minimum(pos, float(win))
        mean = m_g if mean is None else jnp.where(lane >= gi * POOL_GROUP_DIM, m_g, mean)
    delta = mean - cur
    o_ref[...] = (_dot(delta.astype(BF16), w_ref[0]) * sc_ref[0]).astype(BF16)
    prev_ref[...] = cur[ts - hist:]


def _pool(proj, w_bd, scale, l, B, S, ts):
    T = proj.shape[0]
    nS = S // ts
    return pl.pallas_call(
        functools.partial(_pool_body, ts=ts),
        grid=(B, nS),
        in_specs=[pl.BlockSpec((ts, POOL_WIDTH), lambda b, j: (b * nS + j, _C_POOL // 256)),
                  pl.BlockSpec((1, POOL_WIDTH, POOL_WIDTH), lambda b, j: (l, 0, 0)),
                  pl.BlockSpec((1, 1, POOL_WIDTH), lambda b, j: (l, 0, 0))],
        out_specs=pl.BlockSpec((ts, POOL_WIDTH), lambda b, j: (b * nS + j, 0)),
        out_shape=jax.ShapeDtypeStruct((T, POOL_WIDTH), BF16),
        scratch_shapes=[pltpu.VMEM((16, POOL_WIDTH), F32)],
        compiler_params=_cparams(("arbitrary", "arbitrary")),
        name="pool_mixer",
    )(proj, w_bd, scale)


def _head_sum(x, seg_ref):
    return _dot(x.astype(BF16), seg_ref[...])


def _gdn_inputs(s, raw_q, raw_k, raw_v, raw_ab, expand_ref, conv_ref, alog_ref, dt_ref, seg_ref,
                q_ref, k_ref, v_ref, g_ref, b_ref, prev_ref, ts):
    hist = SUBLANES
    cw = conv_ref[0]
    outs = []
    for idx, ref in enumerate((raw_q, raw_k, raw_v)):
        cur = ref[s]
        a, b = idx * GDN_WIDTH, (idx + 1) * GDN_WIDTH
        x = jnp.concatenate([prev_ref[s, :, a:b], cur], axis=0)
        w = cw[:, a:b]
        y = (w[3:4] * x + w[2:3] * pltpu.roll(x, 1, axis=0) + w[1:2] * pltpu.roll(x, 2, axis=0)
             + w[0:1] * pltpu.roll(x, 3, axis=0))[hist:]
        outs.append(y * _sigmoid(y))
        prev_ref[s, :, a:b] = cur[ts - hist:]
    qc, kc, vc = outs
    q_ref[s] = qc * lax.rsqrt(_head_sum(qc * qc, seg_ref) + RMS_EPS) * (GDN_DIM ** -0.5)
    k_ref[s] = kc * lax.rsqrt(_head_sum(kc * kc, seg_ref) + RMS_EPS)
    v_ref[s] = vc
    ab_hi, ab_mid, ab_lo = _split3(raw_ab[s])
    e = expand_ref[...]
    ab = _dot(ab_hi, e) + _dot(ab_mid, e) + _dot(ab_lo, e)
    z = ab[:, :GDN_WIDTH] + dt_ref[0]
    softplus = jnp.maximum(z, 0.0) + jnp.log(1.0 + jnp.exp(-jnp.abs(z)))
    g_ref[s] = -jnp.exp(alog_ref[0]) * softplus
    b_ref[s] = _sigmoid(ab[:, GDN_WIDTH:])


def _gdn_body(raw_q, raw_k, raw_v, raw_ab, z_ref, expand_ref, conv_ref, alog_ref, dt_ref, og_ref, seg_ref, tri_ref,
              o_ref, state_ref, prev_ref, q_ref, k_ref, v_ref, g_ref, b_ref,
              u_s, w_s, qk_s, qd_s, kd_s, gl_s, *, ts):
    j = pl.program_id(1)
    C, W, H = CHUNK, GDN_WIDTH, GDN_HEADS
    n_chunks = ts // C
    groups = n_chunks // INTRA_UNROLL

    @pl.when(j == 0)
    def _():
        state_ref[...] = jnp.zeros_like(state_ref)
        prev_ref[...] = jnp.zeros_like(prev_ref)

    for s in range(GDN_SEQS):
        _gdn_inputs(s, raw_q, raw_k, raw_v, raw_ab, expand_ref, conv_ref, alog_ref, dt_ref, seg_ref,
                    q_ref, k_ref, v_ref, g_ref, b_ref, prev_ref, ts)

    lane = lax.broadcasted_iota(jnp.int32, (C, W), 1)
    row = lax.broadcasted_iota(jnp.int32, (C, W), 0)
    col_tok = lane % C
    incl = row >= col_tok
    strict = row > col_tok
    eye = (row == col_tok).astype(F32)
    head_masks = [(lane // C == h).astype(F32) for h in range(H)]
    bd_mask = (lax.broadcasted_iota(jnp.int32, (W, W), 0) // C
               == lax.broadcasted_iota(jnp.int32, (W, W), 1) // C)

    def expand(m):
        return jnp.concatenate([m * hm for hm in head_masks], axis=0)

    def bdot(lhs, rhs):
        return _dot(lhs.astype(BF16), rhs.astype(BF16))

    def intra(i, _):
        sq = i // groups
        cs = [(i % groups) * INTRA_UNROLL + uu for uu in range(INTRA_UNROLL)]
        r0s = [pl.multiple_of(c * C, C) for c in cs]
        n = range(INTRA_UNROLL)
        tri = tri_ref[...]
        gs = [_split3(g_ref[sq, pl.ds(r0, C), :]) for r0 in r0s]
        gc = [_dot(tri, g[0]) + _dot(tri, g[1]) + _dot(tri, g[2]) for g in gs]
        ks = [k_ref[sq, pl.ds(r0, C), :] for r0 in r0s]
        qs = [q_ref[sq, pl.ds(r0, C), :] for r0 in r0s]
        betas = [b_ref[sq, pl.ds(r0, C), :] for r0 in r0s]
        kb = [ks[x] * betas[x] for x in n]
        aq = [_dot_nt(jnp.concatenate([kb[x], qs[x]], axis=0).astype(BF16), expand(ks[x]).astype(BF16)) for x in n]
        g_row = [jnp.sum(gc[x] * eye, axis=0, keepdims=True) for x in n]
        g_last = [gc[x][C - 1:C, :] for x in n]
        decay = [jnp.where(incl, jnp.exp(jnp.where(incl, gc[x] - g_row[x], 0.0)), 0.0) for x in n]
        e_gc = [jnp.exp(gc[x]) for x in n]
        a_cat = [jnp.where(strict, aq[x][:C] * decay[x], 0.0) for x in n]
        for x in n:
            r0 = r0s[x]
            qk_s[sq, pl.ds(r0, C), :] = jnp.where(incl, aq[x][C:] * decay[x], 0.0).astype(BF16)
            qd_s[sq, pl.ds(r0, C), :] = (qs[x] * e_gc[x]).astype(BF16)
            kd_s[sq, pl.ds(r0, C), :] = (ks[x] * jnp.exp(g_last[x] - gc[x])).astype(BF16)
            gl_s[sq, pl.ds(pl.multiple_of(cs[x] * 8, 8), 8), :] = jnp.broadcast_to(jnp.exp(g_last[x]), (8, W))
        x_cat = [eye - a_cat[x] for x in n]
        p_cat = [bdot(a_cat[x], expand(a_cat[x])) for x in n]
        n_fac = int(math.log2(C)) - 1
        for r in range(n_fac):
            last = r == n_fac - 1
            xp = [bdot(x_cat[x] if last else jnp.concatenate([x_cat[x], p_cat[x]], axis=0), expand(p_cat[x]))
                  for x in n]
            x_cat = [x_cat[x] + xp[x][:C] for x in n]
            if not last:
                p_cat = [xp[x][C:] for x in n]
        for x in n:
            r0 = r0s[x]
            t_cat = x_cat[x].astype(BF16)
            v = v_ref[sq, pl.ds(r0, C), :]
            u_s[sq, pl.ds(r0, C), :] = _dot(t_cat, expand(v * betas[x]).astype(BF16))
            w_s[sq, pl.ds(r0, C), :] = _dot(t_cat, expand(kb[x] * e_gc[x]).astype(BF16)).astype(BF16)
        return 0

    lax.fori_loop(0, GDN_SEQS * groups, intra, 0)

    def scan(c, _):
        r0 = pl.multiple_of(c * C, C)
        sq = range(GDN_SEQS)
        state = [state_ref[s] for s in sq]
        wq = [_dot(jnp.concatenate([w_s[s, pl.ds(r0, C), :], qd_s[s, pl.ds(r0, C), :]], axis=0),
                   state[s].astype(BF16)) for s in sq]
        v_new = [u_s[s, pl.ds(r0, C), :] - wq[s][:C] for s in sq]
        upd = [_dot_tn(kd_s[s, pl.ds(r0, C), :], v_new[s].astype(BF16)) for s in sq]
        for s in sq:
            g_l = gl_s[s, pl.ds(pl.multiple_of(c * 8, 8), 1), :]
            state_ref[s] = state[s] * g_l + jnp.where(bd_mask, upd[s], 0.0)
        o = [wq[s][C:] + _dot(qk_s[s, pl.ds(r0, C), :], expand(v_new[s]).astype(BF16)) for s in sq]
        ms = [_head_sum(o[s] * o[s], seg_ref) * (1.0 / GDN_DIM) for s in sq]
        for s in sq:
            z = z_ref[s, pl.ds(r0, C), :]
            y = o[s] * lax.rsqrt(ms[s] + RMS_EPS) * og_ref[0] * (z * _sigmoid(z))
            o_ref[s, pl.ds(r0, C), :] = y.astype(BF16)
        return 0

    lax.fori_loop(0, n_chunks, scan, 0, unroll=2)


def _gdn(proj, expand, conv_w, alog_e, dt_e, og, seg, tri, l, B, S, ts):
    T = proj.shape[0]
    nS = S // ts
    W = GDN_WIDTH
    Q = GDN_SEQS
    proj3 = proj.reshape(B, S, proj.shape[-1])

    def col(c):
        return pl.BlockSpec((Q, ts, W), lambda b, j: (b, j, c // W))

    vec = pl.BlockSpec((1, 1, W), lambda b, j: (l, 0, 0))
    seq_f32 = pltpu.VMEM((Q, ts, W), F32)
    out = pl.pallas_call(
        functools.partial(_gdn_body, ts=ts),
        grid=(B // Q, nS),
        in_specs=[col(_C_GQ), col(_C_GK), col(_C_GV),
                  pl.BlockSpec((Q, ts, LANE), lambda b, j: (b, j, _C_KRB // LANE)),
                  col(_C_GZ),
                  pl.BlockSpec((LANE, 2 * W), lambda b, j: (0, 0)),
                  pl.BlockSpec((1, GDN_CONV, 3 * W), lambda b, j: (l, 0, 0)),
                  vec, vec, vec,
                  pl.BlockSpec((W, W), lambda b, j: (0, 0)),
                  pl.BlockSpec((CHUNK, CHUNK), lambda b, j: (0, 0))],
        out_specs=pl.BlockSpec((Q, ts, W), lambda b, j: (b, j, 0)),
        out_shape=jax.ShapeDtypeStruct((B, S, W), BF16),
        scratch_shapes=[pltpu.VMEM((Q, W, W), F32), pltpu.VMEM((Q, SUBLANES, 3 * W), F32)]
        + [seq_f32] * 5 + [seq_f32] + [pltpu.VMEM((Q, ts, W), BF16)] * 4
        + [pltpu.VMEM((Q, 8 * (ts // CHUNK), W), F32)],
        compiler_params=_cparams(("arbitrary", "arbitrary")),
        name="gdn_delta_rule",
    )(proj3, proj3, proj3, proj3, proj3, expand, conv_w, alog_e, dt_e, og, seg, tri)
    return out.reshape(T, W)


def _layer_norm(r, g, b):
    mu = jnp.mean(r, axis=-1, keepdims=True)
    d = r - mu
    var = jnp.mean(d * d, axis=-1, keepdims=True)
    return d * lax.rsqrt(var + LN_EPS) * g + b


def _outproj_body(ym_ref, yg_ref, yp_ref, w_ref, x_ref, gt_ref, lg_ref, lb_ref, o_ref, *, alpha):
    w = w_ref[0]
    y = (_dot(ym_ref[...], w[:MLA_WIDTH]) + _dot(yg_ref[...], w[MLA_WIDTH:MLA_WIDTH + GDN_WIDTH])
         + _dot(yp_ref[...], w[MLA_WIDTH + GDN_WIDTH:]))
    r = alpha * x_ref[...] + (1.0 + gt_ref[0]) * y
    o_ref[...] = _layer_norm(r, lg_ref[0], lb_ref[0])


def _outproj(y_mla, y_gdn, y_pool, w_out, x2, mod_l, ln_g, ln_b, l, S, tm, alpha):
    T, D = x2.shape
    nS = S // tm
    vec = pl.BlockSpec((1, 1, D), lambda i: (l, 0, 0))
    return pl.pallas_call(
        functools.partial(_outproj_body, alpha=alpha),
        grid=(T // tm,),
        in_specs=[pl.BlockSpec((tm, MLA_WIDTH), lambda i: (i, 0)),
                  pl.BlockSpec((tm, GDN_WIDTH), lambda i: (i, 0)),
                  pl.BlockSpec((tm, POOL_WIDTH), lambda i: (i, 0)),
                  pl.BlockSpec((1, D, D), lambda i: (l, 0, 0)),
                  pl.BlockSpec((tm, D), lambda i: (i, 0)),
                  pl.BlockSpec((1, 1, D), lambda i: (i // nS, 0, 2)),
                  vec, vec],
        out_specs=pl.BlockSpec((tm, D), lambda i: (i, 0)),
        out_shape=jax.ShapeDtypeStruct((T, D), F32),
        compiler_params=_cparams(("arbitrary",)),
        name="outproj_ln",
    )(y_mla, y_gdn, y_pool, w_out, x2, mod_l, ln_g, ln_b)


def _lane_first(cond, lane_f):
    return jnp.min(jnp.where(cond, lane_f, float(LANE)), axis=-1, keepdims=True)


def _router_body(x_ref, sh_ref, sc_ref, whi_ref, wlo_ref, br_ref, tri_ref,
                 hm_out, cnt_out, carry_ref):
    i = pl.program_id(0)

    @pl.when(i == 0)
    def _():
        carry_ref[...] = jnp.zeros_like(carry_ref)

    h = x_ref[...] * (1.0 + sc_ref[0]) + sh_ref[0]
    hm_out[:, :HALF_D] = _pack_bf16_pairs(h)
    h_hi, h_lo = _split2(h)
    logits = _dot(h_hi, whi_ref[0]) + _dot(h_lo, whi_ref[0]) + _dot(h_hi, wlo_ref[0]) + br_ref[0]
    tm = logits.shape[0]
    lane = lax.broadcasted_iota(jnp.int32, (tm, LANE), 1)
    lane_f = lane.astype(F32)
    neg = -jnp.inf
    gl = jnp.where(lane < N_GROUPS, logits, neg)
    gmax = jnp.max(gl, axis=-1, keepdims=True)
    gsel = _lane_first(gl == gmax, lane_f)
    g_p = 1.0 / jnp.sum(jnp.exp(gl - gmax), axis=-1, keepdims=True)
    lo = N_GROUPS + EXPERTS_PER_GROUP * gsel
    el = jnp.where((lane_f >= lo) & (lane_f < lo + EXPERTS_PER_GROUP), logits, neg)
    m1 = jnp.max(el, axis=-1, keepdims=True)
    i1 = _lane_first(el == m1, lane_f)
    el2 = jnp.where(lane_f == i1, neg, el)
    m2 = jnp.max(el2, axis=-1, keepdims=True)
    i2 = _lane_first(el2 == m2, lane_f)
    t = jnp.exp(m2 - m1)
    w1 = g_p / (1.0 + t)
    w2 = g_p * t / (1.0 + t)
    loc1 = i1 - lo
    loc2 = i2 - lo
    a_loc = jnp.minimum(loc1, loc2)
    b_loc = jnp.maximum(loc1, loc2)
    pair = a_loc * (2 * EXPERTS_PER_GROUP - 1 - a_loc) * 0.5 + (b_loc - a_loc - 1.0)
    bucket = gsel * PAIRS_PER_GROUP + pair
    first_is_a = loc1 < loc2
    w_a = jnp.where(first_is_a, w1, w2)
    w_b = jnp.where(first_is_a, w2, w1)
    hit = lane_f == bucket
    onehot = hit.astype(BF16)
    before = _dot(tri_ref[...], onehot) + carry_ref[0:1, :]
    rank = jnp.sum(jnp.where(hit, before, 0.0), axis=-1, keepdims=True)
    total = carry_ref[0:1, :] + jnp.sum(onehot.astype(F32), axis=0, keepdims=True)
    carry_ref[...] = jnp.broadcast_to(total, carry_ref.shape)
    cnt_out[...] = jnp.broadcast_to(total, cnt_out.shape)
    meta = jnp.zeros((tm, LANE), F32)
    for idx, val in enumerate((bucket, rank, w_a, w_b)):
        meta = jnp.where(lane == idx, val, meta)
    hm_out[:, HALF_D:] = lax.bitcast_convert_type(meta, jnp.uint32)


def _router(x2, mod_l, w_hi, w_lo, b_r, tri, l, S, tm):
    T, D = x2.shape
    nS = S // tm
    return pl.pallas_call(
        _router_body,
        grid=(T // tm,),
        in_specs=[pl.BlockSpec((tm, D), lambda i: (i, 0)),
                  pl.BlockSpec((1, 1, D), lambda i: (i // nS, 0, 3)),
                  pl.BlockSpec((1, 1, D), lambda i: (i // nS, 0, 4)),
                  pl.BlockSpec((1, D, LANE), lambda i: (l, 0, 0)),
                  pl.BlockSpec((1, D, LANE), lambda i: (l, 0, 0)),
                  pl.BlockSpec((1, 1, LANE), lambda i: (l, 0, 0)),
                  pl.BlockSpec((tm, tm), lambda i: (0, 0))],
        out_specs=[pl.BlockSpec((tm, HALF_D + LANE), lambda i: (i, 0)),
                   pl.BlockSpec((8, LANE), lambda i: (0, 0))],
        out_shape=[jax.ShapeDtypeStruct((T, HALF_D + LANE), jnp.uint32),
                   jax.ShapeDtypeStruct((8, LANE), F32)],
        scratch_shapes=[pltpu.VMEM((8, LANE), F32)],
        compiler_params=_cparams(("arbitrary",)),
        name="router",
    )(x2, mod_l, mod_l, w_hi, w_lo, b_r, tri)


def _row_copy(src, s, dst, d, sem):
    return pltpu.make_async_copy(src.at[pl.ds(s, 1)], dst.at[pl.ds(d, 1)], sem)


DMA_WAIT_UNROLL = 32
INTRA_UNROLL = 4
GDN_SEQS = 4
ATTN_UNROLL = 8
COMBINE_ROWS = 128
FFN_BLOCKS = 2


def _issue_rows(n, make_copy):
    def body(g, _):
        base = pl.multiple_of(g * SUBLANES, SUBLANES)
        for u in range(SUBLANES):
            make_copy(base, u).start()
        return 0

    lax.fori_loop(0, n // SUBLANES, body, 0)


def _tile_row(ref, base, u):
    return ref.at[pl.ds(base, SUBLANES)].at[pl.ds(u, 1)]


def _wait_rows(n, make_copy):
    def body(_, c):
        for _u in range(DMA_WAIT_UNROLL):
            make_copy(0).wait()
        return c

    lax.fori_loop(0, n // DMA_WAIT_UNROLL, body, 0)


def _dispatch_body(dest_ref, hm_ref, xs_in_ref, xs_ref, sem, *, tm):
    del xs_in_ref
    copy = lambda base, u: pltpu.make_async_copy(
        _tile_row(hm_ref, base, u), xs_ref.at[pl.ds(dest_ref[base + u], 1)], sem)
    _issue_rows(tm, copy)
    _wait_rows(tm, lambda r: _row_copy(hm_ref, 0, xs_ref, 0, sem))


def _dispatch(hm, dest, xs_buf, tm):
    T, W = hm.shape
    return pl.pallas_call(
        functools.partial(_dispatch_body, tm=tm),
        grid=(T // tm,),
        in_specs=[pl.BlockSpec((tm,), lambda i: (i,), memory_space=pltpu.SMEM),
                  pl.BlockSpec((tm, W), lambda i: (i, 0)),
                  pl.BlockSpec(memory_space=pl.ANY)],
        out_specs=pl.BlockSpec(memory_space=pl.ANY),
        out_shape=jax.ShapeDtypeStruct(xs_buf.shape, xs_buf.dtype),
        input_output_aliases={2: 0},
        scratch_shapes=[pltpu.SemaphoreType.DMA],
        compiler_params=_cparams(("arbitrary",)),
        name="moe_dispatch",
    )(dest, hm, xs_buf)


def _ffn_body(grp_ref, ea_ref, eb_ref, used_ref, xs_ref, wg_ref, wu_ref, wd_ref, o_ref):
    i = pl.program_id(0)
    subs = range(FFN_BLOCKS)
    first = FFN_BLOCKS * i

    @pl.when(used_ref[first] != 0)
    def _():
        a = [ea_ref[first + s] for s in subs]
        b = [eb_ref[first + s] for s in subs]
        xm = [xs_ref[s * MOE_ROWS:(s + 1) * MOE_ROWS, :] for s in subs]
        x = [_unpack_bf16_pairs(xm[s][:, :HALF_D]).astype(BF16) for s in subs]
        g_a = [_dot(x[s], wg_ref[0, 0, a[s]]) for s in subs]
        u_a = [_dot(x[s], wu_ref[0, 0, a[s]]) for s in subs]
        g_b = [_dot(x[s], wg_ref[0, 0, b[s]]) for s in subs]
        u_b = [_dot(x[s], wu_ref[0, 0, b[s]]) for s in subs]
        meta = [lax.bitcast_convert_type(xm[s][:, HALF_D:], F32) for s in subs]
        w_a = [meta[s][:, 2:3] for s in subs]
        w_b = [meta[s][:, 3:4] for s in subs]
        act_a = [(g_a[s] * _sigmoid(g_a[s]) * u_a[s] * w_a[s]).astype(BF16) for s in subs]
        act_b = [(g_b[s] * _sigmoid(g_b[s]) * u_b[s] * w_b[s]).astype(BF16) for s in subs]
        for s in subs:
            y = _dot(act_a[s], wd_ref[0, 0, a[s]]) + _dot(act_b[s], wd_ref[0, 0, b[s]])
            o_ref[s * MOE_ROWS:(s + 1) * MOE_ROWS, :] = _pack_bf16_pairs(y)

    @pl.when(used_ref[first] == 0)
    def _():
        o_ref[...] = jnp.zeros_like(o_ref)


def _ffn(xs, grp, ea, eb, used, w_gate, w_up, w_down, l):
    P, W = xs.shape
    D = D_MODEL
    rows = MOE_ROWS * FFN_BLOCKS
    nb = P // rows
    FF = EXPERT_FF
    E = EXPERTS_PER_GROUP
    w_up_spec = pl.BlockSpec((1, 1, E, D, FF), lambda i, g, ea, eb, u: (l, g[FFN_BLOCKS * i], 0, 0, 0))
    w_dn_spec = pl.BlockSpec((1, 1, E, FF, D), lambda i, g, ea, eb, u: (l, g[FFN_BLOCKS * i], 0, 0, 0))
    grid_spec = pltpu.PrefetchScalarGridSpec(
        num_scalar_prefetch=4,
        grid=(nb,),
        in_specs=[pl.BlockSpec((rows, W), lambda i, g, ea, eb, u: (i, 0)),
                  w_up_spec, w_up_spec, w_dn_spec],
        out_specs=pl.BlockSpec((rows, HALF_D), lambda i, g, ea, eb, u: (i, 0)),
    )
    return pl.pallas_call(
        _ffn_body,
        grid_spec=grid_spec,
        out_shape=jax.ShapeDtypeStruct((P, HALF_D), jnp.uint32),
        compiler_params=_cparams(("arbitrary",)),
        name="moe_ffn",
    )(grp, ea, eb, used, xs, w_gate, w_up, w_down)


def _combine_body(dcur_ref, dnext_ref, x_ref, gt_ref, lg_ref, lb_ref, ys_ref, o_ref,
                  y0_ref, y1_ref, sems, *, tm, n_steps, alpha):
    i = pl.program_id(0)
    bufs = (y0_ref, y1_ref)

    n_chunks = tm // COMBINE_ROWS

    def finish(slot, prefetch):
        _wait_rows(tm, lambda r: _row_copy(ys_ref, 0, bufs[slot], 0, sems.at[slot]))
        gate = 1.0 + gt_ref[0]

        def chunk(c, _):
            r0 = pl.multiple_of(c * COMBINE_ROWS, COMBINE_ROWS)
            if prefetch:
                for t0 in range(0, COMBINE_ROWS, SUBLANES):
                    base = pl.multiple_of(r0 + t0, SUBLANES)
                    for u in range(SUBLANES):
                        pltpu.make_async_copy(ys_ref.at[pl.ds(dnext_ref[base + u], 1)],
                                              _tile_row(bufs[1 - slot], base, u), sems.at[1 - slot]).start()
            rows = pl.ds(r0, COMBINE_ROWS)
            r = alpha * x_ref[rows, :] + gate * _unpack_bf16_pairs(bufs[slot][rows, :])
            o_ref[rows, :] = _layer_norm(r, lg_ref[0], lb_ref[0])
            return 0

        lax.fori_loop(0, n_chunks, chunk, 0)

    @pl.when(i == 0)
    def _():
        _issue_rows(tm, lambda base, u: pltpu.make_async_copy(
            ys_ref.at[pl.ds(dcur_ref[base + u], 1)], _tile_row(bufs[0], base, u), sems.at[0]))

    has_next = i + 1 < n_steps
    for slot in range(2):
        mine = (i % 2) == slot

        @pl.when(mine & has_next)
        def _():
            finish(slot, True)

        @pl.when(mine & jnp.logical_not(has_next))
        def _():
            finish(slot, False)


def _combine(ys, dest, x2, mod_l, ln_g, ln_b, l, S, tm, alpha):
    T, D = x2.shape
    nS = S // tm
    n_steps = T // tm
    vec = pl.BlockSpec((1, 1, D), lambda i: (l, 0, 0))
    return pl.pallas_call(
        functools.partial(_combine_body, tm=tm, n_steps=n_steps, alpha=alpha),
        grid=(n_steps,),
        in_specs=[pl.BlockSpec((tm,), lambda i: (i,), memory_space=pltpu.SMEM),
                  pl.BlockSpec((tm,), lambda i: (jnp.minimum(i + 1, n_steps - 1),), memory_space=pltpu.SMEM),
                  pl.BlockSpec((tm, D), lambda i: (i, 0)),
                  pl.BlockSpec((1, 1, D), lambda i: (i // nS, 0, 5)),
                  vec, vec,
                  pl.BlockSpec(memory_space=pl.ANY)],
        out_specs=pl.BlockSpec((tm, D), lambda i: (i, 0)),
        out_shape=jax.ShapeDtypeStruct((T, D), F32),
        scratch_shapes=[pltpu.VMEM((tm, HALF_D), jnp.uint32), pltpu.VMEM((tm, HALF_D), jnp.uint32),
                        pltpu.SemaphoreType.DMA((2,))],
        compiler_params=_cparams(("arbitrary",)),
        name="moe_combine_ln",
    )(dest, dest, x2, mod_l, ln_g, ln_b, ys)


def _prep_w_in(w_in):
    L, D, _ = w_in.shape
    o = 0
    cq = w_in[..., o:o + MLA_Q_RANK]; o += MLA_Q_RANK
    ckv = w_in[..., o:o + MLA_KV_RANK]; o += MLA_KV_RANK
    kr = w_in[..., o:o + MLA_ROPE]; o += MLA_ROPE
    gq = w_in[..., o:o + GDN_WIDTH]; o += GDN_WIDTH
    gk = w_in[..., o:o + GDN_WIDTH]; o += GDN_WIDTH
    gv = w_in[..., o:o + GDN_WIDTH]; o += GDN_WIDTH
    gz = w_in[..., o:o + GDN_WIDTH]; o += GDN_WIDTH
    ga = w_in[..., o:o + GDN_HEADS]; o += GDN_HEADS
    gb = w_in[..., o:o + GDN_HEADS]; o += GDN_HEADS
    pw = w_in[..., o:o + POOL_WIDTH]
    half = MLA_ROPE // 2
    z = lambda n: jnp.zeros((L, D, n), w_in.dtype)
    kra = jnp.concatenate([z(MLA_NOPE), kr, z(HEAD_PAD - MLA_NOPE - MLA_ROPE)], axis=-1)
    krb = jnp.concatenate([ga, gb, z(MLA_NOPE - 2 * GDN_HEADS), kr[..., half:], kr[..., :half],
                           z(HEAD_PAD - MLA_NOPE - MLA_ROPE)], axis=-1)
    out = jnp.concatenate([gq, gk, gv, gz, cq, pw, ckv, kra, krb], axis=-1)
    assert out.shape[-1] == IN_COLS
    return out.astype(BF16)


def _prep_mla(w_uq, w_ukv):
    L = w_uq.shape[0]
    H, half = MLA_HEADS, MLA_ROPE // 2
    pad = HEAD_PAD - MLA_NOPE - MLA_ROPE
    q = w_uq.reshape(L, MLA_Q_RANK, H, MLA_NOPE + MLA_ROPE)
    nope, r1, r2 = q[..., :MLA_NOPE], q[..., MLA_NOPE:MLA_NOPE + half], q[..., MLA_NOPE + half:]
    zq = lambda n: jnp.zeros((L, MLA_Q_RANK, H, n), w_uq.dtype)
    plain = jnp.concatenate([nope, r1, r2, zq(pad)], axis=-1).reshape(L, MLA_Q_RANK, H * HEAD_PAD)
    partner = jnp.concatenate([zq(MLA_NOPE), r2, r1, zq(pad)], axis=-1).reshape(L, MLA_Q_RANK, H * HEAD_PAD)
    wq2 = jnp.concatenate([plain, partner], axis=-1).astype(BF16)
    kv = w_ukv.reshape(L, MLA_KV_RANK, H, MLA_NOPE + MLA_V)
    k_nope, v = kv[..., :MLA_NOPE], kv[..., MLA_NOPE:]
    zk = lambda n: jnp.zeros((L, MLA_KV_RANK, H, n), w_ukv.dtype)
    k_main = jnp.concatenate([k_nope, zk(HEAD_PAD - MLA_NOPE)], axis=-1).reshape(L, MLA_KV_RANK, H * HEAD_PAD)
    even = (jnp.arange(H) % 2 == 0)[None, None, :, None]
    v_pair = jnp.where(even, jnp.concatenate([v, zk(MLA_V)], axis=-1), jnp.concatenate([zk(MLA_V), v], axis=-1))
    wkv2 = jnp.concatenate([k_main, v_pair.reshape(L, MLA_KV_RANK, H * HEAD_PAD)], axis=-1).astype(BF16)
    return wq2, wkv2


def _rope_tables(S):
    half = MLA_ROPE // 2
    inv_freq = jnp.power(ROPE_THETA, -jnp.arange(0, MLA_ROPE, 2, dtype=F32) / MLA_ROPE)
    ang = jnp.arange(S, dtype=F32)[:, None] * inv_freq[None, :]
    cos, sin = jnp.cos(ang), jnp.sin(ang)
    pad = jnp.zeros((S, HEAD_PAD - MLA_NOPE - MLA_ROPE), F32)
    cos_t = jnp.concatenate([jnp.ones((S, MLA_NOPE), F32), cos, cos, pad], axis=-1)
    sin_t = jnp.concatenate([jnp.zeros((S, MLA_NOPE), F32), -sin, sin, pad], axis=-1)
    scale = (MLA_NOPE + MLA_ROPE) ** -0.5
    return cos_t * scale, sin_t * scale, cos_t, sin_t


def _block_diag(blocks):
    L, G, n, _ = blocks.shape
    eye = jnp.eye(G, dtype=blocks.dtype)
    return jnp.einsum('lgij,gh->lgihj', blocks, eye).reshape(L, G * n, G * n)


def kernel(x, c, w_in, mla_q_norm, mla_kv_norm, mla_w_uq, mla_w_ukv, gdn_conv, gdn_a_log, gdn_dt_bias, gdn_out_norm, pool_w, pool_scale, w_out, w_mod, b_mod, ln1_g, ln1_b, ln2_g, ln2_b, router_w_group, router_b_group, router_w_expert, router_b_expert, moe_w_gate, moe_w_up, moe_w_down):
    B, S, D = x.shape
    L = w_in.shape[0]
    T = B * S
    alpha = (2 * L) ** 0.25
    ts = min(512, S)
    tq = min(256, S)
    t_moe = min(512, S)
    t_disp = min(1024, S)
    assert D == D_MODEL and S % ts == 0 and S % tq == 0 and ts % CHUNK == 0
    assert S % t_disp == 0 and t_moe % DMA_WAIT_UNROLL == 0

    w_in2 = _prep_w_in(w_in)
    wq2, wkv2 = _prep_mla(mla_w_uq, mla_w_ukv)
    tabs = _rope_tables(S)
    gq = mla_q_norm.reshape(L, 1, MLA_Q_RANK)
    gkv = mla_kv_norm.reshape(L, 1, MLA_KV_RANK)
    alog_e = jnp.repeat(gdn_a_log, GDN_DIM, axis=-1).reshape(L, 1, GDN_WIDTH)
    dt_e = jnp.repeat(gdn_dt_bias, GDN_DIM, axis=-1).reshape(L, 1, GDN_WIDTH)
    og_e = jnp.tile(gdn_out_norm, (1, GDN_HEADS)).reshape(L, 1, GDN_WIDTH)
    lane_head = jnp.arange(GDN_WIDTH) // GDN_DIM
    seg = (lane_head[:, None] == lane_head[None, :]).astype(BF16)
    tri_c = (jnp.arange(CHUNK)[:, None] >= jnp.arange(CHUNK)[None, :]).astype(BF16)
    src_lane = jnp.arange(LANE)[:, None]
    out_col = jnp.arange(2 * GDN_WIDTH)[None, :]
    ab_expand = (src_lane == (out_col // GDN_WIDTH) * GDN_HEADS + (out_col % GDN_WIDTH) // GDN_DIM).astype(BF16)
    pool_bd = _block_diag(pool_w).astype(BF16)
    pool_sc = pool_scale.reshape(L, 1, POOL_WIDTH)
    w_out_b = w_out.astype(BF16)
    w_r = jnp.concatenate([router_w_group, router_w_expert,
                           jnp.zeros((L, D, LANE - N_GROUPS - N_EXPERTS), F32)], axis=-1)
    w_r_hi = w_r.astype(BF16)
    w_r_lo = (w_r - w_r_hi.astype(F32)).astype(BF16)
    b_r = jnp.concatenate([router_b_group, router_b_expert,
                           jnp.zeros((L, LANE - N_GROUPS - N_EXPERTS), F32)], axis=-1).reshape(L, 1, LANE)
    tri_r = (jnp.arange(ts)[:, None] > jnp.arange(ts)[None, :]).astype(BF16)
    ln1g, ln1b = ln1_g.reshape(L, 1, D), ln1_b.reshape(L, 1, D)
    ln2g, ln2b = ln2_g.reshape(L, 1, D), ln2_b.reshape(L, 1, D)

    assert N_BUCKETS <= LANE
    step_rows = MOE_ROWS * FFN_BLOCKS
    max_rows = T + N_BUCKETS * (MOE_ROWS - 1) + N_GROUPS * (step_rows - MOE_ROWS)
    P = (max_rows + step_rows - 1) // step_rows * step_rows
    nb = P // MOE_ROWS
    pg, pa, pb = [], [], []
    for g_ in range(N_GROUPS):
        for a_ in range(EXPERTS_PER_GROUP):
            for b_ in range(a_ + 1, EXPERTS_PER_GROUP):
                pg.append(g_)
                pa.append(a_)
                pb.append(b_)
    bucket_g = jnp.asarray(pg, jnp.int32)
    bucket_a = jnp.asarray(pa, jnp.int32)
    bucket_b = jnp.asarray(pb, jnp.int32)
    grouped = lambda w: w.astype(BF16).reshape((L, N_GROUPS, EXPERTS_PER_GROUP) + w.shape[2:])
    wg_b, wu_b, wd_b = grouped(moe_w_gate), grouped(moe_w_up), grouped(moe_w_down)
    bucket_ids = jnp.arange(N_BUCKETS, dtype=jnp.int32)
    blk0 = jnp.arange(nb, dtype=jnp.int32) * MOE_ROWS
    xs = jnp.zeros((P, HALF_D + LANE), jnp.uint32)

    mod = _modulation(c, w_mod, b_mod)
    x2 = x.reshape(T, D)
    for l in range(L):
        mod_l = mod[l].reshape(B, 1, 6 * D)
        proj = _inproj(x2, mod_l, w_in2, l, S, ts)
        q, k, v = _mla_proj(proj, tabs, gq, gkv, wq2, wkv2, l, S, ts)
        y_mla = _attention(q, k, v, B, S, tq)
        y_gdn = _gdn(proj, ab_expand, gdn_conv, alog_e, dt_e, og_e, seg, tri_c, l, B, S, ts)
        y_pool = _pool(proj, pool_bd, pool_sc, l, B, S, ts)
        x2 = _outproj(y_mla, y_gdn, y_pool, w_out_b, x2, mod_l, ln1g, ln1b, l, S, ts, alpha)

        hm, cnt = _router(x2, mod_l, w_r_hi, w_r_lo, b_r, tri_r, l, S, ts)
        counts = cnt[0, :N_BUCKETS].astype(jnp.int32)
        padded = ((counts + MOE_ROWS - 1) // MOE_ROWS * MOE_ROWS).reshape(N_GROUPS, PAIRS_PER_GROUP)
        group_rows = jnp.sum(padded, axis=1)
        group_pad = (step_rows - group_rows % step_rows) % step_rows
        padded = padded.at[:, PAIRS_PER_GROUP - 1].add(group_pad).reshape(N_BUCKETS)
        pad_end = jnp.cumsum(padded)
        pad_start = pad_end - padded
        route = lax.bitcast_convert_type(hm[:, HALF_D:HALF_D + 2], F32).astype(jnp.int32)
        bucket, rank = route[:, 0], route[:, 1]
        dest = jnp.sum(jnp.where(bucket[:, None] == bucket_ids[None, :], pad_start[None, :], 0), axis=1) + rank
        block_bucket = jnp.minimum(jnp.sum((pad_end[None, :] <= blk0[:, None]).astype(jnp.int32), axis=1),
                                   N_BUCKETS - 1)
        onehot_b = (block_bucket[:, None] == bucket_ids[None, :]).astype(jnp.int32)
        grp = jnp.sum(onehot_b * bucket_g[None, :], axis=1)
        ea = jnp.sum(onehot_b * bucket_a[None, :], axis=1)
        eb = jnp.sum(onehot_b * bucket_b[None, :], axis=1)
        used = (blk0 < pad_end[-1]).astype(jnp.int32)

        xs = _dispatch(hm, dest, xs, t_disp)
        ys = _ffn(xs, grp, ea, eb, used, wg_b, wu_b, wd_b, l)
        x2 = _combine(ys, dest, x2, mod_l, ln2g, ln2b, l, S, t_moe, alpha)
    return x2.reshape(B, S, D)
```
